```python
import math
import jax
import jax.numpy as jnp
from jax import lax
import numpy as np

D_MODEL = 1024
BATCH = 2
SEQ = 8192
DEPTH = 2
DEC_BATCH = 32
DEC_SEQ = 4
PAST_LEN = 16384
PAGE_SIZE = 128

N_MIXERS = 2
NSA_HEADS = 16
NSA_KV_HEADS = 4
NSA_HEAD_DIM = D_MODEL // NSA_HEADS
NSA_GROUP = NSA_HEADS // NSA_KV_HEADS
NSA_KV_W = NSA_KV_HEADS * NSA_HEAD_DIM
CMP_BLOCK = 32
CMP_STRIDE = 16
CMP_SPAN = CMP_BLOCK // CMP_STRIDE
CMP_HIDDEN = NSA_HEAD_DIM
SEL_BLOCK = 64
SEL_CHUNKS = SEL_BLOCK // CMP_STRIDE
SEL_TOPN = 16
WINDOW = 512
Q_BLOCK = 128
DIFF_HEADS = 8
DIFF_KV_HEADS = 4
DIFF_GROUP = DIFF_HEADS // DIFF_KV_HEADS
DIFF_HEAD_DIM = D_MODEL // (2 * DIFF_HEADS)
DIFF_V_DIM = 2 * DIFF_HEAD_DIM
NUM_BUCKETS = 32
MAX_DISTANCE = 128
BIAS_COLS = NSA_HEADS
N_EXPERTS = 16
N_EXPERT_GROUPS = 4
EXPERTS_PER_GROUP = N_EXPERTS // N_EXPERT_GROUPS
TOP_K = 2
D_EXPERT = 1024
MOE_BLOCK = 128
DN_ALPHA = (2 * DEPTH) ** 0.25
DN_BETA = (8 * DEPTH) ** -0.25
LN_EPS = 1e-5

kernel_name = "nsa_diffattn_groupmoe_decoder_step"


def layer_norm(x, g, b):
    xf = x.astype(jnp.float32)
    mu = xf.mean(-1, keepdims=True)
    var = jnp.square(xf - mu).mean(-1, keepdims=True)
    return ((xf - mu) * lax.rsqrt(var + LN_EPS) * g + b).astype(x.dtype)


def masked_softmax(logits, mask):
    logits = jnp.where(mask, logits.astype(jnp.float32), -jnp.inf)
    m = jnp.max(logits, -1, keepdims=True)
    m = jnp.where(jnp.isfinite(m), m, 0.0)
    e = jnp.exp(logits - m)
    s = e.sum(-1, keepdims=True)
    return e / jnp.where(s > 0, s, 1.0)


def t5_bucket(dist):
    n = jnp.maximum(dist, 0)
    max_exact = NUM_BUCKETS // 2
    nf = jnp.maximum(n, 1).astype(jnp.float32)
    large = max_exact + (jnp.log(nf / max_exact) / math.log(MAX_DISTANCE / max_exact)
                         * (NUM_BUCKETS - max_exact)).astype(jnp.int32)
    large = jnp.minimum(large, NUM_BUCKETS - 1)
    return jnp.where(n < max_exact, n, large)


def pad_rows(x, mult):
    L = x.shape[0]
    Lp = -(-L // mult) * mult
    return jnp.pad(x, ((0, Lp - L),) + ((0, 0),) * (x.ndim - 1))


def compress(rows, w1, b1, w2, b2):
    L = rows.shape[0]
    n_chunks = -(-L // CMP_STRIDE)
    rows = jnp.pad(rows, ((0, n_chunks * CMP_STRIDE - L), (0, 0), (0, 0)))
    chunks = rows.reshape(n_chunks, CMP_STRIDE, NSA_KV_HEADS, NSA_HEAD_DIM)
    w1r = w1.reshape(CMP_SPAN, CMP_STRIDE, NSA_HEAD_DIM, CMP_HIDDEN)
    part = jnp.einsum('cskd,rsdh->rckh', chunks, w1r)
    nb = n_chunks - CMP_SPAN + 1
    pre = sum(part[r, r:r + nb] for r in range(CMP_SPAN))
    h = jax.nn.gelu(pre + b1)
    out = jnp.einsum('nkh,hd->nkd', h, w2) + b2
    end = jnp.arange(nb) * CMP_STRIDE + (CMP_BLOCK - 1)
    return out, end


def nsa_split(proj):
    lead = proj.shape[:-1]
    hq = NSA_HEADS * NSA_HEAD_DIM
    hkv = 6 * NSA_KV_W
    q = proj[..., :hq].reshape(lead + (NSA_HEADS, NSA_HEAD_DIM)) * (NSA_HEAD_DIM ** -0.5)
    kv = proj[..., hq:hq + hkv].reshape(lead + (3, 2, NSA_KV_HEADS, NSA_HEAD_DIM))
    gates = jax.nn.sigmoid(proj[..., hq + hkv:].astype(jnp.float32)).astype(proj.dtype)
    gates = gates.reshape(lead + (NSA_HEADS, 3))
    return q, kv[..., 0, :, :, :], kv[..., 1, :, :, :], kv[..., 2, :, :, :], gates


def nsa_attend(q, q_pos, gates, ck, cv, c_end, s_kv, w_kv, w_pos, rel_bias):
    Tq = q.shape[0]
    G, J, DH = NSA_KV_HEADS, NSA_GROUP, NSA_HEAD_DIM
    qg = q.reshape(Tq, G, J, DH)
    lc = jnp.einsum('tgjd,ngd->tgjn', qg, ck)
    pc = masked_softmax(lc, (c_end[None, :] <= q_pos[:, None])[:, None, None, :])
    o_cmp = jnp.einsum('tgjn,ngd->tgjd', pc.astype(cv.dtype), cv)
    ps = pc.sum(2)
    chunk = sum(jnp.pad(ps, ((0, 0), (0, 0), (r, CMP_SPAN - 1 - r))) for r in range(CMP_SPAN))
    n_sb = s_kv.shape[0] // SEL_BLOCK
    chunk = jnp.pad(chunk, ((0, 0), (0, 0), (0, n_sb * SEL_CHUNKS - chunk.shape[-1])))
    score = chunk.reshape(Tq, G, n_sb, SEL_CHUNKS).sum(-1)
    blk = jnp.arange(n_sb)[None, :]
    cur = (q_pos // SEL_BLOCK)[:, None]
    valid = blk * SEL_BLOCK <= q_pos[:, None]
    forced = (blk == 0) | (blk == cur) | (blk == cur - 1)
    score = jnp.where(forced[:, None, :], jnp.inf, jnp.where(valid[:, None, :], score, -jnp.inf))
    n_sel = min(SEL_TOPN, n_sb)
    _, idx = lax.top_k(score, n_sel)
    s_blocks = s_kv.reshape(n_sb, SEL_BLOCK, 2, G, DH).transpose(3, 0, 1, 2, 4)
    gathered = s_blocks[jnp.arange(G)[None, :, None], idx]
    M = n_sel * SEL_BLOCK
    k_sel = gathered[..., 0, :].reshape(Tq, G, M, DH)
    v_sel = gathered[..., 1, :].reshape(Tq, G, M, DH)
    k_pos = (idx[..., None] * SEL_BLOCK + jnp.arange(SEL_BLOCK)).reshape(Tq, G, M)
    dist = q_pos[:, None, None] - k_pos
    bias_tab = rel_bias.T.reshape(G, J, NUM_BUCKETS)
    bias = bias_tab[jnp.arange(G)[None, :, None, None], jnp.arange(J)[None, None, :, None],
                    t5_bucket(dist)[:, :, None, :]]
    ls = jnp.einsum('tgjd,tgmd->tgjm', qg, k_sel).astype(jnp.float32) + bias
    p_sel = masked_softmax(ls, (dist >= 0)[:, :, None, :])
    o_slc = jnp.einsum('tgjm,tgmd->tgjd', p_sel.astype(v_sel.dtype), v_sel)
    Lw = w_kv.shape[0]
    wk, wv = w_kv[:, 0], w_kv[:, 1]
    dw = q_pos[:, None] - w_pos[None, :]
    mw = (dw >= 0) & (dw < WINDOW) & (w_pos >= 0)[None, :]
    bw = rel_bias[t5_bucket(dw)].reshape(Tq, Lw, G, J).transpose(0, 2, 3, 1)
    lw = jnp.einsum('tgjd,lgd->tgjl', qg, wk).astype(jnp.float32) + bw
    pw = masked_softmax(lw, mw[:, None, None, :])
    o_win = jnp.einsum('tgjl,lgd->tgjd', pw.astype(wv.dtype), wv)
    g = gates.reshape(Tq, G, J, 3)
    out = g[..., 0:1] * o_cmp + g[..., 1:2] * o_slc + g[..., 2:3] * o_win
    return out.reshape(Tq, NSA_HEADS * DH)


def nsa_prompt_seq(q, gates, c_kv, s_kv, w_kv, rel_bias, cmp_k, cmp_v):
    S = q.shape[0]
    ck, c_end = compress(c_kv[:, 0], *cmp_k)
    cv, _ = compress(c_kv[:, 1], *cmp_v)
    s_pad = pad_rows(s_kv, SEL_BLOCK)
    w_pad = jnp.pad(w_kv, ((WINDOW, 0), (0, 0), (0, 0), (0, 0)))

    def block(i):
        start = i * Q_BLOCK
        q_pos = start + jnp.arange(Q_BLOCK)
        w_pos = start - WINDOW + jnp.arange(WINDOW + Q_BLOCK)
        return nsa_attend(lax.dynamic_slice_in_dim(q, start, Q_BLOCK), q_pos,
                          lax.dynamic_slice_in_dim(gates, start, Q_BLOCK), ck, cv, c_end, s_pad,
                          lax.dynamic_slice_in_dim(w_pad, start, WINDOW + Q_BLOCK), w_pos, rel_bias)

    return lax.map(block, jnp.arange(S // Q_BLOCK)).reshape(S, NSA_HEADS * NSA_HEAD_DIM)


def nsa_sample_seq(q, gates, c_kv, s_kv, w_kv, pt_row, win_buf, cache_cmp_kv, cache_slc_kv,
                   rel_bias, cmp_k, cmp_v):
    T = q.shape[0]
    past = pt_row.shape[0] * PAGE_SIZE
    row_shape = (past, 2, NSA_KV_HEADS, NSA_HEAD_DIM)
    c_all = jnp.concatenate([cache_cmp_kv[pt_row].reshape(row_shape), c_kv], 0)
    s_all = jnp.concatenate([cache_slc_kv[pt_row].reshape(row_shape), s_kv], 0)
    ck, c_end = compress(c_all[:, 0], *cmp_k)
    cv, _ = compress(c_all[:, 1], *cmp_v)
    W = win_buf.shape[0]
    w_all = jnp.concatenate([win_buf, w_kv], 0)
    w_pos = past - W + jnp.arange(W + T)
    q_pos = past + jnp.arange(T)
    out = nsa_attend(q, q_pos, gates, ck, cv, c_end, pad_rows(s_all, SEL_BLOCK), w_all, w_pos, rel_bias)
    return out, w_all[T:]


def diff_split(proj):
    lead = proj.shape[:-1]
    hq = DIFF_HEADS * 2 * DIFF_HEAD_DIM
    q = proj[..., :hq].reshape(lead + (DIFF_HEADS, 2, DIFF_HEAD_DIM)) * (DIFF_HEAD_DIM ** -0.5)
    kv = proj[..., hq:].reshape(lead + (2, DIFF_KV_HEADS, DIFF_V_DIM))
    return q, kv


def diff_attend(q, q_pos, kv, k_pos, lam, rel_bias):
    Tq, Lk = q.shape[0], kv.shape[0]
    G, J = DIFF_KV_HEADS, DIFF_GROUP
    qg = q.reshape(Tq, G, J, 2, DIFF_HEAD_DIM)
    k = kv[:, 0].reshape(Lk, G, 2, DIFF_HEAD_DIM)
    v = kv[:, 1]
    dist = q_pos[:, None] - k_pos[None, :]
    bias = rel_bias[t5_bucket(dist)].reshape(Tq, Lk, G, J, 2).transpose(0, 2, 3, 4, 1)
    logits = jnp.einsum('tgjmd,lgmd->tgjml', qg, k).astype(jnp.float32) + bias
    p = masked_softmax(logits, (dist >= 0)[:, None, None, None, :])
    a = p[:, :, :, 0] - lam * p[:, :, :, 1]
    o = jnp.einsum('tgjl,lge->tgje', a.astype(v.dtype), v)
    return o.reshape(Tq, DIFF_HEADS, DIFF_V_DIM)


def diff_prompt_seq(q, kv, lam, rel_bias):
    S = q.shape[0]
    k_pos = jnp.arange(S)

    def block(i):
        start = i * Q_BLOCK
        return diff_attend(lax.dynamic_slice_in_dim(q, start, Q_BLOCK), start + jnp.arange(Q_BLOCK),
                           kv, k_pos, lam, rel_bias)

    return lax.map(block, jnp.arange(S // Q_BLOCK)).reshape(S, DIFF_HEADS, DIFF_V_DIM)


def diff_sample_seq(q, kv_new, pt_row, cache_diff_kv, lam, rel_bias):
    T = q.shape[0]
    past = pt_row.shape[0] * PAGE_SIZE
    kv = jnp.concatenate([cache_diff_kv[pt_row].reshape((past,) + cache_diff_kv.shape[2:]), kv_new], 0)
    return diff_attend(q, past + jnp.arange(T), kv, jnp.arange(past + T), lam, rel_bias)


def diff_merge(o, gain, lam_init):
    of = o.astype(jnp.float32)
    of = of * lax.rsqrt(jnp.mean(of * of, -1, keepdims=True) + LN_EPS) * gain * (1.0 - lam_init)
    return of.astype(o.dtype).reshape(o.shape[:-2] + (DIFF_HEADS * DIFF_V_DIM,))


def moe(x, router_w, router_b, w_gu, w_dn):
    N, D = x.shape
    logits = jnp.dot(x, router_w).astype(jnp.float32) + router_b.astype(jnp.float32)
    probs = jax.nn.softmax(logits, -1)
    grp_score = lax.top_k(probs.reshape(N, N_EXPERT_GROUPS, EXPERTS_PER_GROUP), TOP_K)[0].sum(-1)
    grp = jnp.argmax(grp_score, -1)
    in_grp = (jnp.arange(N_EXPERTS) // EXPERTS_PER_GROUP)[None, :] == grp[:, None]
    top_logit, top_idx = lax.top_k(jnp.where(in_grp, logits, -jnp.inf), TOP_K)
    gate = jax.nn.softmax(top_logit, -1)
    A = N * TOP_K
    flat_e = top_idx.reshape(A)
    order = jnp.argsort(flat_e)
    e_sorted = flat_e[order]
    tok = order // TOP_K
    sizes = jnp.bincount(flat_e, length=N_EXPERTS)
    start = jnp.cumsum(sizes) - sizes
    padded = (sizes + MOE_BLOCK - 1) // MOE_BLOCK * MOE_BLOCK
    pad_end = jnp.cumsum(padded)
    pad_start = pad_end - padded
    dest = pad_start[e_sorted] + jnp.arange(A) - start[e_sorted]
    n_blocks = -(-A // MOE_BLOCK) + N_EXPERTS
    buf = jnp.zeros((n_blocks * MOE_BLOCK, D), x.dtype).at[dest].set(x[tok])
    blk_e = jnp.minimum(jnp.searchsorted(pad_end, jnp.arange(n_blocks) * MOE_BLOCK, side='right'),
                        N_EXPERTS - 1)

    def expert_block(args):
        xb, e = args
        h = xb @ w_gu[e]
        h = jax.nn.silu(h[:, :D_EXPERT]) * h[:, D_EXPERT:]
        return h @ w_dn[e]

    out = lax.map(expert_block, (buf.reshape(n_blocks, MOE_BLOCK, D), blk_e)).reshape(n_blocks * MOE_BLOCK, D)
    contrib = out[dest] * gate.reshape(A)[order][:, None].astype(x.dtype)
    return jnp.zeros_like(x).at[tok].add(contrib)


def setup_inputs(seed: int = 0) -> dict:
    key = jax.random.key(seed)
    ks = iter(jax.random.split(key, 48))

    def nrm(shape, scale):
        return jax.random.normal(next(ks), shape, jnp.float32) * scale

    n_pages = PAST_LEN // PAGE_SIZE
    used = DEC_BATCH * n_pages
    pool = used + used // 4
    page_table = jax.random.permutation(next(ks), pool)[:used].reshape(DEC_BATCH, n_pages).astype(jnp.int32)
    sd = D_MODEL ** -0.5
    hq = NSA_HEADS * NSA_HEAD_DIM
    nsa_w_in = jnp.concatenate([
        nrm((D_MODEL, hq), sd),
        nrm((D_MODEL, NSA_KV_W), sd), nrm((D_MODEL, NSA_KV_W), sd * DN_BETA),
        nrm((D_MODEL, NSA_KV_W), sd), nrm((D_MODEL, NSA_KV_W), sd * DN_BETA),
        nrm((D_MODEL, NSA_KV_W), sd), nrm((D_MODEL, NSA_KV_W), sd * DN_BETA),
        nrm((D_MODEL, 3 * NSA_HEADS), sd)], axis=1)
    dq = DIFF_HEADS * 2 * DIFF_HEAD_DIM
    dkv = DIFF_KV_HEADS * DIFF_V_DIM
    diff_w_in = jnp.concatenate([nrm((D_MODEL, dq), sd), nrm((D_MODEL, dkv), sd),
                                 nrm((D_MODEL, dkv), sd * DN_BETA)], axis=1)
    w1s = (CMP_BLOCK * NSA_HEAD_DIM) ** -0.5
    return {
        "x_prompt": nrm((BATCH, SEQ, D_MODEL), 1.0),
        "x_sample": nrm((DEC_BATCH, DEC_SEQ, D_MODEL), 1.0),
        "cache_cmp_kv": nrm((pool, PAGE_SIZE, 2, NSA_KV_HEADS, NSA_HEAD_DIM), 1.0),
        "cache_slc_kv": nrm((pool, PAGE_SIZE, 2, NSA_KV_HEADS, NSA_HEAD_DIM), 1.0),
        "state_win_kv": nrm((DEC_BATCH, min(WINDOW, PAST_LEN), 2, NSA_KV_HEADS, NSA_HEAD_DIM), 1.0),
        "cache_diff_kv": nrm((pool, PAGE_SIZE, 2, DIFF_KV_HEADS, DIFF_V_DIM), 1.0),
        "page_table": page_table,
        "nsa_w_in": nsa_w_in,
        "nsa_w_out": nrm((hq, D_MODEL), hq ** -0.5 * DN_BETA),
        "cmp_k_w1": nrm((CMP_BLOCK, NSA_HEAD_DIM, CMP_HIDDEN), w1s),
        "cmp_k_b1": nrm((CMP_HIDDEN,), 0.02),
        "cmp_k_w2": nrm((CMP_HIDDEN, NSA_HEAD_DIM), CMP_HIDDEN ** -0.5),
        "cmp_k_b2": nrm((NSA_HEAD_DIM,), 0.02),
        "cmp_v_w1": nrm((CMP_BLOCK, NSA_HEAD_DIM, CMP_HIDDEN), w1s),
        "cmp_v_b1": nrm((CMP_HIDDEN,), 0.02),
        "cmp_v_w2": nrm((CMP_HIDDEN, NSA_HEAD_DIM), CMP_HIDDEN ** -0.5),
        "cmp_v_b2": nrm((NSA_HEAD_DIM,), 0.02),
        "diff_w_in": diff_w_in,
        "diff_w_out": nrm((DIFF_HEADS * DIFF_V_DIM, D_MODEL), (DIFF_HEADS * DIFF_V_DIM) ** -0.5 * DN_BETA),
        "lambda_q1": nrm((DIFF_HEAD_DIM,), 0.1),
        "lambda_k1": nrm((DIFF_HEAD_DIM,), 0.1),
        "lambda_q2": nrm((DIFF_HEAD_DIM,), 0.1),
        "lambda_k2": nrm((DIFF_HEAD_DIM,), 0.1),
        "diff_subln_gain": 1.0 + nrm((DIFF_V_DIM,), 0.02),
        "rel_bias": nrm((NUM_BUCKETS, BIAS_COLS), 0.5),
        "ln_gain": 1.0 + nrm((DEPTH, 2, D_MODEL), 0.02),
        "ln_bias": nrm((DEPTH, 2, D_MODEL), 0.02),
        "router_w": nrm((D_MODEL, N_EXPERTS), sd),
        "router_b": nrm((N_EXPERTS,), 0.01),
        "moe_w_gate_up": nrm((DEPTH, N_EXPERTS, D_MODEL, 2 * D_EXPERT), sd * DN_BETA),
        "moe_w_down": nrm((DEPTH, N_EXPERTS, D_EXPERT, D_MODEL), D_EXPERT ** -0.5 * DN_BETA),
    }


def reference(x_prompt, x_sample, cache_cmp_kv, cache_slc_kv, state_win_kv, cache_diff_kv, page_table,
              nsa_w_in, nsa_w_out, cmp_k_w1, cmp_k_b1, cmp_k_w2, cmp_k_b2,
              cmp_v_w1, cmp_v_b1, cmp_v_w2, cmp_v_b2, diff_w_in, diff_w_out,
              lambda_q1, lambda_k1, lambda_q2, lambda_k2, diff_subln_gain, rel_bias,
              ln_gain, ln_bias, router_w, router_b, moe_w_gate_up, moe_w_down):
    cmp_k = (cmp_k_w1, cmp_k_b1, cmp_k_w2, cmp_k_b2)
    cmp_v = (cmp_v_w1, cmp_v_b1, cmp_v_w2, cmp_v_b2)
    xp, xs = x_prompt, x_sample
    S = xp.shape[1]
    for layer in range(DEPTH):
        if layer % N_MIXERS == 0:
            qp, cmp_rows_p, slc_rows_p, win_rows_p, gp = nsa_split(xp @ nsa_w_in)
            op = jax.vmap(nsa_prompt_seq, in_axes=(0, 0, 0, 0, 0, None, None, None))(
                qp, gp, cmp_rows_p, slc_rows_p, win_rows_p, rel_bias, cmp_k, cmp_v)
            win_p = win_rows_p[:, S - min(WINDOW, S):]
            qs, cmp_rows_s, slc_rows_s, win_rows_s, gs = nsa_split(xs @ nsa_w_in)

            def nsa_one(a):
                return nsa_sample_seq(a[0], a[1], a[2], a[3], a[4], a[5], a[6],
                                      cache_cmp_kv, cache_slc_kv, rel_bias, cmp_k, cmp_v)

            os_, win_s = lax.map(nsa_one, (qs, gs, cmp_rows_s, slc_rows_s, win_rows_s, page_table, state_win_kv))
            mix_p = op @ nsa_w_out
            mix_s = os_ @ nsa_w_out
        else:
            lam_init = 0.8 - 0.6 * math.exp(-0.3 * layer)
            lam = (jnp.exp(jnp.sum(lambda_q1 * lambda_k1).astype(jnp.float32))
                   - jnp.exp(jnp.sum(lambda_q2 * lambda_k2).astype(jnp.float32)) + lam_init)
            qp, diff_rows_p = diff_split(xp @ diff_w_in)
            op = jax.vmap(diff_prompt_seq, in_axes=(0, 0, None, None))(qp, diff_rows_p, lam, rel_bias)
            qs, diff_rows_s = diff_split(xs @ diff_w_in)

            def diff_one(a):
                return diff_sample_seq(a[0], a[1], a[2], cache_diff_kv, lam, rel_bias)

            os_ = lax.map(diff_one, (qs, diff_rows_s, page_table))
            mix_p = diff_merge(op, diff_subln_gain, lam_init) @ diff_w_out
            mix_s = diff_merge(os_, diff_subln_gain, lam_init) @ diff_w_out
        xp = layer_norm(DN_ALPHA * xp + mix_p, ln_gain[layer, 0], ln_bias[layer, 0])
        xs = layer_norm(DN_ALPHA * xs + mix_s, ln_gain[layer, 0], ln_bias[layer, 0])
        fp = moe(xp.reshape(-1, D_MODEL), router_w, router_b, moe_w_gate_up[layer], moe_w_down[layer])
        fs = moe(xs.reshape(-1, D_MODEL), router_w, router_b, moe_w_gate_up[layer], moe_w_down[layer])
        xp = layer_norm(DN_ALPHA * xp + fp.reshape(xp.shape), ln_gain[layer, 1], ln_bias[layer, 1])
        xs = layer_norm(DN_ALPHA * xs + fs.reshape(xs.shape), ln_gain[layer, 1], ln_bias[layer, 1])
    return (xp, xs, cmp_rows_p, cmp_rows_s, slc_rows_p, slc_rows_s, win_p, win_s, diff_rows_p, diff_rows_s)
```

```python
import functools
import math

import jax
import jax.numpy as jnp
import numpy as np
from jax import lax
from jax.experimental import pallas as pl
from jax.experimental.pallas import tpu as pltpu

F32 = jnp.float32
BF16 = jnp.bfloat16
I32 = jnp.int32
HI = lax.Precision.HIGHEST
NT_DIMS = (((1,), (1,)), ((), ()))

NSA_HEADS = 16
NSA_KV = 4
NSA_J = NSA_HEADS // NSA_KV
NSA_DH = 64
NSA_KVW = NSA_KV * NSA_DH
CMP_BLOCK = 32
CMP_STRIDE = 16
CMP_SPAN = CMP_BLOCK // CMP_STRIDE
SEL_BLOCK = 64
SEL_CHUNKS = SEL_BLOCK // CMP_STRIDE
SEL_TOPN = 16
WINDOW = 512
QB = 128
DIFF_HEADS = 8
DIFF_KV = 4
DIFF_J = DIFF_HEADS // DIFF_KV
DIFF_DH = 64
DIFF_DV = 2 * DIFF_DH
NUM_BUCKETS = 32
MAX_DISTANCE = 128
N_EXPERTS = 16
N_GROUPS = 4
EPG = N_EXPERTS // N_GROUPS
TOP_K = 2
DEPTH = 2
DN_ALPHA = (2 * DEPTH) ** 0.25
LN_EPS = 1e-5

PAGE = 128
KT = 512
PADF = 512
MOE_TM = 512
NEG = -1e30
BIG = 1e30
VMEM_LIMIT = 56 * 1024 * 1024


def _cparams(sem):
    return pltpu.CompilerParams(dimension_semantics=sem, vmem_limit_bytes=VMEM_LIMIT)


def _token_tile(nt):
    for tm in (512, 384, 256, 128):
        if nt % tm == 0:
            return tm
    raise ValueError(f"token count {nt} is not a multiple of 128")


def _bucket_np(dist):
    n = np.maximum(dist, 0)
    max_exact = NUM_BUCKETS // 2
    nf = np.maximum(n, 1).astype(np.float32)
    large = max_exact + (np.log(nf / np.float32(max_exact)) / np.float32(math.log(MAX_DISTANCE / max_exact))
                         * np.float32(NUM_BUCKETS - max_exact)).astype(np.int32)
    large = np.minimum(large, NUM_BUCKETS - 1)
    return np.where(n < max_exact, n, large).astype(np.int32)


def _toeplitz_bias(rel_bias, n_rows, n_cols, offset, lo, hi, shift_far):
    t = np.arange(n_rows)[:, None]
    l = np.arange(n_cols)[None, :]
    d = t + offset - l
    tab = rel_bias[_bucket_np(d)]
    if shift_far:
        tab = tab - rel_bias[NUM_BUCKETS - 1][None, None, :]
    ok = (d >= lo) & (d < hi)
    tab = jnp.where(jnp.asarray(ok)[:, :, None], tab, NEG)
    return jnp.transpose(tab, (2, 0, 1)).astype(F32)


def _gelu_tanh(x):
    return 0.5 * x * (1.0 + jnp.tanh(math.sqrt(2.0 / math.pi) * (x + 0.044715 * (x * x * x))))


def _layer_norm(y, g, b):
    mu = jnp.mean(y, axis=-1, keepdims=True)
    d = y - mu
    var = jnp.mean(d * d, axis=-1, keepdims=True)
    return d * lax.rsqrt(var + LN_EPS) * g + b


def _online_update(carry, s, v):
    m, l, acc = carry
    m_new = jnp.maximum(m, jnp.max(s, axis=-1, keepdims=True))
    alpha = jnp.exp(m - m_new)
    p = jnp.exp(s - m_new)
    l = alpha * l + jnp.sum(p, axis=-1, keepdims=True)
    acc = alpha * acc + jnp.dot(p.astype(BF16), v, preferred_element_type=F32)
    return m_new, l, acc


def _softmax_pv(s, v):
    m = jnp.max(s, axis=-1, keepdims=True)
    e = jnp.exp(s - m)
    l = jnp.sum(e, axis=-1, keepdims=True)
    return jnp.dot(e.astype(BF16), v, preferred_element_type=F32) / l


def _masked_softmax(s, valid):
    sm = jnp.where(valid, s, NEG)
    m = jnp.max(sm, axis=-1, keepdims=True)
    e = jnp.where(valid, jnp.exp(sm - m), 0.0)
    z = jnp.sum(e, axis=-1, keepdims=True)
    return e / jnp.where(z > 0, z, 1.0)


def _topn_mask(score, n_sel):
    nl = score.shape[-1]
    lane = lax.broadcasted_iota(I32, (1, nl), 1).astype(F32)

    def body(_, c):
        sc, sel = c
        m = jnp.max(sc, axis=-1, keepdims=True)
        first = jnp.min(jnp.where(sc == m, lane, float(nl)), axis=-1, keepdims=True)
        one = lane == first
        sel = jnp.where(jnp.logical_and(one, m > -BIG), 1.0, sel)
        sc = jnp.where(one, -BIG, sc)
        return sc, sel

    _, sel = lax.fori_loop(0, n_sel, body, (score, jnp.zeros_like(score)))
    return sel


def _block_expand(sel_bf16, first_block, n_keys):
    nb = sel_bf16.shape[-1]
    kb = lax.broadcasted_iota(I32, (nb, n_keys), 1) // SEL_BLOCK + first_block
    e = (kb == lax.broadcasted_iota(I32, (nb, n_keys), 0)).astype(BF16)
    return jnp.dot(sel_bf16, e, preferred_element_type=F32)


def _select_blocks(score, qpos, n_sb):
    nl = score.shape[-1]
    blk = lax.broadcasted_iota(I32, (1, nl), 1)
    cur = qpos // SEL_BLOCK
    validb = jnp.logical_and(blk * SEL_BLOCK <= qpos, blk < n_sb)
    forced = jnp.logical_or(blk == 0, jnp.logical_or(blk == cur, blk == cur - 1))
    forced = jnp.logical_and(forced, blk < n_sb)
    sc = jnp.where(forced, BIG, jnp.where(validb, score, -BIG))
    return _topn_mask(sc, min(SEL_TOPN, n_sb))


def _page_parts(pages, perm, bd1_ref):
    permuted = [jnp.dot(perm, p.astype(BF16), preferred_element_type=F32) for p in pages]
    nrow = 8 * len(pages)
    acc = [jnp.zeros((nrow, 2 * NSA_KVW), F32) for _ in range(2)]
    for s in range(CMP_STRIDE):
        a = jnp.concatenate([pp[s * 8:(s + 1) * 8] for pp in permuted], axis=0).astype(BF16)
        for kv in range(2):
            acc[kv] = acc[kv] + jnp.dot(a[:, kv * NSA_KVW:(kv + 1) * NSA_KVW], bd1_ref[kv, s],
                                        preferred_element_type=F32)
    return jnp.concatenate(acc, axis=1)


def _compress_finalize(p_ref, n, b1_ref, bd2_ref, b2_ref):
    outs = []
    for kv in range(2):
        c0 = kv * 2 * NSA_KVW
        p0 = p_ref[pl.ds(0, n), c0:c0 + NSA_KVW]
        p1 = p_ref[pl.ds(1, n), c0 + NSA_KVW:c0 + 2 * NSA_KVW]
        h = _gelu_tanh(p0 + p1 + b1_ref[kv])
        outs.append(jnp.dot(h.astype(BF16), bd2_ref[kv], preferred_element_type=F32) + b2_ref[kv])
    return outs


def _proj_kernel(x_ref, *refs, post):
    n = len(post)
    x = x_ref[...].astype(BF16)
    for w_ref, o_ref, p in zip(refs[:n], refs[n:], post):
        y = jnp.dot(x, w_ref[...], preferred_element_type=F32)
        if p == "qscale":
            y = y * 0.125
        elif p == "sigmoid":
            y = jax.nn.sigmoid(y)
        o_ref[...] = y.astype(o_ref.dtype)


def _project(x, weights, post, dtypes):
    nt, d = x.shape
    tm = _token_tile(nt)
    in_specs = [pl.BlockSpec((tm, d), lambda i: (i, 0))]
    in_specs += [pl.BlockSpec(w.shape, lambda i: (0, 0)) for w in weights]
    out_specs = [pl.BlockSpec((tm, w.shape[1]), lambda i: (i, 0)) for w in weights]
    out_shape = [jax.ShapeDtypeStruct((nt, w.shape[1]), dt) for w, dt in zip(weights, dtypes)]
    return pl.pallas_call(
        functools.partial(_proj_kernel, post=post),
        grid=(nt // tm,), in_specs=in_specs, out_specs=out_specs, out_shape=out_shape,
        compiler_params=_cparams(("parallel",)), name="proj",
    )(x, *weights)


def _cmp_prompt_kernel(rows_ref, perm_ref, bd1_ref, b1_ref, bd2_ref, b2_ref, ck_ref, cv_ref, p_ref, *, rt, n_chunks):
    i = pl.program_id(1)

    @pl.when(i == 0)
    def _():
        p_ref[pl.ds(n_chunks, 8), :] = jnp.zeros((8, p_ref.shape[1]), F32)

    pages = [rows_ref[0, pl.ds(k * PAGE, PAGE), :] for k in range(rt // PAGE)]
    nrow = rt // CMP_STRIDE
    p_ref[pl.ds(pl.multiple_of(i * nrow, nrow), nrow), :] = _page_parts(pages, perm_ref[...], bd1_ref)

    @pl.when(i == pl.num_programs(1) - 1)
    def _():
        ck, cv = _compress_finalize(p_ref, n_chunks, b1_ref, bd2_ref, b2_ref)
        ck_ref[0] = ck.astype(ck_ref.dtype)
        cv_ref[0] = cv.astype(cv_ref.dtype)


def _compress_prompt(rows, cw):
    b, s, w = rows.shape
    rt = min(2048, s)
    n_chunks = s // CMP_STRIDE
    const = lambda a: pl.BlockSpec(a.shape, lambda bi, i, _n=a.ndim: (0,) * _n)
    return pl.pallas_call(
        functools.partial(_cmp_prompt_kernel, rt=rt, n_chunks=n_chunks),
        grid=(b, s // rt),
        in_specs=[pl.BlockSpec((1, rt, w), lambda bi, i: (bi, i, 0)),
                  const(cw["perm"]), const(cw["bd1"]), const(cw["b1"]), const(cw["bd2"]), const(cw["b2"])],
        out_specs=[pl.BlockSpec((1, n_chunks, NSA_KVW), lambda bi, i: (bi, 0, 0))] * 2,
        out_shape=[jax.ShapeDtypeStruct((b, n_chunks, NSA_KVW), BF16)] * 2,
        scratch_shapes=[pltpu.VMEM((n_chunks + 8, 4 * NSA_KVW), F32)],
        compiler_params=_cparams(("parallel", "arbitrary")), name="cmp_prompt",
    )(rows, cw["perm"], cw["bd1"], cw["b1"], cw["bd2"], cw["b2"])


def _nsa_prompt_kernel(q_ref, gates_ref, ck_ref, cv_ref, sk_ref, sv_ref, wk_ref, wv_ref, tw_ref, tn_ref, cm_ref,
                       o_ref, *, n_sb):
    g = pl.program_id(1)
    i = pl.program_id(2)
    start = i * QB
    rows = NSA_J * QB
    q2 = q_ref[0].reshape(rows, NSA_DH)
    qpos = start + lax.broadcasted_iota(I32, (QB, 1), 0)

    ck = ck_ref[0, 0]
    nbp = ck.shape[0]
    lc = lax.dot_general(q2, ck, NT_DIMS, preferred_element_type=F32).reshape(NSA_J, QB, nbp)
    c_end = lax.broadcasted_iota(I32, (1, nbp), 1) * CMP_STRIDE + (CMP_BLOCK - 1)
    pc = _masked_softmax(lc, (c_end <= qpos)[None])
    o_cmp = jnp.dot(pc.reshape(rows, nbp).astype(BF16), cv_ref[0, 0], preferred_element_type=F32)
    ps = pc[0] + pc[1] + pc[2] + pc[3]
    score = jnp.dot(ps, cm_ref[...], precision=HI, preferred_element_type=F32)
    sel = _select_blocks(score, qpos, n_sb)

    blk = lax.broadcasted_iota(I32, (1, sel.shape[1]), 1)
    near0 = 2 * i - 2
    sel_far = jnp.where(blk < near0, sel, 0.0).astype(BF16)
    n_far = (jnp.maximum(i - 1, 0) * QB + KT - 1) // KT

    def far_body(kt, carry):
        off = pl.multiple_of(PADF + kt * KT, KT)
        k = sk_ref[0, 0, pl.ds(off, KT), :]
        v = sv_ref[0, 0, pl.ds(off, KT), :]
        s = lax.dot_general(q2, k, NT_DIMS, preferred_element_type=F32).reshape(NSA_J, QB, KT)
        keep = _block_expand(sel_far, kt * (KT // SEL_BLOCK), KT) > 0.5
        s = jnp.where(keep[None], s, NEG).reshape(rows, KT)
        return _online_update(carry, s, v)

    init = (jnp.full((rows, 1), NEG, F32), jnp.zeros((rows, 1), F32), jnp.zeros((rows, NSA_DH), F32))
    carry = lax.fori_loop(0, n_far, far_body, init)

    offn = pl.multiple_of(PADF + start - QB, QB)
    k = sk_ref[0, 0, pl.ds(offn, 2 * QB), :]
    v = sv_ref[0, 0, pl.ds(offn, 2 * QB), :]
    s = lax.dot_general(q2, k, NT_DIMS, preferred_element_type=F32).reshape(NSA_J, QB, 2 * QB) + tn_ref[...]
    keep = _block_expand(sel.astype(BF16), near0, 2 * QB) > 0.5
    s = jnp.where(keep[None], s, NEG).reshape(rows, 2 * QB)
    _, l, acc = _online_update(carry, s, v)
    o_slc = acc / l

    lw = WINDOW + QB
    offw = pl.multiple_of(PADF + start - WINDOW, QB)
    k = wk_ref[0, 0, pl.ds(offw, lw), :]
    v = wv_ref[0, 0, pl.ds(offw, lw), :]
    s = lax.dot_general(q2, k, NT_DIMS, preferred_element_type=F32).reshape(NSA_J, QB, lw) + tw_ref[...]
    in_seq = lax.broadcasted_iota(I32, (1, 1, lw), 2) >= WINDOW - start
    s = jnp.where(in_seq, s, NEG).reshape(rows, lw)
    o_win = _softmax_pv(s, v)

    gt = gates_ref[0]
    colid = lax.broadcasted_iota(I32, (gt.shape[1], NSA_DH), 0)
    for j in range(NSA_J):
        out = jnp.zeros((QB, NSA_DH), F32)
        for c, ob in enumerate((o_cmp, o_slc, o_win)):
            pick = (colid == 3 * (NSA_J * g + j) + c).astype(F32)
            gb = jnp.dot(gt, pick, precision=HI, preferred_element_type=F32)
            out = out + gb * ob[j * QB:(j + 1) * QB]
        o_ref[0, j] = out.astype(o_ref.dtype)


def _nsa_prompt_attention(q4, gates, ck, cv, sk, sv, wk, wv, t_win, t_near, cmat, n_sb):
    b, h, s, dh = q4.shape
    sp = sk.shape[2]
    nq = s // QB
    nbp = ck.shape[2]
    kv_spec = pl.BlockSpec((1, 1, sp, dh), lambda bi, g, i: (bi, g, 0, 0))
    c_spec = pl.BlockSpec((1, 1, nbp, dh), lambda bi, g, i: (bi, g, 0, 0))
    return pl.pallas_call(
        functools.partial(_nsa_prompt_kernel, n_sb=n_sb),
        grid=(b, NSA_KV, nq),
        in_specs=[pl.BlockSpec((1, NSA_J, QB, dh), lambda bi, g, i: (bi, g, i, 0)),
                  pl.BlockSpec((1, QB, gates.shape[2]), lambda bi, g, i: (bi, i, 0)),
                  c_spec, c_spec, kv_spec, kv_spec, kv_spec, kv_spec,
                  pl.BlockSpec((NSA_J, QB, t_win.shape[2]), lambda bi, g, i: (g, 0, 0)),
                  pl.BlockSpec((NSA_J, QB, t_near.shape[2]), lambda bi, g, i: (g, 0, 0)),
                  pl.BlockSpec(cmat.shape, lambda bi, g, i: (0, 0))],
        out_specs=pl.BlockSpec((1, NSA_J, QB, dh), lambda bi, g, i: (bi, g, i, 0)),
        out_shape=jax.ShapeDtypeStruct((b, h, s, dh), BF16),
        compiler_params=_cparams(("parallel", "parallel", "arbitrary")), name="nsa_prompt",
    )(q4, gates, ck, cv, sk, sv, wk, wv, t_win, t_near, cmat)


def _diff_lambda(lq1_ref, lk1_ref, lq2_ref, lk2_ref, lam_init):
    a = jnp.sum(lq1_ref[...] * lk1_ref[...], axis=-1, keepdims=True)
    b = jnp.sum(lq2_ref[...] * lk2_ref[...], axis=-1, keepdims=True)
    return jnp.exp(a) - jnp.exp(b) + lam_init


def _diff_prompt_kernel(q_ref, k_ref, v_ref, tn_ref, lq1_ref, lk1_ref, lq2_ref, lk2_ref, o_ref, *, lam_init):
    i = pl.program_id(2)
    start = i * QB
    rows = DIFF_J * QB
    n_far = (jnp.maximum(i - 1, 0) * QB + KT - 1) // KT
    far_end = start - QB
    outs = []
    for m in range(2):
        q2 = q_ref[0, 0, m].reshape(rows, DIFF_DH)

        def far_body(kt, carry, m=m, q2=q2):
            off = pl.multiple_of(PADF + kt * KT, KT)
            k = k_ref[0, 0, m, pl.ds(off, KT), :]
            v = v_ref[0, 0, pl.ds(off, KT), :]
            s = lax.dot_general(q2, k, NT_DIMS, preferred_element_type=F32)
            pos = kt * KT + lax.broadcasted_iota(I32, (1, KT), 1)
            return _online_update(carry, jnp.where(pos < far_end, s, NEG), v)

        init = (jnp.full((rows, 1), NEG, F32), jnp.zeros((rows, 1), F32), jnp.zeros((rows, DIFF_DV), F32))
        carry = lax.fori_loop(0, n_far, far_body, init)
        offn = pl.multiple_of(PADF + start - QB, QB)
        k = k_ref[0, 0, m, pl.ds(offn, 2 * QB), :]
        v = v_ref[0, 0, pl.ds(offn, 2 * QB), :]
        s = lax.dot_general(q2, k, NT_DIMS, preferred_element_type=F32).reshape(DIFF_J, QB, 2 * QB)
        s = (s + tn_ref[m]).reshape(rows, 2 * QB)
        pos = start - QB + lax.broadcasted_iota(I32, (1, 2 * QB), 1)
        s = jnp.where(pos >= 0, s, NEG)
        _, l, acc = _online_update(carry, s, v)
        outs.append(acc / l)
    lam = _diff_lambda(lq1_ref, lk1_ref, lq2_ref, lk2_ref, lam_init)
    o = outs[0] - lam * outs[1]
    o_ref[0] = jnp.concatenate([o[j * QB:(j + 1) * QB] for j in range(DIFF_J)], axis=1)


def _diff_prompt_attention(q6, k5, v4, t_near, lams, lam_init):
    b, g, _, _, s, dh = q6.shape
    sp = k5.shape[3]
    nq = s // QB
    lam_spec = pl.BlockSpec((1, DIFF_DH), lambda bi, gi, i: (0, 0))
    return pl.pallas_call(
        functools.partial(_diff_prompt_kernel, lam_init=lam_init),
        grid=(b, g, nq),
        in_specs=[pl.BlockSpec((1, 1, 2, DIFF_J, QB, dh), lambda bi, gi, i: (bi, gi, 0, 0, i, 0)),
                  pl.BlockSpec((1, 1, 2, sp, dh), lambda bi, gi, i: (bi, gi, 0, 0, 0)),
                  pl.BlockSpec((1, 1, sp, DIFF_DV), lambda bi, gi, i: (bi, gi, 0, 0)),
                  pl.BlockSpec((None, 2, DIFF_J, QB, 2 * QB), lambda bi, gi, i: (gi, 0, 0, 0, 0)),
                  lam_spec, lam_spec, lam_spec, lam_spec],
        out_specs=pl.BlockSpec((1, QB, DIFF_J * DIFF_DV), lambda bi, gi, i: (bi, i, gi)),
        out_shape=jax.ShapeDtypeStruct((b, s, DIFF_HEADS * DIFF_DV), F32),
        compiler_params=_cparams(("parallel", "parallel", "arbitrary")), name="diff_prompt",
    )(q6, k5, v4, t_near, *lams)


def _nsa_samp_a_kernel(pt_ref, *refs, n_pp, n_pages, past, t_new, n_sb):
    page_refs = refs[:n_pp]
    (xc_ref, qbd_ref, perm_ref, bd1_ref, b1_ref, bd2_ref, b2_ref, cm_ref, ocmp_ref, sel_ref, p_ref) = refs[n_pp:]
    i = pl.program_id(1)
    nrow = n_pp * 8
    parts = _page_parts([r[...] for r in page_refs], perm_ref[...], bd1_ref)
    p_ref[pl.ds(pl.multiple_of(i * nrow, nrow), nrow), :] = parts

    @pl.when(i == pl.num_programs(1) - 1)
    def _():
        xc = xc_ref[0]
        accs = []
        for kv in range(2):
            acc = jnp.zeros((1, 2 * NSA_KVW), F32)
            for s in range(CMP_STRIDE):
                acc = acc + jnp.dot(xc[s:s + 1, kv * NSA_KVW:(kv + 1) * NSA_KVW].astype(BF16), bd1_ref[kv, s],
                                    preferred_element_type=F32)
            accs.append(acc)
        n_cached = n_pages * 8
        p_ref[pl.ds(n_cached, 1), :] = jnp.concatenate(accs, axis=1)
        ck, cv = _compress_finalize(p_ref, n_cached, b1_ref, bd2_ref, b2_ref)
        qbd = qbd_ref[0]
        nrows = qbd.shape[0]
        lc = lax.dot_general(qbd, ck.astype(BF16), NT_DIMS, preferred_element_type=F32)
        qpos = past + lax.broadcasted_iota(I32, (nrows, 1), 0) % t_new
        c_end = lax.broadcasted_iota(I32, (1, n_cached), 1) * CMP_STRIDE + (CMP_BLOCK - 1)
        pc = _masked_softmax(lc, c_end <= qpos)
        ocmp_ref[0] = jnp.dot(pc.astype(BF16), cv.astype(BF16), preferred_element_type=F32)
        rg = nrows // NSA_J
        ps = pc[0:rg] + pc[rg:2 * rg] + pc[2 * rg:3 * rg] + pc[3 * rg:4 * rg]
        score = jnp.dot(ps, cm_ref[...], precision=HI, preferred_element_type=F32)
        sel_ref[0] = _select_blocks(score, qpos[0:rg], n_sb)


def _page_specs(n_pp, width):
    def spec(k):
        return pl.BlockSpec((None, PAGE, width), lambda b, i, pt, k=k: (pt[b, i * n_pp + k], 0, 0))
    return [spec(k) for k in range(n_pp)]


def _nsa_sample_a(page_table, cache, xc, qbd, cw, cmat, past, t_new, n_sb, n_pp):
    bd, n_pages = page_table.shape
    nrows = qbd.shape[1]
    rg = nrows // NSA_J
    nsbp = cmat.shape[1]
    const = lambda a: pl.BlockSpec(a.shape, lambda b, i, pt, _n=a.ndim: (0,) * _n)
    grid_spec = pltpu.PrefetchScalarGridSpec(
        num_scalar_prefetch=1, grid=(bd, n_pages // n_pp),
        in_specs=_page_specs(n_pp, cache.shape[2]) + [
            pl.BlockSpec((1, CMP_STRIDE, cache.shape[2]), lambda b, i, pt: (b, 0, 0)),
            pl.BlockSpec((1, nrows, NSA_KVW), lambda b, i, pt: (b, 0, 0)),
            const(cw["perm"]), const(cw["bd1"]), const(cw["b1"]), const(cw["bd2"]), const(cw["b2"]), const(cmat)],
        out_specs=[pl.BlockSpec((1, nrows, NSA_KVW), lambda b, i, pt: (b, 0, 0)),
                   pl.BlockSpec((1, rg, nsbp), lambda b, i, pt: (b, 0, 0))],
        scratch_shapes=[pltpu.VMEM((n_pages * 8 + 8, 4 * NSA_KVW), F32)])
    return pl.pallas_call(
        functools.partial(_nsa_samp_a_kernel, n_pp=n_pp, n_pages=n_pages, past=past, t_new=t_new, n_sb=n_sb),
        grid_spec=grid_spec,
        out_shape=[jax.ShapeDtypeStruct((bd, nrows, NSA_KVW), F32), jax.ShapeDtypeStruct((bd, rg, nsbp), F32)],
        compiler_params=_cparams(("parallel", "arbitrary")), name="nsa_sample_a",
    )(page_table, *([cache] * n_pp), xc, qbd, cw["perm"], cw["bd1"], cw["b1"], cw["bd2"], cw["b2"], cmat)


def _nsa_samp_b_kernel(pt_ref, *refs, n_pp, n_pages):
    page_refs = refs[:n_pp]
    (qbd_ref, sel_ref, snew_ref, wst_ref, wnew_ref, ocmp_ref, gate_ref, tsn_ref, tsw_ref, fold_ref,
     o_ref, m_ref, l_ref, acc_ref) = refs[n_pp:]
    i = pl.program_id(1)

    @pl.when(i == 0)
    def _():
        m_ref[...] = jnp.full(m_ref.shape, NEG, F32)
        l_ref[...] = jnp.zeros(l_ref.shape, F32)
        acc_ref[...] = jnp.zeros(acc_ref.shape, F32)

    qbd = qbd_ref[0]
    sel = jnp.concatenate([sel_ref[0]] * NSA_J, axis=0)
    near0 = 2 * (n_pages - 1)
    blk = lax.broadcasted_iota(I32, (1, sel.shape[1]), 1)
    sel_far = jnp.where(blk < near0, sel, 0.0).astype(BF16)
    nk = n_pp * PAGE
    kcat = jnp.concatenate([r[:, 0:NSA_KVW] for r in page_refs], axis=0).astype(BF16)
    vcat = jnp.concatenate([r[:, NSA_KVW:2 * NSA_KVW] for r in page_refs], axis=0).astype(BF16)
    s = lax.dot_general(qbd, kcat, NT_DIMS, preferred_element_type=F32)
    keep = _block_expand(sel_far, i * (nk // SEL_BLOCK), nk) > 0.5
    carry = _online_update((m_ref[...], l_ref[...], acc_ref[...]), jnp.where(keep, s, NEG), vcat)
    m_ref[...], l_ref[...], acc_ref[...] = carry

    @pl.when(i == pl.num_programs(1) - 1)
    def _():
        last = page_refs[n_pp - 1]
        snew = snew_ref[0]
        kn = jnp.concatenate([last[:, 0:NSA_KVW], snew[:, 0:NSA_KVW]], axis=0).astype(BF16)
        vn = jnp.concatenate([last[:, NSA_KVW:2 * NSA_KVW], snew[:, NSA_KVW:2 * NSA_KVW]], axis=0).astype(BF16)
        s = lax.dot_general(qbd, kn, NT_DIMS, preferred_element_type=F32) + tsn_ref[...]
        keep = _block_expand(sel.astype(BF16), near0, 2 * PAGE) > 0.5
        _, l, acc = _online_update((m_ref[...], l_ref[...], acc_ref[...]), jnp.where(keep, s, NEG), vn)
        o_slc = acc / l
        wst = wst_ref[0]
        wnew = wnew_ref[0]
        wk = jnp.concatenate([wst[:, 0:NSA_KVW], wnew[:, 0:NSA_KVW]], axis=0).astype(BF16)
        wv = jnp.concatenate([wst[:, NSA_KVW:2 * NSA_KVW], wnew[:, NSA_KVW:2 * NSA_KVW]], axis=0).astype(BF16)
        s = lax.dot_general(qbd, wk, NT_DIMS, preferred_element_type=F32) + tsw_ref[...]
        o_win = _softmax_pv(s, wv)
        gt = gate_ref[0]
        o_full = gt[:, 0:1] * ocmp_ref[0] + gt[:, 1:2] * o_slc + gt[:, 2:3] * o_win
        nrows = o_full.shape[0]
        t_new = nrows // NSA_HEADS
        row_g = (lax.broadcasted_iota(I32, o_full.shape, 0) // t_new) % NSA_KV
        col_g = lax.broadcasted_iota(I32, o_full.shape, 1) // NSA_DH
        od = jnp.where(row_g == col_g, o_full, 0.0).astype(BF16)
        o_ref[0] = jnp.dot(od, fold_ref[...], preferred_element_type=F32).astype(o_ref.dtype)


def _nsa_sample_b(page_table, cache, qbd, sel, snew, wst, wnew, ocmp, gates, tsn, tsw, fold, n_pp):
    bd, n_pages = page_table.shape
    nrows = qbd.shape[1]
    per_b = lambda a: pl.BlockSpec((1,) + a.shape[1:], lambda b, i, pt, _n=a.ndim: (b,) + (0,) * (_n - 1))
    const = lambda a: pl.BlockSpec(a.shape, lambda b, i, pt, _n=a.ndim: (0,) * _n)
    grid_spec = pltpu.PrefetchScalarGridSpec(
        num_scalar_prefetch=1, grid=(bd, n_pages // n_pp),
        in_specs=_page_specs(n_pp, cache.shape[2]) + [
            per_b(qbd), per_b(sel), per_b(snew), per_b(wst), per_b(wnew), per_b(ocmp), per_b(gates),
            const(tsn), const(tsw), const(fold)],
        out_specs=pl.BlockSpec((1, nrows, NSA_DH), lambda b, i, pt: (b, 0, 0)),
        scratch_shapes=[pltpu.VMEM((nrows, 1), F32), pltpu.VMEM((nrows, 1), F32), pltpu.VMEM((nrows, NSA_KVW), F32)])
    return pl.pallas_call(
        functools.partial(_nsa_samp_b_kernel, n_pp=n_pp, n_pages=n_pages),
        grid_spec=grid_spec,
        out_shape=jax.ShapeDtypeStruct((bd, nrows, NSA_DH), BF16),
        compiler_params=_cparams(("parallel", "arbitrary")), name="nsa_sample_b",
    )(page_table, *([cache] * n_pp), qbd, sel, snew, wst, wnew, ocmp, gates, tsn, tsw, fold)


def _diff_samp_kernel(pt_ref, *refs, n_pp, lam_init):
    page_refs = refs[:n_pp]
    (qbd_ref, dnew_ref, tdn_ref, lq1_ref, lk1_ref, lq2_ref, lk2_ref, o_ref, m_ref, l_ref, acc_ref) = refs[n_pp:]
    i = pl.program_id(1)
    last_step = i == pl.num_programs(1) - 1
    kw = DIFF_KV * 2 * DIFF_DH

    @pl.when(i == 0)
    def _():
        m_ref[...] = jnp.full(m_ref.shape, NEG, F32)
        l_ref[...] = jnp.zeros(l_ref.shape, F32)
        acc_ref[...] = jnp.zeros(acc_ref.shape, F32)

    qbd = qbd_ref[0]
    nk = n_pp * PAGE
    kcat = jnp.concatenate([r[:, 0:kw] for r in page_refs], axis=0).astype(BF16)
    vcat = jnp.concatenate([r[:, kw:2 * kw] for r in page_refs], axis=0).astype(BF16)
    s = lax.dot_general(qbd, kcat, NT_DIMS, preferred_element_type=F32)
    limit = jnp.where(last_step, nk - PAGE, nk)
    s = jnp.where(lax.broadcasted_iota(I32, (1, nk), 1) < limit, s, NEG)
    carry = _online_update((m_ref[...], l_ref[...], acc_ref[...]), s, vcat)
    m_ref[...], l_ref[...], acc_ref[...] = carry

    @pl.when(last_step)
    def _():
        last = page_refs[n_pp - 1]
        dnew = dnew_ref[0]
        kn = jnp.concatenate([last[:, 0:kw], dnew[:, 0:kw]], axis=0).astype(BF16)
        vn = jnp.concatenate([last[:, kw:2 * kw], dnew[:, kw:2 * kw]], axis=0).astype(BF16)
        s = lax.dot_general(qbd, kn, NT_DIMS, preferred_element_type=F32) + tdn_ref[...]
        _, l, acc = _online_update((m_ref[...], l_ref[...], acc_ref[...]), s, vn)
        o_full = acc / l
        half = o_full.shape[0] // 2
        lam = _diff_lambda(lq1_ref, lk1_ref, lq2_ref, lk2_ref, lam_init)
        o = o_full[0:half] - lam * o_full[half:2 * half]
        per_g = half // DIFF_KV
        row_g = lax.broadcasted_iota(I32, (half, DIFF_DV), 0) // per_g
        out = jnp.zeros((half, DIFF_DV), F32)
        for g in range(DIFF_KV):
            out = out + jnp.where(row_g == g, o[:, g * DIFF_DV:(g + 1) * DIFF_DV], 0.0)
        o_ref[0] = out


def _diff_sample(page_table, cache, qbd, dnew, tdn, lams, lam_init, n_pp):
    bd, n_pages = page_table.shape
    nrows = qbd.shape[1]
    kw = DIFF_KV * 2 * DIFF_DH
    per_b = lambda a: pl.BlockSpec((1,) + a.shape[1:], lambda b, i, pt, _n=a.ndim: (b,) + (0,) * (_n - 1))
    const = lambda a: pl.BlockSpec(a.shape, lambda b, i, pt, _n=a.ndim: (0,) * _n)
    grid_spec = pltpu.PrefetchScalarGridSpec(
        num_scalar_prefetch=1, grid=(bd, n_pages // n_pp),
        in_specs=_page_specs(n_pp, cache.shape[2]) + [per_b(qbd), per_b(dnew), const(tdn)] + [const(a) for a in lams],
        out_specs=pl.BlockSpec((1, nrows // 2, DIFF_DV), lambda b, i, pt: (b, 0, 0)),
        scratch_shapes=[pltpu.VMEM((nrows, 1), F32), pltpu.VMEM((nrows, 1), F32), pltpu.VMEM((nrows, kw), F32)])
    return pl.pallas_call(
        functools.partial(_diff_samp_kernel, n_pp=n_pp, lam_init=lam_init),
        grid_spec=grid_spec,
        out_shape=jax.ShapeDtypeStruct((bd, nrows // 2, DIFF_DV), F32),
        compiler_params=_cparams(("parallel", "arbitrary")), name="diff_sample",
    )(page_table, *([cache] * n_pp), qbd, dnew, tdn, *lams)


def _route(lt):
    mx = jnp.max(lt, axis=0, keepdims=True)
    e = jnp.exp(lt - mx)
    p = e / jnp.sum(e, axis=0, keepdims=True)
    best = None
    grp = None
    for g in range(N_GROUPS):
        r = [p[EPG * g + k:EPG * g + k + 1] for k in range(EPG)]
        a, b = jnp.maximum(r[0], r[1]), jnp.minimum(r[0], r[1])
        c, d = jnp.maximum(r[2], r[3]), jnp.minimum(r[2], r[3])
        sc = jnp.maximum(a, c) + jnp.maximum(jnp.minimum(a, c), jnp.maximum(b, d))
        if g == 0:
            best, grp = sc, jnp.zeros(sc.shape, I32)
        else:
            better = sc > best
            grp = jnp.where(better, g, grp)
            best = jnp.where(better, sc, best)
    lg = []
    for k in range(EPG):
        v = lt[k:k + 1]
        for g in range(1, N_GROUPS):
            v = jnp.where(grp == g, lt[EPG * g + k:EPG * g + k + 1], v)
        lg.append(v)

    def first_max(vals):
        vmax = jnp.maximum(jnp.maximum(vals[0], vals[1]), jnp.maximum(vals[2], vals[3]))
        idx = jnp.where(vals[0] == vmax, 0, jnp.where(vals[1] == vmax, 1, jnp.where(vals[2] == vmax, 2, 3)))
        return vmax, idx

    v1, i1 = first_max(lg)
    rest = [jnp.where(i1 == k, -jnp.inf, lg[k]) for k in range(EPG)]
    v2, i2 = first_max(rest)
    e2 = jnp.exp(v2 - v1)
    den = 1.0 + e2
    return grp * EPG + i1, grp * EPG + i2, 1.0 / den, e2 / den


def _post_attn_kernel(o_ref, x_ref, w_ref, g_ref, b_ref, wr_ref, rb_ref, gain_ref, x1_ref, eidx_ref, gate_ref, *,
                      merge_scale):
    o = o_ref[...]
    if merge_scale is not None:
        segs = []
        for h in range(DIFF_HEADS):
            seg = o[:, h * DIFF_DV:(h + 1) * DIFF_DV]
            ms = jnp.mean(seg * seg, axis=-1, keepdims=True)
            segs.append(seg * lax.rsqrt(ms + LN_EPS) * gain_ref[...] * merge_scale)
        o = jnp.concatenate(segs, axis=1)
    mix = jnp.dot(o.astype(BF16), w_ref[...], preferred_element_type=F32)
    x1 = _layer_norm(DN_ALPHA * x_ref[...] + mix, g_ref[...], b_ref[...])
    x1_ref[...] = x1
    lt = lax.dot_general(wr_ref[...], x1, NT_DIMS, precision=HI, preferred_element_type=F32) + rb_ref[...]
    e1, e2, g1, g2 = _route(lt)
    eidx_ref[0:1, :] = e1
    eidx_ref[1:2, :] = e2
    gate_ref[0:1, :] = g1
    gate_ref[1:2, :] = g2


def _post_attention(o, x, w_out, ln_g, ln_b, wr_t, rb, gain, merge_scale):
    nt, d = x.shape
    tm = _token_tile(nt)
    row = lambda i: (i, 0)
    const = lambda a: pl.BlockSpec(a.shape, lambda i: (0, 0))
    return pl.pallas_call(
        functools.partial(_post_attn_kernel, merge_scale=merge_scale),
        grid=(nt // tm,),
        in_specs=[pl.BlockSpec((tm, d), row), pl.BlockSpec((tm, d), row), const(w_out), const(ln_g), const(ln_b),
                  const(wr_t), const(rb), const(gain)],
        out_specs=[pl.BlockSpec((tm, d), row), pl.BlockSpec((TOP_K, tm), lambda i: (0, i)),
                   pl.BlockSpec((TOP_K, tm), lambda i: (0, i))],
        out_shape=[jax.ShapeDtypeStruct((nt, d), F32), jax.ShapeDtypeStruct((TOP_K, nt), I32),
                   jax.ShapeDtypeStruct((TOP_K, nt), F32)],
        compiler_params=_cparams(("parallel",)), name="post_attn",
    )(o, x, w_out, ln_g, ln_b, wr_t, rb, gain)


def _row_copy(src_hbm, src_row, buf, slot, r, sem):
    return pltpu.make_async_copy(src_hbm.at[pl.ds(src_row, 1), :], buf.at[slot, pl.ds(r, 1), :], sem.at[slot])


def _start_rows(idx_ref, src_hbm, buf, slot, sem, n):
    def body(r, c):
        _row_copy(src_hbm, idx_ref[0, r], buf, slot, r, sem).start()
        return c
    lax.fori_loop(0, n, body, 0)


def _wait_rows(src_hbm, buf, slot, sem, n):
    def body(r, c):
        _row_copy(src_hbm, 0, buf, slot, r, sem).wait()
        return c
    lax.fori_loop(0, n, body, 0)


def _moe_kernel(blk_e_ref, nv_ref, tok0_ref, tokn_ref, x_hbm, wgu_ref, wdn_ref, o_ref, xbuf, sem):
    i = pl.program_id(0)
    nv = nv_ref[0]
    tm = xbuf.shape[1]
    de = wdn_ref.shape[1]

    @pl.when(i == 0)
    def _():
        _start_rows(tok0_ref, x_hbm, xbuf, 0, sem, tm)

    @pl.when(i + 1 < nv)
    def _():
        _start_rows(tokn_ref, x_hbm, xbuf, (i + 1) % 2, sem, tm)

    @pl.when(i < nv)
    def _():
        slot = i % 2
        _wait_rows(x_hbm, xbuf, slot, sem, tm)
        xb = xbuf[slot].astype(BF16)
        acc = jnp.zeros(o_ref.shape, F32)
        half = de // 2
        for c in range(2):
            gate = jnp.dot(xb, wgu_ref[0, :, c * half:(c + 1) * half], preferred_element_type=F32)
            up = jnp.dot(xb, wgu_ref[0, :, de + c * half:de + (c + 1) * half], preferred_element_type=F32)
            act = (gate * jax.nn.sigmoid(gate) * up).astype(BF16)
            acc = acc + jnp.dot(act, wdn_ref[0, c * half:(c + 1) * half, :], preferred_element_type=F32)
        o_ref[...] = acc

    @pl.when(i >= nv)
    def _():
        o_ref[...] = jnp.zeros(o_ref.shape, F32)


def _moe_experts(x1, tok_blocks, blk_e, n_valid, w_gu, w_dn):
    nt, d = x1.shape
    n_blocks, _, tm = tok_blocks.shape
    grid_spec = pltpu.PrefetchScalarGridSpec(
        num_scalar_prefetch=2, grid=(n_blocks,),
        in_specs=[pl.BlockSpec((None, 1, tm), lambda i, be, nv: (i, 0, 0), memory_space=pltpu.SMEM),
                  pl.BlockSpec((None, 1, tm), lambda i, be, nv: (jnp.minimum(i + 1, n_blocks - 1), 0, 0),
                               memory_space=pltpu.SMEM),
                  pl.BlockSpec(memory_space=pl.ANY),
                  pl.BlockSpec((1,) + w_gu.shape[1:], lambda i, be, nv: (be[i], 0, 0)),
                  pl.BlockSpec((1,) + w_dn.shape[1:], lambda i, be, nv: (be[i], 0, 0))],
        out_specs=pl.BlockSpec((tm, d), lambda i, be, nv: (i, 0)),
        scratch_shapes=[pltpu.VMEM((2, tm, d), F32), pltpu.SemaphoreType.DMA((2,))])
    return pl.pallas_call(
        _moe_kernel, grid_spec=grid_spec,
        out_shape=jax.ShapeDtypeStruct((n_blocks * tm, d), F32),
        compiler_params=_cparams(("arbitrary",)), name="moe_experts",
    )(blk_e, n_valid, tok_blocks, tok_blocks, x1, w_gu, w_dn)


def _moe_combine_kernel(pos0_ref, posn_ref, y_hbm, x1_ref, gate_ref, g_ref, b_ref, o_ref, ybuf, sem):
    i = pl.program_id(0)
    n = pl.num_programs(0)
    tm = x1_ref.shape[0]

    @pl.when(i == 0)
    def _():
        _start_rows(pos0_ref, y_hbm, ybuf, 0, sem, TOP_K * tm)

    @pl.when(i + 1 < n)
    def _():
        _start_rows(posn_ref, y_hbm, ybuf, (i + 1) % 2, sem, TOP_K * tm)

    slot = i % 2
    _wait_rows(y_hbm, ybuf, slot, sem, TOP_K * tm)
    gt = gate_ref[...]
    f = gt[:, 0:1] * ybuf[slot, pl.ds(0, tm), :] + gt[:, 1:2] * ybuf[slot, pl.ds(tm, tm), :]
    o_ref[...] = _layer_norm(DN_ALPHA * x1_ref[...] + f, g_ref[...], b_ref[...])


def _moe_combine(y_sorted, pos_blocks, x1, gates_t, ln_g, ln_b):
    nt, d = x1.shape
    n_tiles, _, tm2 = pos_blocks.shape
    tm = tm2 // TOP_K
    const = lambda a: pl.BlockSpec(a.shape, lambda i: (0, 0))
    return pl.pallas_call(
        _moe_combine_kernel, grid=(n_tiles,),
        in_specs=[pl.BlockSpec((None, 1, tm2), lambda i: (i, 0, 0), memory_space=pltpu.SMEM),
                  pl.BlockSpec((None, 1, tm2), lambda i: (jnp.minimum(i + 1, n_tiles - 1), 0, 0),
                               memory_space=pltpu.SMEM),
                  pl.BlockSpec(memory_space=pl.ANY),
                  pl.BlockSpec((tm, d), lambda i: (i, 0)), pl.BlockSpec((tm, TOP_K), lambda i: (i, 0)),
                  const(ln_g), const(ln_b)],
        out_specs=pl.BlockSpec((tm, d), lambda i: (i, 0)),
        out_shape=jax.ShapeDtypeStruct((nt, d), F32),
        scratch_shapes=[pltpu.VMEM((2, tm2, d), F32), pltpu.SemaphoreType.DMA((2,))],
        compiler_params=_cparams(("arbitrary",)), name="moe_combine",
    )(pos_blocks, pos_blocks, y_sorted, x1, gates_t, ln_g, ln_b)


def _moe_layer(x1, eidx, gate, w_gu, w_dn, ln_g, ln_b):
    nt, _ = x1.shape
    a = nt * TOP_K
    flat_e = eidx.T.reshape(a)
    order = jnp.argsort(flat_e)
    e_sorted = flat_e[order]
    tok = (order // TOP_K).astype(I32)
    sizes = jnp.bincount(flat_e, length=N_EXPERTS)
    seg_start = jnp.cumsum(sizes) - sizes
    padded = (sizes + MOE_TM - 1) // MOE_TM * MOE_TM
    pad_end = jnp.cumsum(padded)
    pad_start = pad_end - padded
    dest = (pad_start[e_sorted] + jnp.arange(a) - seg_start[e_sorted]).astype(I32)
    n_blocks = -(-a // MOE_TM) + N_EXPERTS
    tok_pad = jnp.zeros((n_blocks * MOE_TM,), I32).at[dest].set(tok)
    blk_e = jnp.minimum(jnp.searchsorted(pad_end, jnp.arange(n_blocks) * MOE_TM, side="right"),
                        N_EXPERTS - 1).astype(I32)
    n_valid = (pad_end[-1:] // MOE_TM).astype(I32)
    y_sorted = _moe_experts(x1, tok_pad.reshape(n_blocks, 1, MOE_TM), blk_e, n_valid, w_gu, w_dn)
    pos = jnp.zeros((a,), I32).at[order].set(dest)
    tm = _token_tile(nt)
    pos_blocks = pos.reshape(nt // tm, tm, TOP_K).transpose(0, 2, 1).reshape(nt // tm, 1, TOP_K * tm)
    return _moe_combine(y_sorted, pos_blocks, x1, gate.T, ln_g, ln_b)


def _compress_weights(cmp_k, cmp_v):
    eye = jnp.eye(NSA_KV, dtype=F32)
    bd1, b1, bd2, b2 = [], [], [], []
    for (w1, bias1, w2, bias2) in (cmp_k, cmp_v):
        w1r = w1.reshape(CMP_SPAN, CMP_STRIDE, NSA_DH, NSA_DH)
        bd1.append(jnp.einsum("rsdh,ab->sadrbh", w1r, eye).reshape(CMP_STRIDE, NSA_KVW, CMP_SPAN * NSA_KVW))
        bd2.append(jnp.einsum("hd,ab->ahbd", w2, eye).reshape(NSA_KVW, NSA_KVW))
        b1.append(jnp.tile(bias1, NSA_KV)[None])
        b2.append(jnp.tile(bias2, NSA_KV)[None])
    r = np.arange(PAGE)
    perm = np.zeros((PAGE, PAGE), np.float32)
    perm[r, CMP_STRIDE * (r % 8) + r // 8] = 1.0
    return {"perm": jnp.asarray(perm, BF16), "bd1": jnp.stack(bd1).astype(BF16), "b1": jnp.stack(b1),
            "bd2": jnp.stack(bd2).astype(BF16), "b2": jnp.stack(b2)}


def _chunk_score_matrix(n_blocks_in, n_sb):
    lanes = -(-n_sb // 128) * 128
    n = np.arange(n_blocks_in)[:, None]
    b = np.arange(lanes)[None, :]
    m = ((n >= 4 * b) & (n <= 4 * b + 3)).astype(np.float32) + ((n + 1 >= 4 * b) & (n + 1 <= 4 * b + 3)).astype(np.float32)
    m = np.where(b < n_sb, m, 0.0)
    return jnp.asarray(m, F32)


def _pad_seq(a, axis):
    pads = [(0, 0)] * a.ndim
    pads[axis] = (PADF, PADF)
    return jnp.pad(a, pads)


def kernel(x_prompt, x_sample, cache_cmp_kv, cache_slc_kv, state_win_kv, cache_diff_kv, page_table,
           nsa_w_in, nsa_w_out, cmp_k_w1, cmp_k_b1, cmp_k_w2, cmp_k_b2, cmp_v_w1, cmp_v_b1, cmp_v_w2, cmp_v_b2,
           diff_w_in, diff_w_out, lambda_q1, lambda_k1, lambda_q2, lambda_k2, diff_subln_gain, rel_bias,
           ln_gain, ln_bias, router_w, router_b, moe_w_gate_up, moe_w_down):
    b, s, d = x_prompt.shape
    bd, t_new, _ = x_sample.shape
    n_pages = page_table.shape[1]
    past = n_pages * PAGE
    assert s % KT == 0 and s >= WINDOW and past >= WINDOW and state_win_kv.shape[1] == WINDOW
    assert t_new <= CMP_STRIDE and (NSA_HEADS * t_new) % 8 == 0
    n_p = b * s
    n_s = bd * t_new
    x_all = jnp.concatenate([x_prompt.reshape(n_p, d), x_sample.reshape(n_s, d)], axis=0)

    hq = NSA_HEADS * NSA_DH
    wr_t = router_w.T
    rb = router_b[:, None]
    w_gu = moe_w_gate_up.astype(BF16)
    w_dn = moe_w_down.astype(BF16)
    t640 = _toeplitz_bias(rel_bias, QB, WINDOW + QB, WINDOW, 0, WINDOW, False)
    t_near = _toeplitz_bias(rel_bias, QB, 2 * QB, QB, 0, 1 << 30, True)
    n_pp = 16 if n_pages % 16 == 0 else 8
    assert n_pages % n_pp == 0

    w = nsa_w_in.astype(BF16)
    wg = jnp.pad(w[:, hq + 6 * NSA_KVW:], ((0, 0), (0, 128 - 3 * NSA_HEADS)))
    cols = [w[:, :hq]] + [w[:, hq + 2 * k * NSA_KVW: hq + 2 * (k + 1) * NSA_KVW] for k in range(3)] + [wg]
    q, cmp_rows, slc_rows, win_rows, gates = _project(
        x_all, cols, ("qscale", None, None, None, "sigmoid"), (BF16, F32, F32, F32, F32))
    cw = _compress_weights((cmp_k_w1, cmp_k_b1, cmp_k_w2, cmp_k_b2), (cmp_v_w1, cmp_v_b1, cmp_v_w2, cmp_v_b2))

    kvw = 2 * NSA_KVW
    ck, cv = _compress_prompt(cmp_rows[:n_p].reshape(b, s, kvw), cw)

    def per_group(a):
        return a.reshape(b, a.shape[1], NSA_KV, NSA_DH).transpose(0, 2, 1, 3)

    def kv_split(rows):
        r = rows[:n_p].reshape(b, s, kvw).astype(BF16)
        return _pad_seq(per_group(r[:, :, :NSA_KVW]), 2), _pad_seq(per_group(r[:, :, NSA_KVW:]), 2)

    sk, sv = kv_split(slc_rows)
    wk, wv = kv_split(win_rows)
    q4 = q[:n_p].reshape(b, s, NSA_HEADS, NSA_DH).transpose(0, 2, 1, 3)
    n_sb_p = s // SEL_BLOCK
    o4 = _nsa_prompt_attention(q4, gates[:n_p].reshape(b, s, 128), per_group(ck), per_group(cv), sk, sv, wk, wv,
                               t640, t_near, _chunk_score_matrix(s // CMP_STRIDE, n_sb_p), n_sb_p)
    o_p = o4.transpose(0, 2, 1, 3).reshape(n_p, hq)

    nrows = NSA_HEADS * t_new
    n_sb_s = -(-(past + t_new) // SEL_BLOCK)
    qs = q[n_p:].reshape(bd, t_new, NSA_KV, NSA_J, NSA_DH).transpose(0, 3, 2, 1, 4)
    qbd = jnp.einsum("bjgtd,gh->bjgthd", qs, jnp.eye(NSA_KV, dtype=BF16)).reshape(bd, nrows, NSA_KVW)

    def new_rows(rows, n):
        return jnp.pad(rows[n_p:].reshape(bd, t_new, kvw), ((0, 0), (0, n - t_new), (0, 0)))

    ocmp, sel = _nsa_sample_a(page_table, cache_cmp_kv.reshape(-1, PAGE, kvw), new_rows(cmp_rows, CMP_STRIDE), qbd, cw,
                              _chunk_score_matrix(n_pages * 8, n_sb_s), past, t_new, n_sb_s, n_pp)
    head_of_row = (np.arange(nrows) // t_new % NSA_KV) * NSA_J + np.arange(nrows) // (t_new * NSA_KV)
    tok_of_row = np.arange(nrows) % t_new
    tsn = _toeplitz_bias(rel_bias, t_new, 2 * PAGE, PAGE, 0, 1 << 30, True)[head_of_row, tok_of_row]
    tsw_full = _toeplitz_bias(rel_bias, t_new, WINDOW + PAGE, WINDOW, 0, WINDOW, False)
    tsw_full = jnp.where(jnp.arange(WINDOW + PAGE)[None, None, :] < WINDOW + t_new, tsw_full, NEG)
    tsw = tsw_full[head_of_row, tok_of_row]
    gs = gates[n_p:, :3 * NSA_HEADS].reshape(bd, t_new, NSA_KV, NSA_J, 3).transpose(0, 3, 2, 1, 4).reshape(bd, nrows, 3)
    fold = jnp.asarray(np.tile(np.eye(NSA_DH, dtype=np.float32), (NSA_KV, 1)), BF16)
    win_state = state_win_kv.reshape(bd, WINDOW, kvw)
    o_s = _nsa_sample_b(page_table, cache_slc_kv.reshape(-1, PAGE, kvw), qbd, sel, new_rows(slc_rows, PAGE), win_state,
                        new_rows(win_rows, PAGE), ocmp, gs, tsn, tsw, fold, n_pp)
    o_s = o_s.reshape(bd, NSA_J, NSA_KV, t_new, NSA_DH).transpose(0, 3, 2, 1, 4).reshape(n_s, hq)

    ones = jnp.ones((1, DIFF_DV), F32)
    x1, eidx, gate = _post_attention(jnp.concatenate([o_p, o_s], axis=0), x_all, nsa_w_out.astype(BF16),
                                     ln_gain[0, 0][None], ln_bias[0, 0][None], wr_t, rb, ones, None)
    x_all = _moe_layer(x1, eidx, gate, w_gu[0], w_dn[0], ln_gain[0, 1][None], ln_bias[0, 1][None])

    lam_init = 0.8 - 0.6 * math.exp(-0.3 * 1)
    lams = [a[None] for a in (lambda_q1, lambda_k1, lambda_q2, lambda_k2)]
    wd = diff_w_in.astype(BF16)
    dq = DIFF_HEADS * 2 * DIFF_DH
    qd, diff_rows = _project(x_all, [wd[:, :dq], wd[:, dq:]], ("qscale", None), (BF16, F32))
    dkw = DIFF_KV * DIFF_DV
    tn_d = t_near.reshape(DIFF_KV, DIFF_J, 2, QB, 2 * QB).transpose(0, 2, 1, 3, 4)
    q6 = qd[:n_p].reshape(b, s, DIFF_KV, DIFF_J, 2, DIFF_DH).transpose(0, 2, 4, 3, 1, 5)
    dr = diff_rows[:n_p].reshape(b, s, 2, DIFF_KV, DIFF_DV).astype(BF16)
    k5 = _pad_seq(dr[:, :, 0].reshape(b, s, DIFF_KV, 2, DIFF_DH).transpose(0, 2, 3, 1, 4), 3)
    v4 = _pad_seq(dr[:, :, 1].transpose(0, 2, 1, 3), 2)
    o_p = _diff_prompt_attention(q6, k5, v4, tn_d, lams, lam_init).reshape(n_p, DIFF_HEADS * DIFF_DV)

    nrows_d = 2 * DIFF_HEADS * t_new
    qsd = qd[n_p:].reshape(bd, t_new, DIFF_KV, DIFF_J, 2, DIFF_DH).transpose(0, 4, 2, 3, 1, 5)
    eye_gm = jnp.asarray(np.einsum("gG,mM->mgGM", np.eye(DIFF_KV), np.eye(2)), BF16)
    qbd_d = jnp.einsum("bmgjtd,mgGM->bmgjtGMd", qsd, eye_gm).reshape(bd, nrows_d, dkw)
    r = np.arange(nrows_d)
    col_of_row = (r // (DIFF_J * t_new) % DIFF_KV) * 4 + (r // t_new % DIFF_J) * 2 + r // (DIFF_KV * DIFF_J * t_new)
    tdn_full = _toeplitz_bias(rel_bias, t_new, 2 * PAGE, PAGE, 0, 1 << 30, True)
    tdn_full = jnp.where(jnp.arange(2 * PAGE)[None, None, :] < PAGE + t_new, tdn_full, NEG)
    tdn = tdn_full[col_of_row, r % t_new]
    dnew = jnp.pad(diff_rows[n_p:].reshape(bd, t_new, 2 * dkw), ((0, 0), (0, PAGE - t_new), (0, 0)))
    n_pp_d = 8
    o_s = _diff_sample(page_table, cache_diff_kv.reshape(-1, PAGE, 2 * dkw), qbd_d, dnew, tdn, lams, lam_init, n_pp_d)
    o_s = o_s.reshape(bd, DIFF_KV, DIFF_J, t_new, DIFF_DV).transpose(0, 3, 1, 2, 4).reshape(n_s, DIFF_HEADS * DIFF_DV)

    x1, eidx, gate = _post_attention(jnp.concatenate([o_p, o_s], axis=0), x_all, diff_w_out.astype(BF16),
                                     ln_gain[1, 0][None], ln_bias[1, 0][None], wr_t, rb, diff_subln_gain[None],
                                     1.0 - lam_init)
    x_all = _moe_layer(x1, eidx, gate, w_gu[1], w_dn[1], ln_gain[1, 1][None], ln_bias[1, 1][None])

    kv5 = (2, NSA_KV, NSA_DH)
    cmp_p = cmp_rows[:n_p].reshape((b, s) + kv5)
    cmp_s = cmp_rows[n_p:].reshape((bd, t_new) + kv5)
    slc_p = slc_rows[:n_p].reshape((b, s) + kv5)
    slc_s = slc_rows[n_p:].reshape((bd, t_new) + kv5)
    win_all_p = win_rows[:n_p].reshape((b, s) + kv5)
    win_p = win_all_p[:, s - WINDOW:]
    win_s = jnp.concatenate([state_win_kv, win_rows[n_p:].reshape((bd, t_new) + kv5)], axis=1)[:, t_new:]
    dshape = (2, DIFF_KV, DIFF_DV)
    return (x_all[:n_p].reshape(b, s, d), x_all[n_p:].reshape(bd, t_new, d), cmp_p, cmp_s, slc_p, slc_s, win_p, win_s,
            diff_rows[:n_p].reshape((b, s) + dshape), diff_rows[n_p:].reshape((bd, t_new) + dshape))
```

```python
import functools
import math

import jax
import jax.numpy as jnp
import numpy as np
from jax import lax
from jax.experimental import pallas as pl
from jax.experimental.pallas import tpu as pltpu

F32 = jnp.float32
BF16 = jnp.bfloat16
I32 = jnp.int32
HI = lax.Precision.HIGHEST
NT_DIMS = (((1,), (1,)), ((), ()))

NSA_HEADS = 16
NSA_KV = 4
NSA_J = NSA_HEADS // NSA_KV
NSA_DH = 64
NSA_KVW = NSA_KV * NSA_DH
CMP_BLOCK = 32
CMP_STRIDE = 16
CMP_SPAN = CMP_BLOCK // CMP_STRIDE
SEL_BLOCK = 64
SEL_CHUNKS = SEL_BLOCK // CMP_STRIDE
SEL_TOPN = 16
WINDOW = 512
QB = 128
DIFF_HEADS = 8
DIFF_KV = 4
DIFF_J = DIFF_HEADS // DIFF_KV
DIFF_DH = 64
DIFF_DV = 2 * DIFF_DH
NUM_BUCKETS = 32
MAX_DISTANCE = 128
N_EXPERTS = 16
N_GROUPS = 4
EPG = N_EXPERTS // N_GROUPS
TOP_K = 2
DEPTH = 2
DN_ALPHA = (2 * DEPTH) ** 0.25
LN_EPS = 1e-5

PAGE = 128
KT = 512
KT_DIFF = 1024
PADF = 512
AUG = 16
NSA_QW = NSA_DH + AUG
DIFF_VW = DIFF_DV + AUG
MOE_TM = 512
DMA_UNROLL = 8
NEG = -1e30
BIG = 1e30
VMEM_LIMIT = 56 * 1024 * 1024


def _cparams(sem):
    return pltpu.CompilerParams(dimension_semantics=sem, vmem_limit_bytes=VMEM_LIMIT)


def _token_tile(nt):
    for tm in (512, 384, 256, 128):
        if nt % tm == 0:
            return tm
    raise ValueError(f"token count {nt} is not a multiple of 128")


def _bucket_np(dist):
    n = np.maximum(dist, 0)
    max_exact = NUM_BUCKETS // 2
    nf = np.maximum(n, 1).astype(np.float32)
    large = max_exact + (np.log(nf / np.float32(max_exact)) / np.float32(math.log(MAX_DISTANCE / max_exact))
                         * np.float32(NUM_BUCKETS - max_exact)).astype(np.int32)
    large = np.minimum(large, NUM_BUCKETS - 1)
    return np.where(n < max_exact, n, large).astype(np.int32)


def _toeplitz_bias(rel_bias, n_rows, n_cols, offset, lo, hi, shift_far):
    t = np.arange(n_rows)[:, None]
    l = np.arange(n_cols)[None, :]
    d = t + offset - l
    tab = rel_bias[_bucket_np(d)]
    if shift_far:
        tab = tab - rel_bias[NUM_BUCKETS - 1][None, None, :]
    ok = (d >= lo) & (d < hi)
    tab = jnp.where(jnp.asarray(ok)[:, :, None], tab, NEG)
    return jnp.transpose(tab, (2, 0, 1)).astype(F32)


def _gelu_tanh(x):
    return 0.5 * x * (1.0 + jnp.tanh(math.sqrt(2.0 / math.pi) * (x + 0.044715 * (x * x * x))))


def _layer_norm(y, g, b):
    mu = jnp.mean(y, axis=-1, keepdims=True)
    d = y - mu
    var = jnp.mean(d * d, axis=-1, keepdims=True)
    return d * lax.rsqrt(var + LN_EPS) * g + b


def _online_update(carry, s, pv):
    m, l, acc = carry
    m_new = jnp.maximum(m, jnp.max(s, axis=-1, keepdims=True))
    alpha = jnp.exp(m - m_new)
    p = jnp.exp(s - m_new)
    l = alpha * l + jnp.sum(p, axis=-1, keepdims=True)
    acc = alpha * acc + pv(p.astype(BF16))
    return m_new, l, acc


def _pv_t(vt):
    return lambda p: lax.dot_general(p, vt, NT_DIMS, preferred_element_type=F32)


def _flash_step(carry, s, vt_aug):
    m, acc = carry
    m_new = jnp.maximum(m, jnp.max(s, axis=-1, keepdims=True))
    alpha = jnp.exp(m - m_new)
    p = jnp.exp((s - m_new).astype(BF16))
    acc = alpha * acc + lax.dot_general(p, vt_aug, NT_DIMS, preferred_element_type=F32)
    return m_new, acc


def _softmax_pv_t(s, vt):
    m = jnp.max(s, axis=-1, keepdims=True)
    e = jnp.exp(s - m)
    l = jnp.sum(e, axis=-1, keepdims=True)
    return lax.dot_general(e.astype(BF16), vt, NT_DIMS, preferred_element_type=F32) / l


def _flash_init(rows, width):
    return jnp.full((rows, 1), NEG, F32), jnp.zeros((rows, width), F32)


def _flash_out(acc, dv):
    return acc[:, 0:dv] / acc[:, dv:dv + 1]


def _dot_exact_rhs(a, b):
    hi = a.astype(BF16)
    r1 = a - hi.astype(F32)
    mid = r1.astype(BF16)
    lo = (r1 - mid.astype(F32)).astype(BF16)
    return (jnp.dot(hi, b, preferred_element_type=F32) + jnp.dot(mid, b, preferred_element_type=F32)
            + jnp.dot(lo, b, preferred_element_type=F32))


def _masked_softmax(s, valid):
    sm = jnp.where(valid, s, NEG)
    m = jnp.max(sm, axis=-1, keepdims=True)
    e = jnp.where(valid, jnp.exp(sm - m), 0.0)
    z = jnp.sum(e, axis=-1, keepdims=True)
    return e / jnp.where(z > 0, z, 1.0)


def _topn_mask(score, n_sel, axis):
    n = score.shape[axis]
    pos = lax.broadcasted_iota(I32, score.shape, axis).astype(F32)

    def body(_, c):
        sc, sel = c
        m = jnp.max(sc, axis=axis, keepdims=True)
        first = jnp.min(jnp.where(sc == m, pos, float(n)), axis=axis, keepdims=True)
        one = pos == first
        sel = jnp.where(jnp.logical_and(one, m > -BIG), 1.0, sel)
        sc = jnp.where(one, -BIG, sc)
        return sc, sel

    _, sel = lax.fori_loop(0, n_sel, body, (score, jnp.zeros_like(score)))
    return sel


def _block_expand(sel_bf16, first_block, n_keys):
    nb = sel_bf16.shape[-1]
    kb = lax.broadcasted_iota(I32, (nb, n_keys), 1) // SEL_BLOCK + first_block
    e = (kb == lax.broadcasted_iota(I32, (nb, n_keys), 0)).astype(BF16)
    return jnp.dot(sel_bf16, e, preferred_element_type=F32)


def _select_blocks(score, qpos, n_sb, blocks_on_sublanes):
    nl = score.shape[-1]
    blk = lax.broadcasted_iota(I32, (1, nl), 1)
    cur = qpos // SEL_BLOCK
    validb = jnp.logical_and(blk * SEL_BLOCK <= qpos, blk < n_sb)
    forced = jnp.logical_or(blk == 0, jnp.logical_or(blk == cur, blk == cur - 1))
    forced = jnp.logical_and(forced, blk < n_sb)
    sc = jnp.where(forced, BIG, jnp.where(validb, score, -BIG))
    n_sel = min(SEL_TOPN, n_sb)
    if blocks_on_sublanes:
        return _topn_mask(sc.T, n_sel, 0).T
    return _topn_mask(sc, n_sel, 1)


def _page_parts(pages, perm, bd1_ref):
    permuted = [lax.dot_general(perm, p.astype(BF16), NT_DIMS, preferred_element_type=F32) for p in pages]
    nrow = 8 * len(pages)
    acc = [jnp.zeros((nrow, 2 * NSA_KVW), F32) for _ in range(2)]
    for s in range(CMP_STRIDE):
        a = jnp.concatenate([pp[s * 8:(s + 1) * 8] for pp in permuted], axis=0).astype(BF16)
        for kv in range(2):
            acc[kv] = acc[kv] + jnp.dot(a[:, kv * NSA_KVW:(kv + 1) * NSA_KVW], bd1_ref[kv, s],
                                        preferred_element_type=F32)
    return jnp.concatenate(acc, axis=1)


def _compress_finalize(p_ref, n, b1_ref, bd2_ref, b2_ref):
    outs = []
    for kv in range(2):
        c0 = kv * 2 * NSA_KVW
        p0 = p_ref[pl.ds(0, n), c0:c0 + NSA_KVW]
        p1 = p_ref[pl.ds(1, n), c0 + NSA_KVW:c0 + 2 * NSA_KVW]
        h = _gelu_tanh(p0 + p1 + b1_ref[kv])
        outs.append(jnp.dot(h.astype(BF16), bd2_ref[kv], preferred_element_type=F32) + b2_ref[kv])
    return outs


def _proj_kernel(x_ref, *refs, post):
    n = len(post)
    x = x_ref[...].astype(BF16)
    for w_ref, o_ref, p in zip(refs[:n], refs[n:], post):
        if p == "transposed":
            y = lax.dot_general(w_ref[...], x, NT_DIMS, preferred_element_type=F32)
        else:
            y = jnp.dot(x, w_ref[...], preferred_element_type=F32)
        if p == "qscale":
            y = y * 0.125
        elif p == "sigmoid":
            y = jax.nn.sigmoid(y)
        o_ref[...] = y.astype(o_ref.dtype)


def _project(x, weights, post, dtypes, seq=None):
    nt, d = x.shape
    tm = _token_tile(nt if seq is None else math.gcd(nt, seq))
    in_specs = [pl.BlockSpec((tm, d), lambda i: (i, 0))]
    in_specs += [pl.BlockSpec(w.shape, lambda i: (0, 0)) for w in weights]
    out_specs, out_shape = [], []
    for w, p, dt in zip(weights, post, dtypes):
        if p == "transposed":
            per = seq // tm
            out_specs.append(pl.BlockSpec((None, w.shape[0], tm), lambda i, per=per: (i // per, 0, i % per)))
            out_shape.append(jax.ShapeDtypeStruct((nt // seq, w.shape[0], seq), dt))
        else:
            out_specs.append(pl.BlockSpec((tm, w.shape[1]), lambda i: (i, 0)))
            out_shape.append(jax.ShapeDtypeStruct((nt, w.shape[1]), dt))
    return pl.pallas_call(
        functools.partial(_proj_kernel, post=post),
        grid=(nt // tm,), in_specs=in_specs, out_specs=out_specs, out_shape=out_shape,
        compiler_params=_cparams(("parallel",)), name="proj",
    )(x, *weights)


def _cmp_prompt_kernel(rows_ref, perm_ref, bd1_ref, b1_ref, bd2_ref, b2_ref, ck_ref, cv_ref, p_ref, *, rt, n_chunks):
    i = pl.program_id(1)

    @pl.when(i == 0)
    def _():
        p_ref[pl.ds(n_chunks, 8), :] = jnp.zeros((8, p_ref.shape[1]), F32)

    pages = [rows_ref[0, :, k * PAGE:(k + 1) * PAGE] for k in range(rt // PAGE)]
    nrow = rt // CMP_STRIDE
    p_ref[pl.ds(pl.multiple_of(i * nrow, nrow), nrow), :] = _page_parts(pages, perm_ref[...], bd1_ref)

    @pl.when(i == pl.num_programs(1) - 1)
    def _():
        ck, cv = _compress_finalize(p_ref, n_chunks, b1_ref, bd2_ref, b2_ref)
        ck_ref[0] = ck.astype(ck_ref.dtype)
        cv_ref[0] = cv.astype(cv_ref.dtype)


def _compress_prompt(rows_t, cw):
    b, w, s = rows_t.shape
    rt = min(2048, s)
    n_chunks = s // CMP_STRIDE
    const = lambda a: pl.BlockSpec(a.shape, lambda bi, i, _n=a.ndim: (0,) * _n)
    return pl.pallas_call(
        functools.partial(_cmp_prompt_kernel, rt=rt, n_chunks=n_chunks),
        grid=(b, s // rt),
        in_specs=[pl.BlockSpec((1, w, rt), lambda bi, i: (bi, 0, i)),
                  const(cw["perm"]), const(cw["bd1"]), const(cw["b1"]), const(cw["bd2"]), const(cw["b2"])],
        out_specs=[pl.BlockSpec((1, n_chunks, NSA_KVW), lambda bi, i: (bi, 0, 0))] * 2,
        out_shape=[jax.ShapeDtypeStruct((b, n_chunks, NSA_KVW), BF16)] * 2,
        scratch_shapes=[pltpu.VMEM((n_chunks + 8, 4 * NSA_KVW), F32)],
        compiler_params=_cparams(("parallel", "arbitrary")), name="cmp_prompt",
    )(rows_t, cw["perm"], cw["bd1"], cw["b1"], cw["bd2"], cw["b2"])


def _nsa_prompt_kernel(q_ref, gates_ref, ck_ref, cv_ref, sk_ref, sv_ref, wk_ref, wv_ref, tw_ref, tn_ref, cm_ref,
                       o_ref, *, n_sb):
    g = pl.program_id(1)
    i = pl.program_id(2)
    start = i * QB
    rows = NSA_J * QB
    q2 = q_ref[0].reshape(rows, NSA_QW)
    qpos = start + lax.broadcasted_iota(I32, (QB, 1), 0)

    ck = ck_ref[0, 0]
    nbp = ck.shape[0]
    lc = lax.dot_general(q2, ck, NT_DIMS, preferred_element_type=F32).reshape(NSA_J, QB, nbp)
    c_end = lax.broadcasted_iota(I32, (1, nbp), 1) * CMP_STRIDE + (CMP_BLOCK - 1)
    pc = _masked_softmax(lc, (c_end <= qpos)[None])
    o_cmp = jnp.dot(pc.reshape(rows, nbp).astype(BF16), cv_ref[0, 0], preferred_element_type=F32)
    ps = pc[0] + pc[1] + pc[2] + pc[3]
    score = _dot_exact_rhs(ps, cm_ref[...])
    sel = _select_blocks(score, qpos, n_sb, True)

    blk = lax.broadcasted_iota(I32, (1, sel.shape[1]), 1)
    near0 = 2 * i - 2
    neg_all = jnp.where(sel > 0.5, 0.0, NEG)
    neg_far = jnp.where(blk < near0, neg_all, NEG)
    q_far = jnp.concatenate([jnp.concatenate([neg_far.astype(BF16)] * NSA_J, axis=0), q2], axis=1)
    q_near = jnp.concatenate([jnp.concatenate([neg_all.astype(BF16)] * NSA_J, axis=0), q2], axis=1)
    n_far = (jnp.maximum(i - 1, 0) * QB + KT - 1) // KT

    def far_body(kt, carry):
        off = pl.multiple_of(PADF + kt * KT, KT)
        s = jnp.dot(q_far, sk_ref[0, 0, :, pl.ds(off, KT)], preferred_element_type=F32)
        return _flash_step(carry, s, sv_ref[0, 0, :, pl.ds(off, KT)])

    carry = lax.fori_loop(0, n_far, far_body, _flash_init(rows, NSA_QW))

    offn = pl.multiple_of(PADF + start - QB, QB)
    s = jnp.dot(q_near, sk_ref[0, 0, :, pl.ds(offn, 2 * QB)], preferred_element_type=F32)
    s = (s.reshape(NSA_J, QB, 2 * QB) + tn_ref[...]).reshape(rows, 2 * QB)
    _, acc = _flash_step(carry, s, sv_ref[0, 0, :, pl.ds(offn, 2 * QB)])
    o_slc = _flash_out(acc, NSA_DH)

    lw = WINDOW + QB
    offw = pl.multiple_of(PADF + start - WINDOW, QB)
    s = jnp.dot(q2, wk_ref[0, 0, :, pl.ds(offw, lw)], preferred_element_type=F32)
    s = (s.reshape(NSA_J, QB, lw) + tw_ref[...]).reshape(rows, lw)
    _, acc = _flash_step(_flash_init(rows, NSA_QW), s, wv_ref[0, 0, :, pl.ds(offw, lw)])
    o_win = _flash_out(acc, NSA_DH)

    gt = gates_ref[0]
    colid = lax.broadcasted_iota(I32, gt.shape, 1)
    for j in range(NSA_J):
        out = jnp.zeros((QB, NSA_DH), F32)
        for c, ob in enumerate((o_cmp, o_slc, o_win)):
            gcol = jnp.sum(jnp.where(colid == 3 * (NSA_J * g + j) + c, gt, 0.0), axis=-1, keepdims=True)
            out = out + gcol * ob[j * QB:(j + 1) * QB]
        o_ref[0, j] = out.astype(o_ref.dtype)


def _nsa_prompt_attention(q4, gates, ck, cv, sk, sv, wk, wv, t_win, t_near, cmat, n_sb):
    b, h, s, qw = q4.shape
    nq = s // QB
    per_bg = lambda a: pl.BlockSpec((1, 1) + a.shape[2:], lambda bi, g, i: (bi, g, 0, 0))
    return pl.pallas_call(
        functools.partial(_nsa_prompt_kernel, n_sb=n_sb),
        grid=(b, NSA_KV, nq),
        in_specs=[pl.BlockSpec((1, NSA_J, QB, qw), lambda bi, g, i: (bi, g, i, 0)),
                  pl.BlockSpec((1, QB, gates.shape[2]), lambda bi, g, i: (bi, i, 0)),
                  per_bg(ck), per_bg(cv), per_bg(sk), per_bg(sv), per_bg(wk), per_bg(wv),
                  pl.BlockSpec((NSA_J, QB, t_win.shape[2]), lambda bi, g, i: (g, 0, 0)),
                  pl.BlockSpec((NSA_J, QB, t_near.shape[2]), lambda bi, g, i: (g, 0, 0)),
                  pl.BlockSpec(cmat.shape, lambda bi, g, i: (0, 0))],
        out_specs=pl.BlockSpec((1, NSA_J, QB, NSA_DH), lambda bi, g, i: (bi, g, i, 0)),
        out_shape=jax.ShapeDtypeStruct((b, h, s, NSA_DH), BF16),
        compiler_params=_cparams(("parallel", "parallel", "arbitrary")), name="nsa_prompt",
    )(q4, gates, ck, cv, sk, sv, wk, wv, t_win, t_near, cmat)


def _diff_lambda(lq1_ref, lk1_ref, lq2_ref, lk2_ref, lam_init):
    a = jnp.sum(lq1_ref[...] * lk1_ref[...], axis=-1, keepdims=True)
    b = jnp.sum(lq2_ref[...] * lk2_ref[...], axis=-1, keepdims=True)
    return jnp.exp(a) - jnp.exp(b) + lam_init


def _diff_prompt_kernel(q_ref, k_ref, v_ref, tn_ref, lq1_ref, lk1_ref, lq2_ref, lk2_ref, o_ref, *, lam_init):
    i = pl.program_id(2)
    start = i * QB
    rows = DIFF_J * QB
    far_end = jnp.maximum(start - QB, 0)
    n_far = (far_end + KT_DIFF - 1) // KT_DIFF
    q2 = [q_ref[0, 0, m].reshape(rows, DIFF_DH) for m in range(2)]

    def far_body(kt, carry):
        off = pl.multiple_of(PADF + kt * KT_DIFF, PADF)
        v = v_ref[0, 0, :, pl.ds(off, KT_DIFF)]
        keep = kt * KT_DIFF + lax.broadcasted_iota(I32, (1, KT_DIFF), 1) < far_end
        new = []
        for m in range(2):
            s = jnp.dot(q2[m], k_ref[0, 0, m, :, pl.ds(off, KT_DIFF)], preferred_element_type=F32)
            new.append(_flash_step(carry[m], jnp.where(keep, s, NEG), v))
        return tuple(new)

    init = _flash_init(rows, DIFF_VW)
    carry = lax.fori_loop(0, n_far, far_body, (init, init))
    offn = pl.multiple_of(PADF + start - QB, QB)
    v = v_ref[0, 0, :, pl.ds(offn, 2 * QB)]
    in_seq = start - QB + lax.broadcasted_iota(I32, (1, 2 * QB), 1) >= 0
    outs = []
    for m in range(2):
        s = jnp.dot(q2[m], k_ref[0, 0, m, :, pl.ds(offn, 2 * QB)], preferred_element_type=F32)
        s = (s.reshape(DIFF_J, QB, 2 * QB) + tn_ref[m]).reshape(rows, 2 * QB)
        _, acc = _flash_step(carry[m], jnp.where(in_seq, s, NEG), v)
        outs.append(_flash_out(acc, DIFF_DV))
    lam = _diff_lambda(lq1_ref, lk1_ref, lq2_ref, lk2_ref, lam_init)
    o = outs[0] - lam * outs[1]
    o_ref[0] = jnp.concatenate([o[j * QB:(j + 1) * QB] for j in range(DIFF_J)], axis=1)


def _diff_prompt_attention(q6, k5, v4, t_near, lams, lam_init):
    b, g, _, _, s, dh = q6.shape
    sp = k5.shape[4]
    nq = s // QB
    lam_spec = pl.BlockSpec((1, DIFF_DH), lambda bi, gi, i: (0, 0))
    return pl.pallas_call(
        functools.partial(_diff_prompt_kernel, lam_init=lam_init),
        grid=(b, g, nq),
        in_specs=[pl.BlockSpec((1, 1, 2, DIFF_J, QB, dh), lambda bi, gi, i: (bi, gi, 0, 0, i, 0)),
                  pl.BlockSpec((1, 1, 2, dh, sp), lambda bi, gi, i: (bi, gi, 0, 0, 0)),
                  pl.BlockSpec((1, 1, DIFF_VW, sp), lambda bi, gi, i: (bi, gi, 0, 0)),
                  pl.BlockSpec((None, 2, DIFF_J, QB, 2 * QB), lambda bi, gi, i: (gi, 0, 0, 0, 0)),
                  lam_spec, lam_spec, lam_spec, lam_spec],
        out_specs=pl.BlockSpec((1, QB, DIFF_J * DIFF_DV), lambda bi, gi, i: (bi, i, gi)),
        out_shape=jax.ShapeDtypeStruct((b, s, DIFF_HEADS * DIFF_DV), F32),
        compiler_params=_cparams(("parallel", "parallel", "arbitrary")), name="diff_prompt",
    )(q6, k5, v4, t_near, *lams)


def _nsa_samp_a_kernel(pt_ref, *refs, n_pp, n_pages, past, t_new, n_sb):
    page_refs = refs[:n_pp]
    (xc_ref, qbd_ref, perm_ref, bd1_ref, b1_ref, bd2_ref, b2_ref, cm_ref, ocmp_ref, sel_ref, p_ref) = refs[n_pp:]
    i = pl.program_id(1)
    nrow = n_pp * 8
    parts = _page_parts([r[...] for r in page_refs], perm_ref[...], bd1_ref)
    p_ref[pl.ds(pl.multiple_of(i * nrow, nrow), nrow), :] = parts

    @pl.when(i == pl.num_programs(1) - 1)
    def _():
        xc = xc_ref[0]
        accs = []
        for kv in range(2):
            acc = jnp.zeros((1, 2 * NSA_KVW), F32)
            for s in range(CMP_STRIDE):
                acc = acc + jnp.dot(xc[s:s + 1, kv * NSA_KVW:(kv + 1) * NSA_KVW].astype(BF16), bd1_ref[kv, s],
                                    preferred_element_type=F32)
            accs.append(acc)
        n_cached = n_pages * 8
        p_ref[pl.ds(n_cached, 1), :] = jnp.concatenate(accs, axis=1)
        ck, cv = _compress_finalize(p_ref, n_cached, b1_ref, bd2_ref, b2_ref)
        qbd = qbd_ref[0]
        nrows = qbd.shape[0]
        lc = lax.dot_general(qbd, ck.astype(BF16), NT_DIMS, preferred_element_type=F32)
        qpos = past + lax.broadcasted_iota(I32, (nrows, 1), 0) % t_new
        c_end = lax.broadcasted_iota(I32, (1, n_cached), 1) * CMP_STRIDE + (CMP_BLOCK - 1)
        pc = _masked_softmax(lc, c_end <= qpos)
        ocmp_ref[0] = jnp.dot(pc.astype(BF16), cv.astype(BF16), preferred_element_type=F32)
        rg = nrows // NSA_J
        ps = pc[0:rg] + pc[rg:2 * rg] + pc[2 * rg:3 * rg] + pc[3 * rg:4 * rg]
        score = _dot_exact_rhs(ps, cm_ref[...])
        sel_ref[0] = _select_blocks(score, qpos[0:rg], n_sb, False)


def _page_specs(n_pp, page_shape):
    def spec(k):
        return pl.BlockSpec((None,) + page_shape, lambda b, i, pt, k=k: (pt[b, i * n_pp + k], 0, 0))
    return [spec(k) for k in range(n_pp)]


def _nsa_sample_a(page_table, cache, xc, qbd, cw, cmat, past, t_new, n_sb, n_pp):
    bd, n_pages = page_table.shape
    nrows = qbd.shape[1]
    rg = nrows // NSA_J
    nsbp = cmat.shape[1]
    const = lambda a: pl.BlockSpec(a.shape, lambda b, i, pt, _n=a.ndim: (0,) * _n)
    grid_spec = pltpu.PrefetchScalarGridSpec(
        num_scalar_prefetch=1, grid=(bd, n_pages // n_pp),
        in_specs=_page_specs(n_pp, cache.shape[1:]) + [
            pl.BlockSpec((1, CMP_STRIDE, xc.shape[2]), lambda b, i, pt: (b, 0, 0)),
            pl.BlockSpec((1, nrows, NSA_KVW), lambda b, i, pt: (b, 0, 0)),
            const(cw["perm"]), const(cw["bd1"]), const(cw["b1"]), const(cw["bd2"]), const(cw["b2"]), const(cmat)],
        out_specs=[pl.BlockSpec((1, nrows, NSA_KVW), lambda b, i, pt: (b, 0, 0)),
                   pl.BlockSpec((1, rg, nsbp), lambda b, i, pt: (b, 0, 0))],
        scratch_shapes=[pltpu.VMEM((n_pages * 8 + 8, 4 * NSA_KVW), F32)])
    return pl.pallas_call(
        functools.partial(_nsa_samp_a_kernel, n_pp=n_pp, n_pages=n_pages, past=past, t_new=t_new, n_sb=n_sb),
        grid_spec=grid_spec,
        out_shape=[jax.ShapeDtypeStruct((bd, nrows, NSA_KVW), F32), jax.ShapeDtypeStruct((bd, rg, nsbp), F32)],
        compiler_params=_cparams(("parallel", "arbitrary")), name="nsa_sample_a",
    )(page_table, *([cache] * n_pp), xc, qbd, cw["perm"], cw["bd1"], cw["b1"], cw["bd2"], cw["b2"], cmat)


def _nsa_samp_b_kernel(pt_ref, *refs, n_pp, n_pages):
    page_refs = refs[:n_pp]
    (qbd_ref, sel_ref, snew_ref, wst_ref, wnew_ref, ocmp_ref, gate_ref, tsn_ref, tsw_ref, fold_ref,
     o_ref, m_ref, l_ref, acc_ref) = refs[n_pp:]
    i = pl.program_id(1)

    @pl.when(i == 0)
    def _():
        m_ref[...] = jnp.full(m_ref.shape, NEG, F32)
        l_ref[...] = jnp.zeros(l_ref.shape, F32)
        acc_ref[...] = jnp.zeros(acc_ref.shape, F32)

    qbd = qbd_ref[0]
    sel = jnp.concatenate([sel_ref[0]] * NSA_J, axis=0)
    near0 = 2 * (n_pages - 1)
    blk = lax.broadcasted_iota(I32, (1, sel.shape[1]), 1)
    sel_far = jnp.where(blk < near0, sel, 0.0).astype(BF16)
    nk = n_pp * PAGE
    kcat = jnp.concatenate([r[0:NSA_KVW, :] for r in page_refs], axis=1).astype(BF16)
    vcat = jnp.concatenate([r[NSA_KVW:2 * NSA_KVW, :] for r in page_refs], axis=1).astype(BF16)
    s = jnp.dot(qbd, kcat, preferred_element_type=F32)
    keep = _block_expand(sel_far, i * (nk // SEL_BLOCK), nk) > 0.5
    carry = _online_update((m_ref[...], l_ref[...], acc_ref[...]), jnp.where(keep, s, NEG), _pv_t(vcat))
    m_ref[...], l_ref[...], acc_ref[...] = carry

    @pl.when(i == pl.num_programs(1) - 1)
    def _():
        last = page_refs[n_pp - 1]
        snew = snew_ref[0]
        kn = jnp.concatenate([last[0:NSA_KVW, :], snew[0:NSA_KVW, :]], axis=1).astype(BF16)
        vn = jnp.concatenate([last[NSA_KVW:2 * NSA_KVW, :], snew[NSA_KVW:2 * NSA_KVW, :]], axis=1).astype(BF16)
        s = jnp.dot(qbd, kn, preferred_element_type=F32) + tsn_ref[...]
        keep = _block_expand(sel.astype(BF16), near0, 2 * PAGE) > 0.5
        _, l, acc = _online_update((m_ref[...], l_ref[...], acc_ref[...]), jnp.where(keep, s, NEG), _pv_t(vn))
        o_slc = acc / l
        wst = wst_ref[0]
        wnew = wnew_ref[0]
        wk = jnp.concatenate([wst[0:NSA_KVW, :], wnew[0:NSA_KVW, :]], axis=1).astype(BF16)
        wv = jnp.concatenate([wst[NSA_KVW:2 * NSA_KVW, :], wnew[NSA_KVW:2 * NSA_KVW, :]], axis=1).astype(BF16)
        s = jnp.dot(qbd, wk, preferred_element_type=F32) + tsw_ref[...]
        o_win = _softmax_pv_t(s, wv)
        gt = gate_ref[0]
        o_full = gt[:, 0:1] * ocmp_ref[0] + gt[:, 1:2] * o_slc + gt[:, 2:3] * o_win
        nrows = o_full.shape[0]
        t_new = nrows // NSA_HEADS
        row_g = (lax.broadcasted_iota(I32, o_full.shape, 0) // t_new) % NSA_KV
        col_g = lax.broadcasted_iota(I32, o_full.shape, 1) // NSA_DH
        od = jnp.where(row_g == col_g, o_full, 0.0).astype(BF16)
        o_ref[0] = jnp.dot(od, fold_ref[...], preferred_element_type=F32).astype(o_ref.dtype)


def _nsa_sample_b(page_table, cache, qbd, sel, snew, wst, wnew, ocmp, gates, tsn, tsw, fold, n_pp):
    bd, n_pages = page_table.shape
    nrows = qbd.shape[1]
    per_b = lambda a: pl.BlockSpec((1,) + a.shape[1:], lambda b, i, pt, _n=a.ndim: (b,) + (0,) * (_n - 1))
    const = lambda a: pl.BlockSpec(a.shape, lambda b, i, pt, _n=a.ndim: (0,) * _n)
    grid_spec = pltpu.PrefetchScalarGridSpec(
        num_scalar_prefetch=1, grid=(bd, n_pages // n_pp),
        in_specs=_page_specs(n_pp, cache.shape[1:]) + [
            per_b(qbd), per_b(sel), per_b(snew), per_b(wst), per_b(wnew), per_b(ocmp), per_b(gates),
            const(tsn), const(tsw), const(fold)],
        out_specs=pl.BlockSpec((1, nrows, NSA_DH), lambda b, i, pt: (b, 0, 0)),
        scratch_shapes=[pltpu.VMEM((nrows, 1), F32), pltpu.VMEM((nrows, 1), F32), pltpu.VMEM((nrows, NSA_KVW), F32)])
    return pl.pallas_call(
        functools.partial(_nsa_samp_b_kernel, n_pp=n_pp, n_pages=n_pages),
        grid_spec=grid_spec,
        out_shape=jax.ShapeDtypeStruct((bd, nrows, NSA_DH), BF16),
        compiler_params=_cparams(("parallel", "arbitrary")), name="nsa_sample_b",
    )(page_table, *([cache] * n_pp), qbd, sel, snew, wst, wnew, ocmp, gates, tsn, tsw, fold)


def _diff_samp_kernel(pt_ref, *refs, n_pp, lam_init):
    page_refs = refs[:n_pp]
    (qbd_ref, dnew_ref, tdn_ref, lq1_ref, lk1_ref, lq2_ref, lk2_ref, o_ref, m_ref, l_ref, acc_ref) = refs[n_pp:]
    i = pl.program_id(1)
    last_step = i == pl.num_programs(1) - 1
    kw = DIFF_KV * 2 * DIFF_DH

    @pl.when(i == 0)
    def _():
        m_ref[...] = jnp.full(m_ref.shape, NEG, F32)
        l_ref[...] = jnp.zeros(l_ref.shape, F32)
        acc_ref[...] = jnp.zeros(acc_ref.shape, F32)

    qbd = qbd_ref[0]
    rg = qbd.shape[0] // DIFF_KV
    nk = n_pp * PAGE
    groups = range(DIFF_KV)

    def k_rows(ref, g):
        return ref[pl.ds(g, PAGE, stride=2 * DIFF_KV), :]

    def v_rows(ref, g):
        return ref[pl.ds(DIFF_KV + g, PAGE, stride=2 * DIFF_KV), :]

    def scores(k_of):
        return jnp.concatenate([lax.dot_general(qbd[g * rg:(g + 1) * rg], k_of(g), NT_DIMS,
                                                preferred_element_type=F32) for g in groups], axis=0)

    def pv(v_of):
        return lambda p: jnp.concatenate([jnp.dot(p[g * rg:(g + 1) * rg], v_of(g), preferred_element_type=F32)
                                          for g in groups], axis=0)

    s = scores(lambda g: jnp.concatenate([k_rows(r, g) for r in page_refs], axis=0).astype(BF16))
    limit = jnp.where(last_step, nk - PAGE, nk)
    s = jnp.where(lax.broadcasted_iota(I32, (1, nk), 1) < limit, s, NEG)
    carry = _online_update((m_ref[...], l_ref[...], acc_ref[...]), s,
                           pv(lambda g: jnp.concatenate([v_rows(r, g) for r in page_refs], axis=0).astype(BF16)))
    m_ref[...], l_ref[...], acc_ref[...] = carry

    @pl.when(last_step)
    def _():
        last = page_refs[n_pp - 1]
        dnew = dnew_ref[0]
        s = scores(lambda g: jnp.concatenate(
            [k_rows(last, g), dnew[:, g * DIFF_DV:(g + 1) * DIFF_DV]], axis=0).astype(BF16)) + tdn_ref[...]
        _, l, acc = _online_update((m_ref[...], l_ref[...], acc_ref[...]), s, pv(lambda g: jnp.concatenate(
            [v_rows(last, g), dnew[:, kw + g * DIFF_DV:kw + (g + 1) * DIFF_DV]], axis=0).astype(BF16)))
        o_full = acc / l
        lam = _diff_lambda(lq1_ref, lk1_ref, lq2_ref, lk2_ref, lam_init)
        h = rg // 2
        o_ref[0] = jnp.concatenate([o_full[g * rg:g * rg + h] - lam * o_full[g * rg + h:(g + 1) * rg]
                                    for g in groups], axis=0)


def _diff_sample(page_table, cache, qbd, dnew, tdn, lams, lam_init, n_pp):
    bd, n_pages = page_table.shape
    nrows = qbd.shape[1]
    kw = DIFF_KV * 2 * DIFF_DH
    per_b = lambda a: pl.BlockSpec((1,) + a.shape[1:], lambda b, i, pt, _n=a.ndim: (b,) + (0,) * (_n - 1))
    const = lambda a: pl.BlockSpec(a.shape, lambda b, i, pt, _n=a.ndim: (0,) * _n)
    grid_spec = pltpu.PrefetchScalarGridSpec(
        num_scalar_prefetch=1, grid=(bd, n_pages // n_pp),
        in_specs=_page_specs(n_pp, cache.shape[1:]) + [per_b(qbd), per_b(dnew), const(tdn)] + [const(a) for a in lams],
        out_specs=pl.BlockSpec((1, nrows // 2, DIFF_DV), lambda b, i, pt: (b, 0, 0)),
        scratch_shapes=[pltpu.VMEM((nrows, 1), F32), pltpu.VMEM((nrows, 1), F32), pltpu.VMEM((nrows, DIFF_DV), F32)])
    return pl.pallas_call(
        functools.partial(_diff_samp_kernel, n_pp=n_pp, lam_init=lam_init),
        grid_spec=grid_spec,
        out_shape=jax.ShapeDtypeStruct((bd, nrows // 2, DIFF_DV), F32),
        compiler_params=_cparams(("parallel", "arbitrary")), name="diff_sample",
    )(page_table, *([cache] * n_pp), qbd, dnew, tdn, *lams)


def _route(lt):
    mx = jnp.max(lt, axis=0, keepdims=True)
    e = jnp.exp(lt - mx)
    p = e / jnp.sum(e, axis=0, keepdims=True)
    best = None
    grp = None
    for g in range(N_GROUPS):
        r = [p[EPG * g + k:EPG * g + k + 1] for k in range(EPG)]
        a, b = jnp.maximum(r[0], r[1]), jnp.minimum(r[0], r[1])
        c, d = jnp.maximum(r[2], r[3]), jnp.minimum(r[2], r[3])
        sc = jnp.maximum(a, c) + jnp.maximum(jnp.minimum(a, c), jnp.maximum(b, d))
        if g == 0:
            best, grp = sc, jnp.zeros(sc.shape, I32)
        else:
            better = sc > best
            grp = jnp.where(better, g, grp)
            best = jnp.where(better, sc, best)
    lg = []
    for k in range(EPG):
        v = lt[k:k + 1]
        for g in range(1, N_GROUPS):
            v = jnp.where(grp == g, lt[EPG * g + k:EPG * g + k + 1], v)
        lg.append(v)

    def first_max(vals):
        vmax = jnp.maximum(jnp.maximum(vals[0], vals[1]), jnp.maximum(vals[2], vals[3]))
        idx = jnp.where(vals[0] == vmax, 0, jnp.where(vals[1] == vmax, 1, jnp.where(vals[2] == vmax, 2, 3)))
        return vmax, idx

    v1, i1 = first_max(lg)
    rest = [jnp.where(i1 == k, -jnp.inf, lg[k]) for k in range(EPG)]
    v2, i2 = first_max(rest)
    e2 = jnp.exp(v2 - v1)
    den = 1.0 + e2
    return grp * EPG + i1, grp * EPG + i2, 1.0 / den, e2 / den


def _post_attn_kernel(o_ref, x_ref, w_ref, g_ref, b_ref, wr_ref, rb_ref, gain_ref, x1_ref, eidx_ref, gate_ref, *,
                      merge_scale):
    o = o_ref[...]
    if merge_scale is not None:
        segs = []
        for h in range(DIFF_HEADS):
            seg = o[:, h * DIFF_DV:(h + 1) * DIFF_DV]
            ms = jnp.mean(seg * seg, axis=-1, keepdims=True)
            segs.append(seg * lax.rsqrt(ms + LN_EPS) * gain_ref[...] * merge_scale)
        o = jnp.concatenate(segs, axis=1)
    mix = jnp.dot(o.astype(BF16), w_ref[...], preferred_element_type=F32)
    x1 = _layer_norm(DN_ALPHA * x_ref[...] + mix, g_ref[...], b_ref[...])
    x1_ref[...] = x1
    lt = lax.dot_general(wr_ref[...], x1, NT_DIMS, precision=HI, preferred_element_type=F32) + rb_ref[...]
    e1, e2, g1, g2 = _route(lt)
    eidx_ref[0:1, :] = e1
    eidx_ref[1:2, :] = e2
    gate_ref[0:1, :] = g1
    gate_ref[1:2, :] = g2


def _post_attention(o, x, w_out, ln_g, ln_b, wr_t, rb, gain, merge_scale):
    nt, d = x.shape
    tm = _token_tile(nt)
    row = lambda i: (i, 0)
    const = lambda a: pl.BlockSpec(a.shape, lambda i: (0, 0))
    return pl.pallas_call(
        functools.partial(_post_attn_kernel, merge_scale=merge_scale),
        grid=(nt // tm,),
        in_specs=[pl.BlockSpec((tm, d), row), pl.BlockSpec((tm, d), row), const(w_out), const(ln_g), const(ln_b),
                  const(wr_t), const(rb), const(gain)],
        out_specs=[pl.BlockSpec((tm, d), row), pl.BlockSpec((TOP_K, tm), lambda i: (0, i)),
                   pl.BlockSpec((TOP_K, tm), lambda i: (0, i))],
        out_shape=[jax.ShapeDtypeStruct((nt, d), F32), jax.ShapeDtypeStruct((TOP_K, nt), I32),
                   jax.ShapeDtypeStruct((TOP_K, nt), F32)],
        compiler_params=_cparams(("parallel",)), name="post_attn",
    )(o, x, w_out, ln_g, ln_b, wr_t, rb, gain)


def _start_rows(idx_ref, src_hbm, buf, slot, sem, n):
    def body(r0, c):
        for k in range(DMA_UNROLL):
            r = r0 * DMA_UNROLL + k
            pltpu.make_async_copy(src_hbm.at[pl.ds(idx_ref[0, r], 1), :], buf.at[slot, pl.ds(r, 1), :],
                                  sem.at[slot]).start(priority=k % 2)
        return c
    lax.fori_loop(0, n // DMA_UNROLL, body, 0)


def _wait_rows(src_hbm, buf, slot, sem, n):
    pltpu.make_async_copy(src_hbm.at[pl.ds(0, n), :], buf.at[slot], sem.at[slot]).wait()


def _moe_kernel(blk_e_ref, nv_ref, tok0_ref, tokn_ref, x_hbm, wgu_ref, wdn_ref, o_ref, xbuf, sem):
    i = pl.program_id(0)
    nv = nv_ref[0]
    tm = xbuf.shape[1]
    de = wdn_ref.shape[1]

    @pl.when(i == 0)
    def _():
        _start_rows(tok0_ref, x_hbm, xbuf, 0, sem, tm)

    @pl.when(i + 1 < nv)
    def _():
        _start_rows(tokn_ref, x_hbm, xbuf, (i + 1) % 2, sem, tm)

    @pl.when(i < nv)
    def _():
        slot = i % 2
        _wait_rows(x_hbm, xbuf, slot, sem, tm)
        xb = xbuf[slot].astype(BF16)
        acc = jnp.zeros(o_ref.shape, F32)
        half = de // 2
        for c in range(2):
            gate = jnp.dot(xb, wgu_ref[0, :, c * half:(c + 1) * half], preferred_element_type=F32)
            up = jnp.dot(xb, wgu_ref[0, :, de + c * half:de + (c + 1) * half], preferred_element_type=F32)
            act = (gate * jax.nn.sigmoid(gate) * up).astype(BF16)
            acc = acc + jnp.dot(act, wdn_ref[0, c * half:(c + 1) * half, :], preferred_element_type=F32)
        o_ref[...] = acc

    @pl.when(i >= nv)
    def _():
        o_ref[...] = jnp.zeros(o_ref.shape, F32)


def _moe_experts(x1, tok_blocks, blk_e, n_valid, w_gu, w_dn):
    nt, d = x1.shape
    n_blocks, _, tm = tok_blocks.shape
    grid_spec = pltpu.PrefetchScalarGridSpec(
        num_scalar_prefetch=2, grid=(n_blocks,),
        in_specs=[pl.BlockSpec((None, 1, tm), lambda i, be, nv: (i, 0, 0), memory_space=pltpu.SMEM),
                  pl.BlockSpec((None, 1, tm), lambda i, be, nv: (jnp.minimum(i + 1, n_blocks - 1), 0, 0),
                               memory_space=pltpu.SMEM),
                  pl.BlockSpec(memory_space=pl.ANY),
                  pl.BlockSpec((1,) + w_gu.shape[1:], lambda i, be, nv: (be[i], 0, 0)),
                  pl.BlockSpec((1,) + w_dn.shape[1:], lambda i, be, nv: (be[i], 0, 0))],
        out_specs=pl.BlockSpec((tm, d), lambda i, be, nv: (i, 0)),
        scratch_shapes=[pltpu.VMEM((2, tm, d), F32), pltpu.SemaphoreType.DMA((2,))])
    return pl.pallas_call(
        _moe_kernel, grid_spec=grid_spec,
        out_shape=jax.ShapeDtypeStruct((n_blocks * tm, d), F32),
        compiler_params=_cparams(("arbitrary",)), name="moe_experts",
    )(blk_e, n_valid, tok_blocks, tok_blocks, x1, w_gu, w_dn)


def _moe_combine_kernel(pos0_ref, posn_ref, y_hbm, x1_ref, gate_ref, g_ref, b_ref, o_ref, ybuf, sem):
    i = pl.program_id(0)
    n = pl.num_programs(0)
    tm = x1_ref.shape[0]

    @pl.when(i == 0)
    def _():
        _start_rows(pos0_ref, y_hbm, ybuf, 0, sem, TOP_K * tm)

    @pl.when(i + 1 < n)
    def _():
        _start_rows(posn_ref, y_hbm, ybuf, (i + 1) % 2, sem, TOP_K * tm)

    slot = i % 2
    _wait_rows(y_hbm, ybuf, slot, sem, TOP_K * tm)
    gt = gate_ref[...]
    f = gt[:, 0:1] * ybuf[slot, pl.ds(0, tm), :] + gt[:, 1:2] * ybuf[slot, pl.ds(tm, tm), :]
    o_ref[...] = _layer_norm(DN_ALPHA * x1_ref[...] + f, g_ref[...], b_ref[...])


def _moe_combine(y_sorted, pos_blocks, x1, gates_t, ln_g, ln_b):
    nt, d = x1.shape
    n_tiles, _, tm2 = pos_blocks.shape
    tm = tm2 // TOP_K
    const = lambda a: pl.BlockSpec(a.shape, lambda i: (0, 0))
    return pl.pallas_call(
        _moe_combine_kernel, grid=(n_tiles,),
        in_specs=[pl.BlockSpec((None, 1, tm2), lambda i: (i, 0, 0), memory_space=pltpu.SMEM),
                  pl.BlockSpec((None, 1, tm2), lambda i: (jnp.minimum(i + 1, n_tiles - 1), 0, 0),
                               memory_space=pltpu.SMEM),
                  pl.BlockSpec(memory_space=pl.ANY),
                  pl.BlockSpec((tm, d), lambda i: (i, 0)), pl.BlockSpec((tm, TOP_K), lambda i: (i, 0)),
                  const(ln_g), const(ln_b)],
        out_specs=pl.BlockSpec((tm, d), lambda i: (i, 0)),
        out_shape=jax.ShapeDtypeStruct((nt, d), F32),
        scratch_shapes=[pltpu.VMEM((2, tm2, d), F32), pltpu.SemaphoreType.DMA((2,))],
        compiler_params=_cparams(("arbitrary",)), name="moe_combine",
    )(pos_blocks, pos_blocks, y_sorted, x1, gates_t, ln_g, ln_b)


def _moe_layer(x1, eidx, gate, w_gu, w_dn, ln_g, ln_b):
    nt, _ = x1.shape
    a = nt * TOP_K
    flat_e = eidx.T.reshape(a)
    order = jnp.argsort(flat_e)
    e_sorted = flat_e[order]
    tok = (order // TOP_K).astype(I32)
    sizes = jnp.bincount(flat_e, length=N_EXPERTS)
    seg_start = jnp.cumsum(sizes) - sizes
    padded = (sizes + MOE_TM - 1) // MOE_TM * MOE_TM
    pad_end = jnp.cumsum(padded)
    pad_start = pad_end - padded
    dest = (pad_start[e_sorted] + jnp.arange(a) - seg_start[e_sorted]).astype(I32)
    n_blocks = -(-a // MOE_TM) + N_EXPERTS
    tok_pad = jnp.zeros((n_blocks * MOE_TM,), I32).at[dest].set(tok)
    blk_e = jnp.minimum(jnp.searchsorted(pad_end, jnp.arange(n_blocks) * MOE_TM, side="right"),
                        N_EXPERTS - 1).astype(I32)
    n_valid = (pad_end[-1:] // MOE_TM).astype(I32)
    y_sorted = _moe_experts(x1, tok_pad.reshape(n_blocks, 1, MOE_TM), blk_e, n_valid, w_gu, w_dn)
    pos = jnp.zeros((a,), I32).at[order].set(dest)
    tm = _token_tile(nt)
    pos_blocks = pos.reshape(nt // tm, tm, TOP_K).transpose(0, 2, 1).reshape(nt // tm, 1, TOP_K * tm)
    return _moe_combine(y_sorted, pos_blocks, x1, gate.T, ln_g, ln_b)


def _compress_weights(cmp_k, cmp_v):
    eye = jnp.eye(NSA_KV, dtype=F32)
    bd1, b1, bd2, b2 = [], [], [], []
    for (w1, bias1, w2, bias2) in (cmp_k, cmp_v):
        w1r = w1.reshape(CMP_SPAN, CMP_STRIDE, NSA_DH, NSA_DH)
        bd1.append(jnp.einsum("rsdh,ab->sadrbh", w1r, eye).reshape(CMP_STRIDE, NSA_KVW, CMP_SPAN * NSA_KVW))
        bd2.append(jnp.einsum("hd,ab->ahbd", w2, eye).reshape(NSA_KVW, NSA_KVW))
        b1.append(jnp.tile(bias1, NSA_KV)[None])
        b2.append(jnp.tile(bias2, NSA_KV)[None])
    r = np.arange(PAGE)
    perm = np.zeros((PAGE, PAGE), np.float32)
    perm[r, CMP_STRIDE * (r % 8) + r // 8] = 1.0
    return {"perm": jnp.asarray(perm, BF16), "bd1": jnp.stack(bd1).astype(BF16), "b1": jnp.stack(b1),
            "bd2": jnp.stack(bd2).astype(BF16), "b2": jnp.stack(b2)}


def _chunk_score_matrix(n_blocks_in, n_sb):
    lanes = -(-n_sb // 128) * 128
    n = np.arange(n_blocks_in)[:, None]
    b = np.arange(lanes)[None, :]
    m = ((n >= 4 * b) & (n <= 4 * b + 3)).astype(np.float32) + ((n + 1 >= 4 * b) & (n + 1 <= 4 * b + 3)).astype(np.float32)
    m = np.where(b < n_sb, m, 0.0)
    return jnp.asarray(m, BF16)


def kernel(x_prompt, x_sample, cache_cmp_kv, cache_slc_kv, state_win_kv, cache_diff_kv, page_table,
           nsa_w_in, nsa_w_out, cmp_k_w1, cmp_k_b1, cmp_k_w2, cmp_k_b2, cmp_v_w1, cmp_v_b1, cmp_v_w2, cmp_v_b2,
           diff_w_in, diff_w_out, lambda_q1, lambda_k1, lambda_q2, lambda_k2, diff_subln_gain, rel_bias,
           ln_gain, ln_bias, router_w, router_b, moe_w_gate_up, moe_w_down):
    b, s, d = x_prompt.shape
    bd, t_new, _ = x_sample.shape
    n_pages = page_table.shape[1]
    past = n_pages * PAGE
    assert s % KT == 0 and s >= WINDOW and past >= WINDOW and state_win_kv.shape[1] == WINDOW
    assert t_new <= CMP_STRIDE and (NSA_HEADS * t_new) % 8 == 0 and (DIFF_J * t_new) % 8 == 0
    n_p = b * s
    n_s = bd * t_new
    x_p = x_prompt.reshape(n_p, d)
    x_s = x_sample.reshape(n_s, d)

    hq = NSA_HEADS * NSA_DH
    wr_t = router_w.T
    rb = router_b[:, None]
    w_gu = moe_w_gate_up.astype(BF16)
    w_dn = moe_w_down.astype(BF16)
    t640 = _toeplitz_bias(rel_bias, QB, WINDOW + QB, WINDOW, 0, WINDOW, False)
    t_near = _toeplitz_bias(rel_bias, QB, 2 * QB, QB, 0, 1 << 30, True)
    n_pp = 16 if n_pages % 16 == 0 else 8
    assert n_pages % n_pp == 0

    w = nsa_w_in.astype(BF16)
    wg = jnp.pad(w[:, hq + 6 * NSA_KVW:], ((0, 0), (0, 128 - 3 * NSA_HEADS)))
    kv_w = [w[:, hq + 2 * k * NSA_KVW: hq + 2 * (k + 1) * NSA_KVW] for k in range(3)]
    f5 = (BF16, F32, F32, F32, F32)
    q_p, cmp_t, slc_t, win_t, gates_p = _project(x_p, [w[:, :hq]] + [a.T for a in kv_w] + [wg],
                                                 ("qscale", "transposed", "transposed", "transposed", "sigmoid"), f5, s)
    q_s, cmp_s, slc_s, win_s, gates_s = _project(x_s, [w[:, :hq]] + kv_w + [wg],
                                                 ("qscale", None, None, None, "sigmoid"), f5)
    cw = _compress_weights((cmp_k_w1, cmp_k_b1, cmp_k_w2, cmp_k_b2), (cmp_v_w1, cmp_v_b1, cmp_v_w2, cmp_v_b2))

    kvw = 2 * NSA_KVW
    ck, cv = _compress_prompt(cmp_t, cw)

    def per_group(a):
        return a.reshape(b, a.shape[1], NSA_KV, NSA_DH).transpose(0, 2, 1, 3)

    n_sb_p = s // SEL_BLOCK
    sp = s + 2 * PADF
    pos = np.arange(sp) - PADF
    inside = (pos >= 0) & (pos < s)
    blk_rows = -(-n_sb_p // 128) * 128
    onehot = (pos[None, :] // SEL_BLOCK == np.arange(blk_rows)[:, None]) & inside[None, :]
    flag_rows = np.zeros((AUG, sp), np.float32)
    flag_rows[0] = ~inside
    ones_rows = np.zeros((AUG, sp), np.float32)
    ones_rows[0] = 1.0

    def per_bg(a):
        return jnp.broadcast_to(jnp.asarray(a, BF16), (b, NSA_KV) + a.shape)

    def kv_t(rows_t):
        r = jnp.pad(rows_t.astype(BF16), ((0, 0), (0, 0), (PADF, PADF))).reshape(b, 2, NSA_KV, NSA_DH, sp)
        return r[:, 0], r[:, 1]

    skt, svt = kv_t(slc_t)
    wkt, wvt = kv_t(win_t)
    sk = jnp.concatenate([per_bg(onehot), skt, per_bg(flag_rows)], axis=2)
    sv = jnp.concatenate([svt, per_bg(ones_rows)], axis=2)
    wk = jnp.concatenate([wkt, per_bg(flag_rows)], axis=2)
    wv = jnp.concatenate([wvt, per_bg(ones_rows)], axis=2)
    q_flag = np.zeros((AUG,), np.float32)
    q_flag[0] = NEG
    q4 = q_p.reshape(b, s, NSA_HEADS, NSA_DH).transpose(0, 2, 1, 3)
    q4 = jnp.concatenate([q4, jnp.broadcast_to(jnp.asarray(q_flag, BF16), q4.shape[:3] + (AUG,))], axis=3)
    ck4 = jnp.pad(per_group(ck), ((0, 0), (0, 0), (0, 0), (0, AUG)))
    o4 = _nsa_prompt_attention(q4, gates_p.reshape(b, s, 128), ck4, per_group(cv), sk, sv, wk, wv,
                               t640, t_near, _chunk_score_matrix(s // CMP_STRIDE, n_sb_p), n_sb_p)
    o_p = o4.transpose(0, 2, 1, 3).reshape(n_p, hq)

    nrows = NSA_HEADS * t_new
    n_sb_s = -(-(past + t_new) // SEL_BLOCK)
    qs = q_s.reshape(bd, t_new, NSA_KV, NSA_J, NSA_DH).transpose(0, 3, 2, 1, 4)
    qbd = jnp.einsum("bjgtd,gh->bjgthd", qs, jnp.eye(NSA_KV, dtype=BF16)).reshape(bd, nrows, NSA_KVW)

    def native_t(a):
        return a.transpose(0, 2, 3, 4, 1).reshape(a.shape[0], kvw, a.shape[1])

    def new_rows_t(rows):
        return jnp.pad(rows.reshape(bd, t_new, kvw).transpose(0, 2, 1), ((0, 0), (0, 0), (0, PAGE - t_new)))

    xc = jnp.pad(cmp_s.reshape(bd, t_new, kvw), ((0, 0), (0, CMP_STRIDE - t_new), (0, 0)))
    ocmp, sel = _nsa_sample_a(page_table, native_t(cache_cmp_kv), xc, qbd, cw,
                              _chunk_score_matrix(n_pages * 8, n_sb_s), past, t_new, n_sb_s, n_pp)
    head_of_row = (np.arange(nrows) // t_new % NSA_KV) * NSA_J + np.arange(nrows) // (t_new * NSA_KV)
    tok_of_row = np.arange(nrows) % t_new
    tsn = _toeplitz_bias(rel_bias, t_new, 2 * PAGE, PAGE, 0, 1 << 30, True)[head_of_row, tok_of_row]
    tsw_full = _toeplitz_bias(rel_bias, t_new, WINDOW + PAGE, WINDOW, 0, WINDOW, False)
    tsw_full = jnp.where(jnp.arange(WINDOW + PAGE)[None, None, :] < WINDOW + t_new, tsw_full, NEG)
    tsw = tsw_full[head_of_row, tok_of_row]
    gs = gates_s[:, :3 * NSA_HEADS].reshape(bd, t_new, NSA_KV, NSA_J, 3).transpose(0, 3, 2, 1, 4).reshape(bd, nrows, 3)
    fold = jnp.asarray(np.tile(np.eye(NSA_DH, dtype=np.float32), (NSA_KV, 1)), BF16)
    o_s = _nsa_sample_b(page_table, native_t(cache_slc_kv), qbd, sel, new_rows_t(slc_s), native_t(state_win_kv),
                        new_rows_t(win_s), ocmp, gs, tsn, tsw, fold, n_pp)
    o_s = o_s.reshape(bd, NSA_J, NSA_KV, t_new, NSA_DH).transpose(0, 3, 2, 1, 4).reshape(n_s, hq)

    ones = jnp.ones((1, DIFF_DV), F32)
    x_all = jnp.concatenate([x_p, x_s], axis=0)
    x1, eidx, gate = _post_attention(jnp.concatenate([o_p, o_s], axis=0), x_all, nsa_w_out.astype(BF16),
                                     ln_gain[0, 0][None], ln_bias[0, 0][None], wr_t, rb, ones, None)
    x_all = _moe_layer(x1, eidx, gate, w_gu[0], w_dn[0], ln_gain[0, 1][None], ln_bias[0, 1][None])

    lam_init = 0.8 - 0.6 * math.exp(-0.3 * 1)
    lams = [a[None] for a in (lambda_q1, lambda_k1, lambda_q2, lambda_k2)]
    wd = diff_w_in.astype(BF16)
    dq = DIFF_HEADS * 2 * DIFF_DH
    dkw = DIFF_KV * DIFF_DV
    qd_p, diff_p, diff_t = _project(x_all[:n_p], [wd[:, :dq], wd[:, dq:], wd[:, dq:].T],
                                    ("qscale", None, "transposed"), (BF16, F32, BF16), s)
    qd_s, diff_s = _project(x_all[n_p:], [wd[:, :dq], wd[:, dq:]], ("qscale", None), (BF16, F32))
    tn_d = t_near.reshape(DIFF_KV, DIFF_J, 2, QB, 2 * QB).transpose(0, 2, 1, 3, 4)
    q6 = qd_p.reshape(b, s, DIFF_KV, DIFF_J, 2, DIFF_DH).transpose(0, 2, 4, 3, 1, 5)
    spd = s + PADF + KT_DIFF
    dt = jnp.pad(diff_t, ((0, 0), (0, 0), (PADF, KT_DIFF)))
    k5 = dt[:, :dkw].reshape(b, DIFF_KV, 2, DIFF_DH, spd)
    ones_d = np.zeros((AUG, spd), np.float32)
    ones_d[0] = 1.0
    ones_d = jnp.broadcast_to(jnp.asarray(ones_d, BF16), (b, DIFF_KV, AUG, spd))
    v4 = jnp.concatenate([dt[:, dkw:].reshape(b, DIFF_KV, DIFF_DV, spd), ones_d], axis=2)
    o_p = _diff_prompt_attention(q6, k5, v4, tn_d, lams, lam_init).reshape(n_p, DIFF_HEADS * DIFF_DV)

    nrows_d = 2 * DIFF_HEADS * t_new
    qsd = qd_s.reshape(bd, t_new, DIFF_KV, DIFF_J, 2, DIFF_DH).transpose(0, 2, 4, 3, 1, 5)
    qbd_d = jnp.einsum("bgmjtd,mM->bgmjtMd", qsd, jnp.eye(2, dtype=BF16)).reshape(bd, nrows_d, 2 * DIFF_DH)
    r = np.arange(nrows_d)
    col_of_row = (r // (2 * DIFF_J * t_new)) * 4 + (r // t_new % DIFF_J) * 2 + r // (DIFF_J * t_new) % 2
    tdn_full = _toeplitz_bias(rel_bias, t_new, 2 * PAGE, PAGE, 0, 1 << 30, True)
    tdn_full = jnp.where(jnp.arange(2 * PAGE)[None, None, :] < PAGE + t_new, tdn_full, NEG)
    tdn = tdn_full[col_of_row, r % t_new]
    dnew = jnp.pad(diff_s.reshape(bd, t_new, 2 * dkw), ((0, 0), (0, PAGE - t_new), (0, 0)))
    n_pp_d = 8
    cache_d = cache_diff_kv.reshape(-1, PAGE * 2 * DIFF_KV, DIFF_DV)
    o_s = _diff_sample(page_table, cache_d, qbd_d, dnew, tdn, lams, lam_init, n_pp_d)
    o_s = o_s.reshape(bd, DIFF_KV, DIFF_J, t_new, DIFF_DV).transpose(0, 3, 1, 2, 4).reshape(n_s, DIFF_HEADS * DIFF_DV)

    x1, eidx, gate = _post_attention(jnp.concatenate([o_p, o_s], axis=0), x_all, diff_w_out.astype(BF16),
                                     ln_gain[1, 0][None], ln_bias[1, 0][None], wr_t, rb, diff_subln_gain[None],
                                     1.0 - lam_init)
    x_all = _moe_layer(x1, eidx, gate, w_gu[1], w_dn[1], ln_gain[1, 1][None], ln_bias[1, 1][None])

    kv5 = (2, NSA_KV, NSA_DH)

    def rows_out(a_t):
        return a_t.reshape((b,) + kv5 + (a_t.shape[2],)).transpose(0, 4, 1, 2, 3)

    win_out_s = jnp.concatenate([state_win_kv, win_s.reshape((bd, t_new) + kv5)], axis=1)[:, t_new:]
    dshape = (2, DIFF_KV, DIFF_DV)
    return (x_all[:n_p].reshape(b, s, d), x_all[n_p:].reshape(bd, t_new, d),
            rows_out(cmp_t), cmp_s.reshape((bd, t_new) + kv5), rows_out(slc_t), slc_s.reshape((bd, t_new) + kv5),
            rows_out(win_t[:, :, s - WINDOW:]), win_out_s,
            diff_p.reshape((b, s) + dshape), diff_s.reshape((bd, t_new) + dshape))
```

```python
import functools
import math

import jax
import jax.numpy as jnp
import numpy as np
from jax import lax
from jax.experimental import pallas as pl
from jax.experimental.pallas import tpu as pltpu

F32 = jnp.float32
BF16 = jnp.bfloat16
I32 = jnp.int32
HI = lax.Precision.HIGHEST
NT_DIMS = (((1,), (1,)), ((), ()))

NSA_HEADS = 16
NSA_KV = 4
NSA_J = NSA_HEADS // NSA_KV
NSA_DH = 64
NSA_KVW = NSA_KV * NSA_DH
CMP_BLOCK = 32
CMP_STRIDE = 16
CMP_SPAN = CMP_BLOCK // CMP_STRIDE
SEL_BLOCK = 64
SEL_CHUNKS = SEL_BLOCK // CMP_STRIDE
SEL_TOPN = 16
WINDOW = 512
QB = 128
DIFF_HEADS = 8
DIFF_KV = 4
DIFF_J = DIFF_HEADS // DIFF_KV
DIFF_DH = 64
DIFF_DV = 2 * DIFF_DH
NUM_BUCKETS = 32
MAX_DISTANCE = 128
N_EXPERTS = 16
N_GROUPS = 4
EPG = N_EXPERTS // N_GROUPS
TOP_K = 2
DEPTH = 2
DN_ALPHA = (2 * DEPTH) ** 0.25
LN_EPS = 1e-5

PAGE = 128
KT = 512
KT_DIFF = 1024
PADF = 512
AUG = 16
NSA_QW = NSA_DH + AUG
DIFF_VW = DIFF_DV + AUG
MOE_TM = 512
DMA_UNROLL = 8
NEG = -1e30
BIG = 1e30
VMEM_LIMIT = 56 * 1024 * 1024


def _cparams(sem):
    return pltpu.CompilerParams(dimension_semantics=sem, vmem_limit_bytes=VMEM_LIMIT)


def _token_tile(nt):
    for tm in (512, 384, 256, 128):
        if nt % tm == 0:
            return tm
    raise ValueError(f"token count {nt} is not a multiple of 128")


def _bucket_np(dist):
    n = np.maximum(dist, 0)
    max_exact = NUM_BUCKETS // 2
    nf = np.maximum(n, 1).astype(np.float32)
    large = max_exact + (np.log(nf / np.float32(max_exact)) / np.float32(math.log(MAX_DISTANCE / max_exact))
                         * np.float32(NUM_BUCKETS - max_exact)).astype(np.int32)
    large = np.minimum(large, NUM_BUCKETS - 1)
    return np.where(n < max_exact, n, large).astype(np.int32)


def _toeplitz_bias(rel_bias, n_rows, n_cols, offset, lo, hi, shift_far):
    d = offset + (n_rows - 1) - np.arange(n_rows + n_cols - 1)
    u = rel_bias[_bucket_np(d)]
    if shift_far:
        u = u - rel_bias[NUM_BUCKETS - 1][None, :]
    u = jnp.where(jnp.asarray((d >= lo) & (d < hi))[:, None], u, NEG).T.astype(F32)
    return jnp.stack([u[:, n_rows - 1 - t:n_rows - 1 - t + n_cols] for t in range(n_rows)], axis=1)


def _gelu_tanh(x):
    return 0.5 * x * (1.0 + jnp.tanh(math.sqrt(2.0 / math.pi) * (x + 0.044715 * (x * x * x))))


def _layer_norm(y, g, b):
    mu = jnp.mean(y, axis=-1, keepdims=True)
    d = y - mu
    var = jnp.mean(d * d, axis=-1, keepdims=True)
    return d * lax.rsqrt(var + LN_EPS) * g + b


def _online_update(carry, s, pv):
    m, l, acc = carry
    m_new = jnp.maximum(m, jnp.max(s, axis=-1, keepdims=True))
    alpha = jnp.exp(m - m_new)
    p = jnp.exp(s - m_new)
    l = alpha * l + jnp.sum(p, axis=-1, keepdims=True)
    acc = alpha * acc + pv(p.astype(BF16))
    return m_new, l, acc


def _pv_t(vt):
    return lambda p: lax.dot_general(p, vt, NT_DIMS, preferred_element_type=F32)


def _flash_step(carry, s, vt_aug):
    m, acc = carry
    m_new = jnp.maximum(m, jnp.max(s, axis=-1, keepdims=True))
    alpha = jnp.exp(m - m_new)
    p = jnp.exp((s - m_new).astype(BF16))
    acc = alpha * acc + lax.dot_general(p, vt_aug, NT_DIMS, preferred_element_type=F32)
    return m_new, acc


def _softmax_pv_t(s, vt):
    m = jnp.max(s, axis=-1, keepdims=True)
    e = jnp.exp(s - m)
    l = jnp.sum(e, axis=-1, keepdims=True)
    return lax.dot_general(e.astype(BF16), vt, NT_DIMS, preferred_element_type=F32) / l


def _flash_init(rows, width):
    return jnp.full((rows, 1), NEG, F32), jnp.zeros((rows, width), F32)


def _flash_out(acc, dv):
    return acc[:, 0:dv] / acc[:, dv:dv + 1]


def _dot_exact_rhs(a, b):
    hi = a.astype(BF16)
    r1 = a - hi.astype(F32)
    mid = r1.astype(BF16)
    lo = (r1 - mid.astype(F32)).astype(BF16)
    return (jnp.dot(hi, b, preferred_element_type=F32) + jnp.dot(mid, b, preferred_element_type=F32)
            + jnp.dot(lo, b, preferred_element_type=F32))


def _masked_softmax(s, valid):
    sm = jnp.where(valid, s, NEG)
    m = jnp.max(sm, axis=-1, keepdims=True)
    e = jnp.where(valid, jnp.exp(sm - m), 0.0)
    z = jnp.sum(e, axis=-1, keepdims=True)
    return e / jnp.where(z > 0, z, 1.0)


def _topn_mask(score, n_sel, axis):
    n = score.shape[axis]
    pos = lax.broadcasted_iota(I32, score.shape, axis).astype(F32)

    def body(_, c):
        sc, sel = c
        m = jnp.max(sc, axis=axis, keepdims=True)
        first = jnp.min(jnp.where(sc == m, pos, float(n)), axis=axis, keepdims=True)
        one = pos == first
        sel = jnp.where(jnp.logical_and(one, m > -BIG), 1.0, sel)
        sc = jnp.where(one, -BIG, sc)
        return sc, sel

    _, sel = lax.fori_loop(0, n_sel, body, (score, jnp.zeros_like(score)))
    return sel


def _block_expand(sel_bf16, first_block, n_keys):
    nb = sel_bf16.shape[-1]
    kb = lax.broadcasted_iota(I32, (nb, n_keys), 1) // SEL_BLOCK + first_block
    e = (kb == lax.broadcasted_iota(I32, (nb, n_keys), 0)).astype(BF16)
    return jnp.dot(sel_bf16, e, preferred_element_type=F32)


def _select_blocks(score, qpos, n_sb, blocks_on_sublanes):
    nl = score.shape[-1]
    blk = lax.broadcasted_iota(I32, (1, nl), 1)
    cur = qpos // SEL_BLOCK
    validb = jnp.logical_and(blk * SEL_BLOCK <= qpos, blk < n_sb)
    forced = jnp.logical_or(blk == 0, jnp.logical_or(blk == cur, blk == cur - 1))
    forced = jnp.logical_and(forced, blk < n_sb)
    sc = jnp.where(forced, BIG, jnp.where(validb, score, -BIG))
    n_sel = min(SEL_TOPN, n_sb)
    if blocks_on_sublanes:
        return _topn_mask(sc.T, n_sel, 0).T
    return _topn_mask(sc, n_sel, 1)


def _page_parts(pages, perm, bd1_ref):
    permuted = [lax.dot_general(perm, p.astype(BF16), NT_DIMS, preferred_element_type=F32) for p in pages]
    outs = []
    for kv in range(2):
        for hp in range(NSA_KV // 2):
            c0 = kv * NSA_KVW + hp * 128
            lhs = jnp.concatenate(
                [jnp.concatenate([pp[s * 8:(s + 1) * 8, c0:c0 + 128] for s in range(CMP_STRIDE)], axis=1)
                 for pp in permuted], axis=0).astype(BF16)
            outs.append(jnp.dot(lhs, bd1_ref[kv], preferred_element_type=F32))
    return jnp.concatenate(outs, axis=1)


def _compress_finalize(p_ref, n, b1_ref, bd2_ref, b2_ref):
    outs = []
    for kv in range(2):
        c0 = kv * 2 * NSA_KVW
        p0 = jnp.concatenate([p_ref[pl.ds(0, n), c0 + 256 * hp:c0 + 256 * hp + 128] for hp in range(2)], axis=1)
        p1 = jnp.concatenate([p_ref[pl.ds(1, n), c0 + 256 * hp + 128:c0 + 256 * hp + 256] for hp in range(2)], axis=1)
        h = _gelu_tanh(p0 + p1 + b1_ref[kv])
        outs.append(jnp.dot(h.astype(BF16), bd2_ref[kv], preferred_element_type=F32) + b2_ref[kv])
    return outs


def _proj_kernel(x_ref, *refs, post):
    n = len(post)
    x = x_ref[...].astype(BF16)
    for w_ref, o_ref, p in zip(refs[:n], refs[n:], post):
        if p == "split128":
            for c in range(w_ref.shape[1] // 128):
                o_ref[:, c, :] = jnp.dot(x, w_ref[:, c * 128:(c + 1) * 128], preferred_element_type=F32)
            continue
        if p == "transposed":
            y = lax.dot_general(w_ref[...], x, NT_DIMS, preferred_element_type=F32)
        else:
            y = jnp.dot(x, w_ref[...], preferred_element_type=F32)
        if p == "qscale":
            y = y * 0.125
        elif p == "sigmoid":
            y = jax.nn.sigmoid(y)
        o_ref[...] = y.astype(o_ref.dtype)


def _project(x, weights, post, dtypes, seq=None):
    nt, d = x.shape
    tm = _token_tile(nt if seq is None else math.gcd(nt, seq))
    in_specs = [pl.BlockSpec((tm, d), lambda i: (i, 0))]
    in_specs += [pl.BlockSpec(w.shape, lambda i: (0, 0)) for w in weights]
    out_specs, out_shape = [], []
    for w, p, dt in zip(weights, post, dtypes):
        if p == "transposed":
            per = seq // tm
            out_specs.append(pl.BlockSpec((None, w.shape[0], tm), lambda i, per=per: (i // per, 0, i % per)))
            out_shape.append(jax.ShapeDtypeStruct((nt // seq, w.shape[0], seq), dt))
        elif p == "split128":
            out_specs.append(pl.BlockSpec((tm, w.shape[1] // 128, 128), lambda i: (i, 0, 0)))
            out_shape.append(jax.ShapeDtypeStruct((nt, w.shape[1] // 128, 128), dt))
        else:
            out_specs.append(pl.BlockSpec((tm, w.shape[1]), lambda i: (i, 0)))
            out_shape.append(jax.ShapeDtypeStruct((nt, w.shape[1]), dt))
    return pl.pallas_call(
        functools.partial(_proj_kernel, post=post),
        grid=(nt // tm,), in_specs=in_specs, out_specs=out_specs, out_shape=out_shape,
        compiler_params=_cparams(("parallel",)), name="proj",
    )(x, *weights)


def _cmp_prompt_kernel(rows_ref, perm_ref, bd1_ref, b1_ref, bd2_ref, b2_ref, ck_ref, cv_ref, p_ref, *, rt, n_chunks):
    i = pl.program_id(1)

    @pl.when(i == 0)
    def _():
        p_ref[pl.ds(n_chunks, 8), :] = jnp.zeros((8, p_ref.shape[1]), F32)

    pages = [rows_ref[0, :, k * PAGE:(k + 1) * PAGE] for k in range(rt // PAGE)]
    nrow = rt // CMP_STRIDE
    p_ref[pl.ds(pl.multiple_of(i * nrow, nrow), nrow), :] = _page_parts(pages, perm_ref[...], bd1_ref)

    @pl.when(i == pl.num_programs(1) - 1)
    def _():
        ck, cv = _compress_finalize(p_ref, n_chunks, b1_ref, bd2_ref, b2_ref)
        ck_ref[0] = ck.astype(ck_ref.dtype)
        cv_ref[0] = cv.astype(cv_ref.dtype)


def _compress_prompt(rows_t, cw):
    b, w, s = rows_t.shape
    rt = min(2048, s)
    n_chunks = s // CMP_STRIDE
    const = lambda a: pl.BlockSpec(a.shape, lambda bi, i, _n=a.ndim: (0,) * _n)
    return pl.pallas_call(
        functools.partial(_cmp_prompt_kernel, rt=rt, n_chunks=n_chunks),
        grid=(b, s // rt),
        in_specs=[pl.BlockSpec((1, w, rt), lambda bi, i: (bi, 0, i)),
                  const(cw["perm"]), const(cw["bd1"]), const(cw["b1"]), const(cw["bd2"]), const(cw["b2"])],
        out_specs=[pl.BlockSpec((1, n_chunks, NSA_KVW), lambda bi, i: (bi, 0, 0))] * 2,
        out_shape=[jax.ShapeDtypeStruct((b, n_chunks, NSA_KVW), BF16)] * 2,
        scratch_shapes=[pltpu.VMEM((n_chunks + 8, 4 * NSA_KVW), F32)],
        compiler_params=_cparams(("parallel", "arbitrary")), name="cmp_prompt",
    )(rows_t, cw["perm"], cw["bd1"], cw["b1"], cw["bd2"], cw["b2"])


def _nsa_prompt_kernel(q_ref, gates_ref, ck_ref, cv_ref, sk_ref, sv_ref, wk_ref, wv_ref, tw_ref, tn_ref, cm_ref,
                       o_ref, *, n_sb):
    g = pl.program_id(1)
    i = pl.program_id(2)
    start = i * QB
    rows = NSA_J * QB
    q2 = q_ref[0].reshape(rows, NSA_QW)
    qpos = start + lax.broadcasted_iota(I32, (QB, 1), 0)

    ck = ck_ref[0, 0]
    nbp = ck.shape[0]
    lc = lax.dot_general(q2, ck, NT_DIMS, preferred_element_type=F32).reshape(NSA_J, QB, nbp)
    c_end = lax.broadcasted_iota(I32, (1, nbp), 1) * CMP_STRIDE + (CMP_BLOCK - 1)
    pc = _masked_softmax(lc, (c_end <= qpos)[None])
    o_cmp = jnp.dot(pc.reshape(rows, nbp).astype(BF16), cv_ref[0, 0], preferred_element_type=F32)
    ps = pc[0] + pc[1] + pc[2] + pc[3]
    score = _dot_exact_rhs(ps, cm_ref[...])
    sel = _select_blocks(score, qpos, n_sb, True)

    blk = lax.broadcasted_iota(I32, (1, sel.shape[1]), 1)
    near0 = 2 * i - 2
    neg_all = jnp.where(sel > 0.5, 0.0, NEG)
    neg_far = jnp.where(blk < near0, neg_all, NEG)
    q_far = jnp.concatenate([jnp.concatenate([neg_far.astype(BF16)] * NSA_J, axis=0), q2], axis=1)
    q_near = jnp.concatenate([jnp.concatenate([neg_all.astype(BF16)] * NSA_J, axis=0), q2], axis=1)
    n_far = (jnp.maximum(i - 1, 0) * QB + KT - 1) // KT

    def far_body(kt, carry):
        off = pl.multiple_of(PADF + kt * KT, KT)
        s = jnp.dot(q_far, sk_ref[0, 0, :, pl.ds(off, KT)], preferred_element_type=F32)
        return _flash_step(carry, s, sv_ref[0, 0, :, pl.ds(off, KT)])

    carry = lax.fori_loop(0, n_far, far_body, _flash_init(rows, NSA_QW))

    offn = pl.multiple_of(PADF + start - QB, QB)
    s = jnp.dot(q_near, sk_ref[0, 0, :, pl.ds(offn, 2 * QB)], preferred_element_type=F32)
    s = (s.reshape(NSA_J, QB, 2 * QB) + tn_ref[...]).reshape(rows, 2 * QB)
    _, acc = _flash_step(carry, s, sv_ref[0, 0, :, pl.ds(offn, 2 * QB)])
    o_slc = _flash_out(acc, NSA_DH)

    lw = WINDOW + QB
    offw = pl.multiple_of(PADF + start - WINDOW, QB)
    s = jnp.dot(q2, wk_ref[0, 0, :, pl.ds(offw, lw)], preferred_element_type=F32)
    s = (s.reshape(NSA_J, QB, lw) + tw_ref[...]).reshape(rows, lw)
    _, acc = _flash_step(_flash_init(rows, NSA_QW), s, wv_ref[0, 0, :, pl.ds(offw, lw)])
    o_win = _flash_out(acc, NSA_DH)

    gt = gates_ref[0]
    colid = lax.broadcasted_iota(I32, gt.shape, 1)
    for j in range(NSA_J):
        out = jnp.zeros((QB, NSA_DH), F32)
        for c, ob in enumerate((o_cmp, o_slc, o_win)):
            gcol = jnp.sum(jnp.where(colid == 3 * (NSA_J * g + j) + c, gt, 0.0), axis=-1, keepdims=True)
            out = out + gcol * ob[j * QB:(j + 1) * QB]
        o_ref[0, j] = out.astype(o_ref.dtype)


def _nsa_prompt_attention(q4, gates, ck, cv, sk, sv, wk, wv, t_win, t_near, cmat, n_sb):
    b, h, s, qw = q4.shape
    nq = s // QB
    per_bg = lambda a: pl.BlockSpec((1, 1) + a.shape[2:], lambda bi, g, i: (bi, g, 0, 0))
    return pl.pallas_call(
        functools.partial(_nsa_prompt_kernel, n_sb=n_sb),
        grid=(b, NSA_KV, nq),
        in_specs=[pl.BlockSpec((1, NSA_J, QB, qw), lambda bi, g, i: (bi, g, i, 0)),
                  pl.BlockSpec((1, QB, gates.shape[2]), lambda bi, g, i: (bi, i, 0)),
                  per_bg(ck), per_bg(cv), per_bg(sk), per_bg(sv), per_bg(wk), per_bg(wv),
                  pl.BlockSpec((NSA_J, QB, t_win.shape[2]), lambda bi, g, i: (g, 0, 0)),
                  pl.BlockSpec((NSA_J, QB, t_near.shape[2]), lambda bi, g, i: (g, 0, 0)),
                  pl.BlockSpec(cmat.shape, lambda bi, g, i: (0, 0))],
        out_specs=pl.BlockSpec((1, NSA_J, QB, NSA_DH), lambda bi, g, i: (bi, g, i, 0)),
        out_shape=jax.ShapeDtypeStruct((b, h, s, NSA_DH), BF16),
        compiler_params=_cparams(("parallel", "parallel", "arbitrary")), name="nsa_prompt",
    )(q4, gates, ck, cv, sk, sv, wk, wv, t_win, t_near, cmat)


def _diff_lambda(lq1_ref, lk1_ref, lq2_ref, lk2_ref, lam_init):
    a = jnp.sum(lq1_ref[...] * lk1_ref[...], axis=-1, keepdims=True)
    b = jnp.sum(lq2_ref[...] * lk2_ref[...], axis=-1, keepdims=True)
    return jnp.exp(a) - jnp.exp(b) + lam_init


def _diff_prompt_kernel(q_ref, k_ref, v_ref, tn_ref, lq1_ref, lk1_ref, lq2_ref, lk2_ref, o_ref, *, lam_init):
    i = pl.program_id(2)
    start = i * QB
    rows = DIFF_J * QB
    far_end = jnp.maximum(start - QB, 0)
    n_far = (far_end + KT_DIFF - 1) // KT_DIFF
    q2 = [q_ref[0, 0, m].reshape(rows, DIFF_DH) for m in range(2)]

    def far_body(kt, carry):
        off = pl.multiple_of(PADF + kt * KT_DIFF, PADF)
        v = v_ref[0, 0, :, pl.ds(off, KT_DIFF)]
        keep = kt * KT_DIFF + lax.broadcasted_iota(I32, (1, KT_DIFF), 1) < far_end
        new = []
        for m in range(2):
            s = jnp.dot(q2[m], k_ref[0, 0, m, :, pl.ds(off, KT_DIFF)], preferred_element_type=F32)
            new.append(_flash_step(carry[m], jnp.where(keep, s, NEG), v))
        return tuple(new)

    init = _flash_init(rows, DIFF_VW)
    carry = lax.fori_loop(0, n_far, far_body, (init, init))
    offn = pl.multiple_of(PADF + start - QB, QB)
    v = v_ref[0, 0, :, pl.ds(offn, 2 * QB)]
    in_seq = start - QB + lax.broadcasted_iota(I32, (1, 2 * QB), 1) >= 0
    outs = []
    for m in range(2):
        s = jnp.dot(q2[m], k_ref[0, 0, m, :, pl.ds(offn, 2 * QB)], preferred_element_type=F32)
        s = (s.reshape(DIFF_J, QB, 2 * QB) + tn_ref[m]).reshape(rows, 2 * QB)
        _, acc = _flash_step(carry[m], jnp.where(in_seq, s, NEG), v)
        outs.append(_flash_out(acc, DIFF_DV))
    lam = _diff_lambda(lq1_ref, lk1_ref, lq2_ref, lk2_ref, lam_init)
    o = outs[0] - lam * outs[1]
    o_ref[0] = jnp.concatenate([o[j * QB:(j + 1) * QB] for j in range(DIFF_J)], axis=1)


def _diff_prompt_attention(q6, k5, v4, t_near, lams, lam_init):
    b, g, _, _, s, dh = q6.shape
    sp = k5.shape[4]
    nq = s // QB
    lam_spec = pl.BlockSpec((1, DIFF_DH), lambda bi, gi, i: (0, 0))
    return pl.pallas_call(
        functools.partial(_diff_prompt_kernel, lam_init=lam_init),
        grid=(b, g, nq),
        in_specs=[pl.BlockSpec((1, 1, 2, DIFF_J, QB, dh), lambda bi, gi, i: (bi, gi, 0, 0, i, 0)),
                  pl.BlockSpec((1, 1, 2, dh, sp), lambda bi, gi, i: (bi, gi, 0, 0, 0)),
                  pl.BlockSpec((1, 1, DIFF_VW, sp), lambda bi, gi, i: (bi, gi, 0, 0)),
                  pl.BlockSpec((None, 2, DIFF_J, QB, 2 * QB), lambda bi, gi, i: (gi, 0, 0, 0, 0)),
                  lam_spec, lam_spec, lam_spec, lam_spec],
        out_specs=pl.BlockSpec((1, QB, DIFF_J * DIFF_DV), lambda bi, gi, i: (bi, i, gi)),
        out_shape=jax.ShapeDtypeStruct((b, s, DIFF_HEADS * DIFF_DV), F32),
        compiler_params=_cparams(("parallel", "parallel", "arbitrary")), name="diff_prompt",
    )(q6, k5, v4, t_near, *lams)


def _nsa_samp_a_kernel(pt_ref, *refs, n_pp, n_pages, past, t_new, n_sb):
    page_refs = refs[:n_pp]
    (xc_ref, qbd_ref, perm_ref, bd1_ref, b1_ref, bd2_ref, b2_ref, cm_ref, ocmp_ref, sel_ref, p_ref) = refs[n_pp:]
    i = pl.program_id(1)
    nrow = n_pp * 8
    parts = _page_parts([r[...] for r in page_refs], perm_ref[...], bd1_ref)
    p_ref[pl.ds(pl.multiple_of(i * nrow, nrow), nrow), :] = parts

    @pl.when(i == pl.num_programs(1) - 1)
    def _():
        xc = xc_ref[0]
        accs = []
        for kv in range(2):
            for hp in range(NSA_KV // 2):
                c0 = kv * NSA_KVW + hp * 128
                lhs = jnp.concatenate([xc[s:s + 1, c0:c0 + 128] for s in range(CMP_STRIDE)], axis=1).astype(BF16)
                accs.append(jnp.dot(lhs, bd1_ref[kv], preferred_element_type=F32))
        n_cached = n_pages * 8
        p_ref[pl.ds(n_cached, 1), :] = jnp.concatenate(accs, axis=1)
        ck, cv = _compress_finalize(p_ref, n_cached, b1_ref, bd2_ref, b2_ref)
        qbd = qbd_ref[0]
        nrows = qbd.shape[0]
        lc = lax.dot_general(qbd, ck.astype(BF16), NT_DIMS, preferred_element_type=F32)
        qpos = past + lax.broadcasted_iota(I32, (nrows, 1), 0) % t_new
        c_end = lax.broadcasted_iota(I32, (1, n_cached), 1) * CMP_STRIDE + (CMP_BLOCK - 1)
        pc = _masked_softmax(lc, c_end <= qpos)
        ocmp_ref[0] = jnp.dot(pc.astype(BF16), cv.astype(BF16), preferred_element_type=F32)
        rg = nrows // NSA_J
        ps = pc[0:rg] + pc[rg:2 * rg] + pc[2 * rg:3 * rg] + pc[3 * rg:4 * rg]
        score = _dot_exact_rhs(ps, cm_ref[...])
        sel_ref[0] = _select_blocks(score, qpos[0:rg], n_sb, False)


def _page_specs(n_pp, page_shape):
    def spec(k):
        return pl.BlockSpec((None,) + page_shape, lambda b, i, pt, k=k: (pt[b, i * n_pp + k], 0, 0))
    return [spec(k) for k in range(n_pp)]


def _nsa_sample_a(page_table, cache, xc, qbd, cw, cmat, past, t_new, n_sb, n_pp):
    bd, n_pages = page_table.shape
    nrows = qbd.shape[1]
    rg = nrows // NSA_J
    nsbp = cmat.shape[1]
    const = lambda a: pl.BlockSpec(a.shape, lambda b, i, pt, _n=a.ndim: (0,) * _n)
    grid_spec = pltpu.PrefetchScalarGridSpec(
        num_scalar_prefetch=1, grid=(bd, n_pages // n_pp),
        in_specs=_page_specs(n_pp, cache.shape[1:]) + [
            pl.BlockSpec((1, CMP_STRIDE, xc.shape[2]), lambda b, i, pt: (b, 0, 0)),
            pl.BlockSpec((1, nrows, NSA_KVW), lambda b, i, pt: (b, 0, 0)),
            const(cw["perm"]), const(cw["bd1"]), const(cw["b1"]), const(cw["bd2"]), const(cw["b2"]), const(cmat)],
        out_specs=[pl.BlockSpec((1, nrows, NSA_KVW), lambda b, i, pt: (b, 0, 0)),
                   pl.BlockSpec((1, rg, nsbp), lambda b, i, pt: (b, 0, 0))],
        scratch_shapes=[pltpu.VMEM((n_pages * 8 + 8, 4 * NSA_KVW), F32)])
    return pl.pallas_call(
        functools.partial(_nsa_samp_a_kernel, n_pp=n_pp, n_pages=n_pages, past=past, t_new=t_new, n_sb=n_sb),
        grid_spec=grid_spec,
        out_shape=[jax.ShapeDtypeStruct((bd, nrows, NSA_KVW), F32), jax.ShapeDtypeStruct((bd, rg, nsbp), F32)],
        compiler_params=_cparams(("parallel", "arbitrary")), name="nsa_sample_a",
    )(page_table, *([cache] * n_pp), xc, qbd, cw["perm"], cw["bd1"], cw["b1"], cw["bd2"], cw["b2"], cmat)


def _nsa_samp_b_kernel(pt_ref, *refs, n_pp, n_pages):
    page_refs = refs[:n_pp]
    (qbd_ref, sel_ref, snew_ref, wst_ref, wnew_ref, ocmp_ref, gate_ref, tsn_ref, tsw_ref, fold_ref,
     o_ref, m_ref, l_ref, acc_ref) = refs[n_pp:]
    i = pl.program_id(1)

    @pl.when(i == 0)
    def _():
        m_ref[...] = jnp.full(m_ref.shape, NEG, F32)
        l_ref[...] = jnp.zeros(l_ref.shape, F32)
        acc_ref[...] = jnp.zeros(acc_ref.shape, F32)

    qbd = qbd_ref[0]
    sel = jnp.concatenate([sel_ref[0]] * NSA_J, axis=0)
    near0 = 2 * (n_pages - 1)
    blk = lax.broadcasted_iota(I32, (1, sel.shape[1]), 1)
    sel_far = jnp.where(blk < near0, sel, 0.0).astype(BF16)
    nk = n_pp * PAGE
    kcat = jnp.concatenate([r[0:NSA_KVW, :] for r in page_refs], axis=1).astype(BF16)
    vcat = jnp.concatenate([r[NSA_KVW:2 * NSA_KVW, :] for r in page_refs], axis=1).astype(BF16)
    s = jnp.dot(qbd, kcat, preferred_element_type=F32)
    keep = _block_expand(sel_far, i * (nk // SEL_BLOCK), nk) > 0.5
    carry = _online_update((m_ref[...], l_ref[...], acc_ref[...]), jnp.where(keep, s, NEG), _pv_t(vcat))
    m_ref[...], l_ref[...], acc_ref[...] = carry

    @pl.when(i == pl.num_programs(1) - 1)
    def _():
        last = page_refs[n_pp - 1]
        snew = snew_ref[0]
        kn = jnp.concatenate([last[0:NSA_KVW, :], snew[0:NSA_KVW, :]], axis=1).astype(BF16)
        vn = jnp.concatenate([last[NSA_KVW:2 * NSA_KVW, :], snew[NSA_KVW:2 * NSA_KVW, :]], axis=1).astype(BF16)
        s = jnp.dot(qbd, kn, preferred_element_type=F32) + tsn_ref[...]
        keep = _block_expand(sel.astype(BF16), near0, 2 * PAGE) > 0.5
        _, l, acc = _online_update((m_ref[...], l_ref[...], acc_ref[...]), jnp.where(keep, s, NEG), _pv_t(vn))
        o_slc = acc / l
        wst = wst_ref[0]
        wnew = wnew_ref[0]
        wk = jnp.concatenate([wst[0:NSA_KVW, :], wnew[0:NSA_KVW, :]], axis=1).astype(BF16)
        wv = jnp.concatenate([wst[NSA_KVW:2 * NSA_KVW, :], wnew[NSA_KVW:2 * NSA_KVW, :]], axis=1).astype(BF16)
        s = jnp.dot(qbd, wk, preferred_element_type=F32) + tsw_ref[...]
        o_win = _softmax_pv_t(s, wv)
        gt = gate_ref[0]
        o_full = gt[:, 0:1] * ocmp_ref[0] + gt[:, 1:2] * o_slc + gt[:, 2:3] * o_win
        nrows = o_full.shape[0]
        t_new = nrows // NSA_HEADS
        row_g = (lax.broadcasted_iota(I32, o_full.shape, 0) // t_new) % NSA_KV
        col_g = lax.broadcasted_iota(I32, o_full.shape, 1) // NSA_DH
        od = jnp.where(row_g == col_g, o_full, 0.0).astype(BF16)
        o_ref[0] = jnp.dot(od, fold_ref[...], preferred_element_type=F32).astype(o_ref.dtype)


def _nsa_sample_b(page_table, cache, qbd, sel, snew, wst, wnew, ocmp, gates, tsn, tsw, fold, n_pp):
    bd, n_pages = page_table.shape
    nrows = qbd.shape[1]
    per_b = lambda a: pl.BlockSpec((1,) + a.shape[1:], lambda b, i, pt, _n=a.ndim: (b,) + (0,) * (_n - 1))
    const = lambda a: pl.BlockSpec(a.shape, lambda b, i, pt, _n=a.ndim: (0,) * _n)
    grid_spec = pltpu.PrefetchScalarGridSpec(
        num_scalar_prefetch=1, grid=(bd, n_pages // n_pp),
        in_specs=_page_specs(n_pp, cache.shape[1:]) + [
            per_b(qbd), per_b(sel), per_b(snew), per_b(wst), per_b(wnew), per_b(ocmp), per_b(gates),
            const(tsn), const(tsw), const(fold)],
        out_specs=pl.BlockSpec((1, nrows, NSA_DH), lambda b, i, pt: (b, 0, 0)),
        scratch_shapes=[pltpu.VMEM((nrows, 1), F32), pltpu.VMEM((nrows, 1), F32), pltpu.VMEM((nrows, NSA_KVW), F32)])
    return pl.pallas_call(
        functools.partial(_nsa_samp_b_kernel, n_pp=n_pp, n_pages=n_pages),
        grid_spec=grid_spec,
        out_shape=jax.ShapeDtypeStruct((bd, nrows, NSA_DH), BF16),
        compiler_params=_cparams(("parallel", "arbitrary")), name="nsa_sample_b",
    )(page_table, *([cache] * n_pp), qbd, sel, snew, wst, wnew, ocmp, gates, tsn, tsw, fold)


def _diff_samp_kernel(pt_ref, *refs, n_pp, lam_init):
    page_refs = refs[:n_pp]
    (qbd_ref, dnew_ref, tdn_ref, lq1_ref, lk1_ref, lq2_ref, lk2_ref, o_ref, m_ref, l_ref, acc_ref) = refs[n_pp:]
    i = pl.program_id(1)
    last_step = i == pl.num_programs(1) - 1
    kw = DIFF_KV * 2 * DIFF_DH

    @pl.when(i == 0)
    def _():
        m_ref[...] = jnp.full(m_ref.shape, NEG, F32)
        l_ref[...] = jnp.zeros(l_ref.shape, F32)
        acc_ref[...] = jnp.zeros(acc_ref.shape, F32)

    qbd = qbd_ref[0]
    rg = qbd.shape[0] // DIFF_KV
    nk = n_pp * PAGE
    groups = range(DIFF_KV)

    def k_rows(ref, g):
        return ref[pl.ds(g, PAGE, stride=2 * DIFF_KV), :]

    def v_rows(ref, g):
        return ref[pl.ds(DIFF_KV + g, PAGE, stride=2 * DIFF_KV), :]

    def scores(k_of):
        return jnp.concatenate([lax.dot_general(qbd[g * rg:(g + 1) * rg], k_of(g), NT_DIMS,
                                                preferred_element_type=F32) for g in groups], axis=0)

    def pv(v_of):
        return lambda p: jnp.concatenate([jnp.dot(p[g * rg:(g + 1) * rg], v_of(g), preferred_element_type=F32)
                                          for g in groups], axis=0)

    s = scores(lambda g: jnp.concatenate([k_rows(r, g) for r in page_refs], axis=0).astype(BF16))
    limit = jnp.where(last_step, nk - PAGE, nk)
    s = jnp.where(lax.broadcasted_iota(I32, (1, nk), 1) < limit, s, NEG)
    carry = _online_update((m_ref[...], l_ref[...], acc_ref[...]), s,
                           pv(lambda g: jnp.concatenate([v_rows(r, g) for r in page_refs], axis=0).astype(BF16)))
    m_ref[...], l_ref[...], acc_ref[...] = carry

    @pl.when(last_step)
    def _():
        last = page_refs[n_pp - 1]
        dnew = dnew_ref[0]
        s = scores(lambda g: jnp.concatenate(
            [k_rows(last, g), dnew[:, g * DIFF_DV:(g + 1) * DIFF_DV]], axis=0).astype(BF16)) + tdn_ref[...]
        _, l, acc = _online_update((m_ref[...], l_ref[...], acc_ref[...]), s, pv(lambda g: jnp.concatenate(
            [v_rows(last, g), dnew[:, kw + g * DIFF_DV:kw + (g + 1) * DIFF_DV]], axis=0).astype(BF16)))
        o_full = acc / l
        lam = _diff_lambda(lq1_ref, lk1_ref, lq2_ref, lk2_ref, lam_init)
        h = rg // 2
        o_ref[0] = jnp.concatenate([o_full[g * rg:g * rg + h] - lam * o_full[g * rg + h:(g + 1) * rg]
                                    for g in groups], axis=0)


def _diff_sample(page_table, cache, qbd, dnew, tdn, lams, lam_init, n_pp):
    bd, n_pages = page_table.shape
    nrows = qbd.shape[1]
    kw = DIFF_KV * 2 * DIFF_DH
    per_b = lambda a: pl.BlockSpec((1,) + a.shape[1:], lambda b, i, pt, _n=a.ndim: (b,) + (0,) * (_n - 1))
    const = lambda a: pl.BlockSpec(a.shape, lambda b, i, pt, _n=a.ndim: (0,) * _n)
    grid_spec = pltpu.PrefetchScalarGridSpec(
        num_scalar_prefetch=1, grid=(bd, n_pages // n_pp),
        in_specs=_page_specs(n_pp, cache.shape[1:]) + [per_b(qbd), per_b(dnew), const(tdn)] + [const(a) for a in lams],
        out_specs=pl.BlockSpec((1, nrows // 2, DIFF_DV), lambda b, i, pt: (b, 0, 0)),
        scratch_shapes=[pltpu.VMEM((nrows, 1), F32), pltpu.VMEM((nrows, 1), F32), pltpu.VMEM((nrows, DIFF_DV), F32)])
    return pl.pallas_call(
        functools.partial(_diff_samp_kernel, n_pp=n_pp, lam_init=lam_init),
        grid_spec=grid_spec,
        out_shape=jax.ShapeDtypeStruct((bd, nrows // 2, DIFF_DV), F32),
        compiler_params=_cparams(("parallel", "arbitrary")), name="diff_sample",
    )(page_table, *([cache] * n_pp), qbd, dnew, tdn, *lams)


def _route(lt):
    mx = jnp.max(lt, axis=0, keepdims=True)
    e = jnp.exp(lt - mx)
    p = e / jnp.sum(e, axis=0, keepdims=True)
    best = None
    grp = None
    for g in range(N_GROUPS):
        r = [p[EPG * g + k:EPG * g + k + 1] for k in range(EPG)]
        a, b = jnp.maximum(r[0], r[1]), jnp.minimum(r[0], r[1])
        c, d = jnp.maximum(r[2], r[3]), jnp.minimum(r[2], r[3])
        sc = jnp.maximum(a, c) + jnp.maximum(jnp.minimum(a, c), jnp.maximum(b, d))
        if g == 0:
            best, grp = sc, jnp.zeros(sc.shape, I32)
        else:
            better = sc > best
            grp = jnp.where(better, g, grp)
            best = jnp.where(better, sc, best)
    lg = []
    for k in range(EPG):
        v = lt[k:k + 1]
        for g in range(1, N_GROUPS):
            v = jnp.where(grp == g, lt[EPG * g + k:EPG * g + k + 1], v)
        lg.append(v)

    def first_max(vals):
        vmax = jnp.maximum(jnp.maximum(vals[0], vals[1]), jnp.maximum(vals[2], vals[3]))
        idx = jnp.where(vals[0] == vmax, 0, jnp.where(vals[1] == vmax, 1, jnp.where(vals[2] == vmax, 2, 3)))
        return vmax, idx

    v1, i1 = first_max(lg)
    rest = [jnp.where(i1 == k, -jnp.inf, lg[k]) for k in range(EPG)]
    v2, i2 = first_max(rest)
    e2 = jnp.exp(v2 - v1)
    den = 1.0 + e2
    return grp * EPG + i1, grp * EPG + i2, 1.0 / den, e2 / den


def _post_attn_kernel(o_ref, x_ref, w_ref, g_ref, b_ref, wr_ref, rb_ref, gain_ref, x1_ref, eidx_ref, gate_ref, *,
                      merge_scale):
    o = o_ref[...]
    if merge_scale is not None:
        segs = []
        for h in range(DIFF_HEADS):
            seg = o[:, h * DIFF_DV:(h + 1) * DIFF_DV]
            ms = jnp.mean(seg * seg, axis=-1, keepdims=True)
            segs.append(seg * lax.rsqrt(ms + LN_EPS) * gain_ref[...] * merge_scale)
        o = jnp.concatenate(segs, axis=1)
    mix = jnp.dot(o.astype(BF16), w_ref[...], preferred_element_type=F32)
    x1 = _layer_norm(DN_ALPHA * x_ref[...] + mix, g_ref[...], b_ref[...])
    x1_ref[...] = x1
    lt = lax.dot_general(wr_ref[...], x1, NT_DIMS, precision=HI, preferred_element_type=F32) + rb_ref[...]
    e1, e2, g1, g2 = _route(lt)
    eidx_ref[0:1, :] = e1
    eidx_ref[1:2, :] = e2
    gate_ref[0:1, :] = g1
    gate_ref[1:2, :] = g2


def _post_attention(o, x, w_out, ln_g, ln_b, wr_t, rb, gain, merge_scale):
    nt, d = x.shape
    tm = _token_tile(nt)
    row = lambda i: (i, 0)
    const = lambda a: pl.BlockSpec(a.shape, lambda i: (0, 0))
    return pl.pallas_call(
        functools.partial(_post_attn_kernel, merge_scale=merge_scale),
        grid=(nt // tm,),
        in_specs=[pl.BlockSpec((tm, d), row), pl.BlockSpec((tm, d), row), const(w_out), const(ln_g), const(ln_b),
                  const(wr_t), const(rb), const(gain)],
        out_specs=[pl.BlockSpec((tm, d), row), pl.BlockSpec((TOP_K, tm), lambda i: (0, i)),
                   pl.BlockSpec((TOP_K, tm), lambda i: (0, i))],
        out_shape=[jax.ShapeDtypeStruct((nt, d), F32), jax.ShapeDtypeStruct((TOP_K, nt), I32),
                   jax.ShapeDtypeStruct((TOP_K, nt), F32)],
        compiler_params=_cparams(("parallel",)), name="post_attn",
    )(o, x, w_out, ln_g, ln_b, wr_t, rb, gain)


def _start_rows(idx_ref, src_hbm, buf, slot, sem, n):
    def body(r0, c):
        for k in range(DMA_UNROLL):
            r = r0 * DMA_UNROLL + k
            pltpu.make_async_copy(src_hbm.at[pl.ds(idx_ref[0, r], 1), :], buf.at[slot, pl.ds(r, 1), :],
                                  sem.at[slot]).start(priority=k % 2)
        return c
    lax.fori_loop(0, n // DMA_UNROLL, body, 0)


def _wait_rows(src_hbm, buf, slot, sem, n):
    pltpu.make_async_copy(src_hbm.at[pl.ds(0, n), :], buf.at[slot], sem.at[slot]).wait()


def _moe_kernel(blk_e_ref, nv_ref, tok0_ref, tokn_ref, x_hbm, wgu_ref, wdn_ref, o_ref, xbuf, sem):
    i = pl.program_id(0)
    nv = nv_ref[0]
    tm = xbuf.shape[1]
    de = wdn_ref.shape[1]

    @pl.when(i == 0)
    def _():
        _start_rows(tok0_ref, x_hbm, xbuf, 0, sem, tm)

    @pl.when(i + 1 < nv)
    def _():
        _start_rows(tokn_ref, x_hbm, xbuf, (i + 1) % 2, sem, tm)

    @pl.when(i < nv)
    def _():
        slot = i % 2
        _wait_rows(x_hbm, xbuf, slot, sem, tm)
        xb = xbuf[slot].astype(BF16)
        acc = jnp.zeros(o_ref.shape, F32)
        half = de // 2
        for c in range(2):
            gate = jnp.dot(xb, wgu_ref[0, :, c * half:(c + 1) * half], preferred_element_type=F32)
            up = jnp.dot(xb, wgu_ref[0, :, de + c * half:de + (c + 1) * half], preferred_element_type=F32)
            act = (gate * jax.nn.sigmoid(gate) * up).astype(BF16)
            acc = acc + jnp.dot(act, wdn_ref[0, c * half:(c + 1) * half, :], preferred_element_type=F32)
        o_ref[...] = acc

    @pl.when(i >= nv)
    def _():
        o_ref[...] = jnp.zeros(o_ref.shape, F32)


def _moe_experts(x1, tok_blocks, blk_e, n_valid, w_gu, w_dn):
    nt, d = x1.shape
    n_blocks, _, tm = tok_blocks.shape
    grid_spec = pltpu.PrefetchScalarGridSpec(
        num_scalar_prefetch=2, grid=(n_blocks,),
        in_specs=[pl.BlockSpec((None, 1, tm), lambda i, be, nv: (i, 0, 0), memory_space=pltpu.SMEM),
                  pl.BlockSpec((None, 1, tm), lambda i, be, nv: (jnp.minimum(i + 1, n_blocks - 1), 0, 0),
                               memory_space=pltpu.SMEM),
                  pl.BlockSpec(memory_space=pl.ANY),
                  pl.BlockSpec((1,) + w_gu.shape[1:], lambda i, be, nv: (be[i], 0, 0)),
                  pl.BlockSpec((1,) + w_dn.shape[1:], lambda i, be, nv: (be[i], 0, 0))],
        out_specs=pl.BlockSpec((tm, d), lambda i, be, nv: (i, 0)),
        scratch_shapes=[pltpu.VMEM((2, tm, d), F32), pltpu.SemaphoreType.DMA((2,))])
    return pl.pallas_call(
        _moe_kernel, grid_spec=grid_spec,
        out_shape=jax.ShapeDtypeStruct((n_blocks * tm, d), F32),
        compiler_params=_cparams(("arbitrary",)), name="moe_experts",
    )(blk_e, n_valid, tok_blocks, tok_blocks, x1, w_gu, w_dn)


def _moe_combine_kernel(pos0_ref, posn_ref, y_hbm, x1_ref, gate_ref, g_ref, b_ref, o_ref, ybuf, sem):
    i = pl.program_id(0)
    n = pl.num_programs(0)
    tm = x1_ref.shape[0]

    @pl.when(i == 0)
    def _():
        _start_rows(pos0_ref, y_hbm, ybuf, 0, sem, TOP_K * tm)

    @pl.when(i + 1 < n)
    def _():
        _start_rows(posn_ref, y_hbm, ybuf, (i + 1) % 2, sem, TOP_K * tm)

    slot = i % 2
    _wait_rows(y_hbm, ybuf, slot, sem, TOP_K * tm)
    gt = gate_ref[...]
    f = gt[:, 0:1] * ybuf[slot, pl.ds(0, tm), :] + gt[:, 1:2] * ybuf[slot, pl.ds(tm, tm), :]
    o_ref[...] = _layer_norm(DN_ALPHA * x1_ref[...] + f, g_ref[...], b_ref[...])


def _moe_combine(y_sorted, pos_blocks, x1, gates_t, ln_g, ln_b):
    nt, d = x1.shape
    n_tiles, _, tm2 = pos_blocks.shape
    tm = tm2 // TOP_K
    const = lambda a: pl.BlockSpec(a.shape, lambda i: (0, 0))
    return pl.pallas_call(
        _moe_combine_kernel, grid=(n_tiles,),
        in_specs=[pl.BlockSpec((None, 1, tm2), lambda i: (i, 0, 0), memory_space=pltpu.SMEM),
                  pl.BlockSpec((None, 1, tm2), lambda i: (jnp.minimum(i + 1, n_tiles - 1), 0, 0),
                               memory_space=pltpu.SMEM),
                  pl.BlockSpec(memory_space=pl.ANY),
                  pl.BlockSpec((tm, d), lambda i: (i, 0)), pl.BlockSpec((tm, TOP_K), lambda i: (i, 0)),
                  const(ln_g), const(ln_b)],
        out_specs=pl.BlockSpec((tm, d), lambda i: (i, 0)),
        out_shape=jax.ShapeDtypeStruct((nt, d), F32),
        scratch_shapes=[pltpu.VMEM((2, tm2, d), F32), pltpu.SemaphoreType.DMA((2,))],
        compiler_params=_cparams(("arbitrary",)), name="moe_combine",
    )(pos_blocks, pos_blocks, y_sorted, x1, gates_t, ln_g, ln_b)


def _moe_layer(x1, eidx, gate, w_gu, w_dn, ln_g, ln_b):
    nt, _ = x1.shape
    a = nt * TOP_K
    flat_e = eidx.T.reshape(a)
    onehot = (flat_e[:, None] == jnp.arange(N_EXPERTS, dtype=I32)[None, :]).astype(I32)
    running = jnp.cumsum(onehot, axis=0)
    sizes = running[-1]
    rank = jnp.sum(onehot * running, axis=1) - 1
    padded = (sizes + MOE_TM - 1) // MOE_TM * MOE_TM
    pad_end = jnp.cumsum(padded)
    pad_start = pad_end - padded
    pos = (jnp.sum(onehot * pad_start[None, :], axis=1) + rank).astype(I32)
    n_blocks = -(-a // MOE_TM) + N_EXPERTS
    tok_pad = jnp.zeros((n_blocks * MOE_TM,), I32).at[pos].set(jnp.arange(a, dtype=I32) // TOP_K)
    blk_start = jnp.arange(n_blocks, dtype=I32)[:, None] * MOE_TM
    blk_e = jnp.minimum(jnp.sum((pad_end[None, :] <= blk_start).astype(I32), axis=1), N_EXPERTS - 1).astype(I32)
    n_valid = (pad_end[-1:] // MOE_TM).astype(I32)
    y_sorted = _moe_experts(x1, tok_pad.reshape(n_blocks, 1, MOE_TM), blk_e, n_valid, w_gu, w_dn)
    tm = _token_tile(nt)
    pos_blocks = pos.reshape(nt // tm, tm, TOP_K).transpose(0, 2, 1).reshape(nt // tm, 1, TOP_K * tm)
    return _moe_combine(y_sorted, pos_blocks, x1, gate.T, ln_g, ln_b)


def _compress_weights(cmp_k, cmp_v):
    eye = jnp.eye(NSA_KV, dtype=F32)
    eye2 = jnp.eye(2, dtype=F32)
    bd1, b1, bd2, b2 = [], [], [], []
    for (w1, bias1, w2, bias2) in (cmp_k, cmp_v):
        w1r = w1.reshape(CMP_SPAN, CMP_STRIDE, NSA_DH, NSA_DH)
        bd1.append(jnp.einsum("rsdh,ab->sadrbh", w1r, eye2).reshape(CMP_STRIDE * 2 * NSA_DH, CMP_SPAN * 2 * NSA_DH))
        bd2.append(jnp.einsum("hd,ab->ahbd", w2, eye).reshape(NSA_KVW, NSA_KVW))
        b1.append(jnp.tile(bias1, NSA_KV)[None])
        b2.append(jnp.tile(bias2, NSA_KV)[None])
    r = np.arange(PAGE)
    perm = np.zeros((PAGE, PAGE), np.float32)
    perm[r, CMP_STRIDE * (r % 8) + r // 8] = 1.0
    return {"perm": jnp.asarray(perm, BF16), "bd1": jnp.stack(bd1).astype(BF16), "b1": jnp.stack(b1),
            "bd2": jnp.stack(bd2).astype(BF16), "b2": jnp.stack(b2)}


def _chunk_score_matrix(n_blocks_in, n_sb):
    lanes = -(-n_sb // 128) * 128
    n = np.arange(n_blocks_in)[:, None]
    b = np.arange(lanes)[None, :]
    m = ((n >= 4 * b) & (n <= 4 * b + 3)).astype(np.float32) + ((n + 1 >= 4 * b) & (n + 1 <= 4 * b + 3)).astype(np.float32)
    m = np.where(b < n_sb, m, 0.0)
    return jnp.asarray(m, BF16)


def kernel(x_prompt, x_sample, cache_cmp_kv, cache_slc_kv, state_win_kv, cache_diff_kv, page_table,
           nsa_w_in, nsa_w_out, cmp_k_w1, cmp_k_b1, cmp_k_w2, cmp_k_b2, cmp_v_w1, cmp_v_b1, cmp_v_w2, cmp_v_b2,
           diff_w_in, diff_w_out, lambda_q1, lambda_k1, lambda_q2, lambda_k2, diff_subln_gain, rel_bias,
           ln_gain, ln_bias, router_w, router_b, moe_w_gate_up, moe_w_down):
    b, s, d = x_prompt.shape
    bd, t_new, _ = x_sample.shape
    n_pages = page_table.shape[1]
    past = n_pages * PAGE
    assert s % KT == 0 and s >= WINDOW and past >= WINDOW and state_win_kv.shape[1] == WINDOW
    assert t_new <= CMP_STRIDE and (NSA_HEADS * t_new) % 8 == 0 and (DIFF_J * t_new) % 8 == 0
    n_p = b * s
    n_s = bd * t_new
    x_p = x_prompt.reshape(n_p, d)
    x_s = x_sample.reshape(n_s, d)

    hq = NSA_HEADS * NSA_DH
    wr_t = router_w.T
    rb = router_b[:, None]
    w_gu = moe_w_gate_up.astype(BF16)
    w_dn = moe_w_down.astype(BF16)
    t640 = _toeplitz_bias(rel_bias, QB, WINDOW + QB, WINDOW, 0, WINDOW, False)
    t_near = _toeplitz_bias(rel_bias, QB, 2 * QB, QB, 0, 1 << 30, True)
    n_pp = 16 if n_pages % 16 == 0 else 8
    assert n_pages % n_pp == 0

    w = nsa_w_in.astype(BF16)
    wg = jnp.pad(w[:, hq + 6 * NSA_KVW:], ((0, 0), (0, 128 - 3 * NSA_HEADS)))
    kv_w = [w[:, hq + 2 * k * NSA_KVW: hq + 2 * (k + 1) * NSA_KVW] for k in range(3)]
    f5 = (BF16, F32, F32, F32, F32)
    q_p, cmp_t, slc_t, win_t, gates_p = _project(x_p, [w[:, :hq]] + [a.T for a in kv_w] + [wg],
                                                 ("qscale", "transposed", "transposed", "transposed", "sigmoid"), f5, s)
    q_s, cmp_s, slc_s, win_s, gates_s = _project(x_s, [w[:, :hq]] + kv_w + [wg],
                                                 ("qscale", None, None, None, "sigmoid"), f5)
    cw = _compress_weights((cmp_k_w1, cmp_k_b1, cmp_k_w2, cmp_k_b2), (cmp_v_w1, cmp_v_b1, cmp_v_w2, cmp_v_b2))

    kvw = 2 * NSA_KVW
    ck, cv = _compress_prompt(cmp_t, cw)

    def per_group(a):
        return a.reshape(b, a.shape[1], NSA_KV, NSA_DH).transpose(0, 2, 1, 3)

    n_sb_p = s // SEL_BLOCK
    sp = s + 2 * PADF
    pos = np.arange(sp) - PADF
    inside = (pos >= 0) & (pos < s)
    blk_rows = -(-n_sb_p // 128) * 128
    onehot = (pos[None, :] // SEL_BLOCK == np.arange(blk_rows)[:, None]) & inside[None, :]
    flag_rows = np.zeros((AUG, sp), np.float32)
    flag_rows[0] = ~inside
    ones_rows = np.zeros((AUG, sp), np.float32)
    ones_rows[0] = 1.0

    def per_bg(a):
        return jnp.broadcast_to(jnp.asarray(a, BF16), (b, NSA_KV) + a.shape)

    def kv_t(rows_t):
        r = jnp.pad(rows_t.astype(BF16), ((0, 0), (0, 0), (PADF, PADF))).reshape(b, 2, NSA_KV, NSA_DH, sp)
        return r[:, 0], r[:, 1]

    skt, svt = kv_t(slc_t)
    wkt, wvt = kv_t(win_t)
    sk = jnp.concatenate([per_bg(onehot), skt, per_bg(flag_rows)], axis=2)
    sv = jnp.concatenate([svt, per_bg(ones_rows)], axis=2)
    wk = jnp.concatenate([wkt, per_bg(flag_rows)], axis=2)
    wv = jnp.concatenate([wvt, per_bg(ones_rows)], axis=2)
    q_flag = np.zeros((AUG,), np.float32)
    q_flag[0] = NEG
    q4 = q_p.reshape(b, s, NSA_HEADS, NSA_DH).transpose(0, 2, 1, 3)
    q4 = jnp.concatenate([q4, jnp.broadcast_to(jnp.asarray(q_flag, BF16), q4.shape[:3] + (AUG,))], axis=3)
    ck4 = jnp.pad(per_group(ck), ((0, 0), (0, 0), (0, 0), (0, AUG)))
    o4 = _nsa_prompt_attention(q4, gates_p.reshape(b, s, 128), ck4, per_group(cv), sk, sv, wk, wv,
                               t640, t_near, _chunk_score_matrix(s // CMP_STRIDE, n_sb_p), n_sb_p)
    o_p = o4.transpose(0, 2, 1, 3).reshape(n_p, hq)

    nrows = NSA_HEADS * t_new
    n_sb_s = -(-(past + t_new) // SEL_BLOCK)
    qs = q_s.reshape(bd, t_new, NSA_KV, NSA_J, NSA_DH).transpose(0, 3, 2, 1, 4)
    qbd = jnp.einsum("bjgtd,gh->bjgthd", qs, jnp.eye(NSA_KV, dtype=BF16)).reshape(bd, nrows, NSA_KVW)

    def native_t(a):
        return a.transpose(0, 2, 3, 4, 1).reshape(a.shape[0], kvw, a.shape[1])

    def new_rows_t(rows):
        return jnp.pad(rows.reshape(bd, t_new, kvw).transpose(0, 2, 1), ((0, 0), (0, 0), (0, PAGE - t_new)))

    xc = jnp.pad(cmp_s.reshape(bd, t_new, kvw), ((0, 0), (0, CMP_STRIDE - t_new), (0, 0)))
    ocmp, sel = _nsa_sample_a(page_table, native_t(cache_cmp_kv), xc, qbd, cw,
                              _chunk_score_matrix(n_pages * 8, n_sb_s), past, t_new, n_sb_s, n_pp)
    head_of_row = (np.arange(nrows) // t_new % NSA_KV) * NSA_J + np.arange(nrows) // (t_new * NSA_KV)
    tok_of_row = np.arange(nrows) % t_new
    tsn = _toeplitz_bias(rel_bias, t_new, 2 * PAGE, PAGE, 0, 1 << 30, True)[head_of_row, tok_of_row]
    tsw_full = _toeplitz_bias(rel_bias, t_new, WINDOW + PAGE, WINDOW, 0, WINDOW, False)
    tsw_full = jnp.where(jnp.arange(WINDOW + PAGE)[None, None, :] < WINDOW + t_new, tsw_full, NEG)
    tsw = tsw_full[head_of_row, tok_of_row]
    gs = gates_s[:, :3 * NSA_HEADS].reshape(bd, t_new, NSA_KV, NSA_J, 3).transpose(0, 3, 2, 1, 4).reshape(bd, nrows, 3)
    fold = jnp.asarray(np.tile(np.eye(NSA_DH, dtype=np.float32), (NSA_KV, 1)), BF16)
    o_s = _nsa_sample_b(page_table, native_t(cache_slc_kv), qbd, sel, new_rows_t(slc_s), native_t(state_win_kv),
                        new_rows_t(win_s), ocmp, gs, tsn, tsw, fold, n_pp)
    o_s = o_s.reshape(bd, NSA_J, NSA_KV, t_new, NSA_DH).transpose(0, 3, 2, 1, 4).reshape(n_s, hq)

    ones = jnp.ones((1, DIFF_DV), F32)
    x_all = jnp.concatenate([x_p, x_s], axis=0)
    x1, eidx, gate = _post_attention(jnp.concatenate([o_p, o_s], axis=0), x_all, nsa_w_out.astype(BF16),
                                     ln_gain[0, 0][None], ln_bias[0, 0][None], wr_t, rb, ones, None)
    x_all = _moe_layer(x1, eidx, gate, w_gu[0], w_dn[0], ln_gain[0, 1][None], ln_bias[0, 1][None])

    lam_init = 0.8 - 0.6 * math.exp(-0.3 * 1)
    lams = [a[None] for a in (lambda_q1, lambda_k1, lambda_q2, lambda_k2)]
    wd = diff_w_in.astype(BF16)
    dq = DIFF_HEADS * 2 * DIFF_DH
    dkw = DIFF_KV * DIFF_DV
    qd_p, diff_p, diff_t = _project(x_all[:n_p], [wd[:, :dq], wd[:, dq:], wd[:, dq:].T],
                                    ("qscale", "split128", "transposed"), (BF16, F32, BF16), s)
    qd_s, diff_s = _project(x_all[n_p:], [wd[:, :dq], wd[:, dq:]], ("qscale", "split128"), (BF16, F32))
    tn_d = t_near.reshape(DIFF_KV, DIFF_J, 2, QB, 2 * QB).transpose(0, 2, 1, 3, 4)
    q6 = qd_p.reshape(b, s, DIFF_KV, DIFF_J, 2, DIFF_DH).transpose(0, 2, 4, 3, 1, 5)
    spd = s + PADF + KT_DIFF
    dt = jnp.pad(diff_t, ((0, 0), (0, 0), (PADF, KT_DIFF)))
    k5 = dt[:, :dkw].reshape(b, DIFF_KV, 2, DIFF_DH, spd)
    ones_d = np.zeros((AUG, spd), np.float32)
    ones_d[0] = 1.0
    ones_d = jnp.broadcast_to(jnp.asarray(ones_d, BF16), (b, DIFF_KV, AUG, spd))
    v4 = jnp.concatenate([dt[:, dkw:].reshape(b, DIFF_KV, DIFF_DV, spd), ones_d], axis=2)
    o_p = _diff_prompt_attention(q6, k5, v4, tn_d, lams, lam_init).reshape(n_p, DIFF_HEADS * DIFF_DV)

    nrows_d = 2 * DIFF_HEADS * t_new
    qsd = qd_s.reshape(bd, t_new, DIFF_KV, DIFF_J, 2, DIFF_DH).transpose(0, 2, 4, 3, 1, 5)
    qbd_d = jnp.einsum("bgmjtd,mM->bgmjtMd", qsd, jnp.eye(2, dtype=BF16)).reshape(bd, nrows_d, 2 * DIFF_DH)
    r = np.arange(nrows_d)
    col_of_row = (r // (2 * DIFF_J * t_new)) * 4 + (r // t_new % DIFF_J) * 2 + r // (DIFF_J * t_new) % 2
    tdn_full = _toeplitz_bias(rel_bias, t_new, 2 * PAGE, PAGE, 0, 1 << 30, True)
    tdn_full = jnp.where(jnp.arange(2 * PAGE)[None, None, :] < PAGE + t_new, tdn_full, NEG)
    tdn = tdn_full[col_of_row, r % t_new]
    dnew = jnp.pad(diff_s.reshape(bd, t_new, 2 * dkw), ((0, 0), (0, PAGE - t_new), (0, 0)))
    n_pp_d = 8
    cache_d = cache_diff_kv.reshape(-1, PAGE * 2 * DIFF_KV, DIFF_DV)
    o_s = _diff_sample(page_table, cache_d, qbd_d, dnew, tdn, lams, lam_init, n_pp_d)
    o_s = o_s.reshape(bd, DIFF_KV, DIFF_J, t_new, DIFF_DV).transpose(0, 3, 1, 2, 4).reshape(n_s, DIFF_HEADS * DIFF_DV)

    x1, eidx, gate = _post_attention(jnp.concatenate([o_p, o_s], axis=0), x_all, diff_w_out.astype(BF16),
                                     ln_gain[1, 0][None], ln_bias[1, 0][None], wr_t, rb, diff_subln_gain[None],
                                     1.0 - lam_init)
    x_all = _moe_layer(x1, eidx, gate, w_gu[1], w_dn[1], ln_gain[1, 1][None], ln_bias[1, 1][None])

    kv5 = (2, NSA_KV, NSA_DH)

    def rows_out(a_t):
        return a_t.reshape((b,) + kv5 + (a_t.shape[2],)).transpose(0, 4, 1, 2, 3)

    win_out_s = jnp.concatenate([state_win_kv, win_s.reshape((bd, t_new) + kv5)], axis=1)[:, t_new:]
    dshape = (2, DIFF_KV, DIFF_DV)
    return (x_all[:n_p].reshape(b, s, d), x_all[n_p:].reshape(bd, t_new, d),
            rows_out(cmp_t), cmp_s.reshape((bd, t_new) + kv5), rows_out(slc_t), slc_s.reshape((bd, t_new) + kv5),
            rows_out(win_t[:, :, s - WINDOW:]), win_out_s,
            diff_p.reshape((b, s) + dshape), diff_s.reshape((bd, t_new) + dshape))
```

```python
import functools
import math

import jax
import jax.numpy as jnp
import numpy as np
from jax import lax
from jax.experimental import pallas as pl
from jax.experimental.pallas import tpu as pltpu

F32 = jnp.float32
BF16 = jnp.bfloat16
I32 = jnp.int32
HI = lax.Precision.HIGHEST
NT_DIMS = (((1,), (1,)), ((), ()))

NSA_HEADS = 16
NSA_KV = 4
NSA_J = NSA_HEADS // NSA_KV
NSA_DH = 64
NSA_KVW = NSA_KV * NSA_DH
CMP_BLOCK = 32
CMP_STRIDE = 16
CMP_SPAN = CMP_BLOCK // CMP_STRIDE
SEL_BLOCK = 64
SEL_CHUNKS = SEL_BLOCK // CMP_STRIDE
SEL_TOPN = 16
WINDOW = 512
QB = 128
DIFF_HEADS = 8
DIFF_KV = 4
DIFF_J = DIFF_HEADS // DIFF_KV
DIFF_DH = 64
DIFF_DV = 2 * DIFF_DH
NUM_BUCKETS = 32
MAX_DISTANCE = 128
N_EXPERTS = 16
N_GROUPS = 4
EPG = N_EXPERTS // N_GROUPS
TOP_K = 2
DEPTH = 2
DN_ALPHA = (2 * DEPTH) ** 0.25
LN_EPS = 1e-5

PAGE = 128
KT = 512
KT_DIFF = 1024
PADF = 512
AUG = 16
NSA_QW = NSA_DH + AUG
DIFF_VW = DIFF_DV + AUG
MOE_TM = 512
DMA_UNROLL = 8
NEG = -1e30
BIG = 1e30
VMEM_LIMIT = 56 * 1024 * 1024


def _cparams(sem):
    return pltpu.CompilerParams(dimension_semantics=sem, vmem_limit_bytes=VMEM_LIMIT)


def _token_tile(nt):
    for tm in (512, 384, 256, 128):
        if nt % tm == 0:
            return tm
    raise ValueError(f"token count {nt} is not a multiple of 128")


def _bucket_np(dist):
    n = np.maximum(dist, 0)
    max_exact = NUM_BUCKETS // 2
    nf = np.maximum(n, 1).astype(np.float32)
    large = max_exact + (np.log(nf / np.float32(max_exact)) / np.float32(math.log(MAX_DISTANCE / max_exact))
                         * np.float32(NUM_BUCKETS - max_exact)).astype(np.int32)
    large = np.minimum(large, NUM_BUCKETS - 1)
    return np.where(n < max_exact, n, large).astype(np.int32)


def _toeplitz_bias(rel_bias, n_rows, n_cols, offset, lo, hi, shift_far):
    d = offset + (n_rows - 1) - np.arange(n_rows + n_cols - 1)
    u = rel_bias[_bucket_np(d)]
    if shift_far:
        u = u - rel_bias[NUM_BUCKETS - 1][None, :]
    u = jnp.where(jnp.asarray((d >= lo) & (d < hi))[:, None], u, NEG).T.astype(F32)
    return jnp.stack([u[:, n_rows - 1 - t:n_rows - 1 - t + n_cols] for t in range(n_rows)], axis=1)


def _gelu_tanh(x):
    return 0.5 * x * (1.0 + jnp.tanh(math.sqrt(2.0 / math.pi) * (x + 0.044715 * (x * x * x))))


def _layer_norm(y, g, b):
    mu = jnp.mean(y, axis=-1, keepdims=True)
    d = y - mu
    var = jnp.mean(d * d, axis=-1, keepdims=True)
    return d * lax.rsqrt(var + LN_EPS) * g + b


def _online_update(carry, s, pv):
    m, l, acc = carry
    m_new = jnp.maximum(m, jnp.max(s, axis=-1, keepdims=True))
    alpha = jnp.exp(m - m_new)
    p = jnp.exp(s - m_new)
    l = alpha * l + jnp.sum(p, axis=-1, keepdims=True)
    acc = alpha * acc + pv(p.astype(BF16))
    return m_new, l, acc


def _pv_t(vt):
    return lambda p: lax.dot_general(p, vt, NT_DIMS, preferred_element_type=F32)


def _flash_step(carry, s, vt_aug):
    m, acc = carry
    m_new = jnp.maximum(m, jnp.max(s, axis=-1, keepdims=True))
    alpha = jnp.exp(m - m_new)
    p = jnp.exp((s - m_new).astype(BF16))
    acc = alpha * acc + lax.dot_general(p, vt_aug, NT_DIMS, preferred_element_type=F32)
    return m_new, acc


def _softmax_pv_t(s, vt):
    m = jnp.max(s, axis=-1, keepdims=True)
    e = jnp.exp(s - m)
    l = jnp.sum(e, axis=-1, keepdims=True)
    return lax.dot_general(e.astype(BF16), vt, NT_DIMS, preferred_element_type=F32) / l


def _flash_init(rows, width):
    return jnp.full((rows, 1), NEG, F32), jnp.zeros((rows, width), F32)


def _flash_out(acc, dv):
    return acc[:, 0:dv] / acc[:, dv:dv + 1]


def _dot_exact_rhs(a, b):
    hi = a.astype(BF16)
    r1 = a - hi.astype(F32)
    mid = r1.astype(BF16)
    lo = (r1 - mid.astype(F32)).astype(BF16)
    return (jnp.dot(hi, b, preferred_element_type=F32) + jnp.dot(mid, b, preferred_element_type=F32)
            + jnp.dot(lo, b, preferred_element_type=F32))


def _masked_softmax(s, valid):
    sm = jnp.where(valid, s, NEG)
    m = jnp.max(sm, axis=-1, keepdims=True)
    e = jnp.where(valid, jnp.exp(sm - m), 0.0)
    z = jnp.sum(e, axis=-1, keepdims=True)
    return e / jnp.where(z > 0, z, 1.0)


def _topn_mask(score, n_sel, axis):
    n = score.shape[axis]
    pos = lax.broadcasted_iota(I32, score.shape, axis).astype(F32)

    def body(_, c):
        sc, sel = c
        m = jnp.max(sc, axis=axis, keepdims=True)
        first = jnp.min(jnp.where(sc == m, pos, float(n)), axis=axis, keepdims=True)
        one = pos == first
        sel = jnp.where(jnp.logical_and(one, m > -BIG), 1.0, sel)
        sc = jnp.where(one, -BIG, sc)
        return sc, sel

    _, sel = lax.fori_loop(0, n_sel, body, (score, jnp.zeros_like(score)))
    return sel


def _block_expand(sel_bf16, first_block, n_keys):
    nb = sel_bf16.shape[-1]
    kb = lax.broadcasted_iota(I32, (nb, n_keys), 1) // SEL_BLOCK + first_block
    e = (kb == lax.broadcasted_iota(I32, (nb, n_keys), 0)).astype(BF16)
    return jnp.dot(sel_bf16, e, preferred_element_type=F32)


def _select_blocks(score, qpos, n_sb, blocks_on_sublanes):
    nl = score.shape[-1]
    blk = lax.broadcasted_iota(I32, (1, nl), 1)
    cur = qpos // SEL_BLOCK
    validb = jnp.logical_and(blk * SEL_BLOCK <= qpos, blk < n_sb)
    forced = jnp.logical_or(blk == 0, jnp.logical_or(blk == cur, blk == cur - 1))
    forced = jnp.logical_and(forced, blk < n_sb)
    sc = jnp.where(forced, BIG, jnp.where(validb, score, -BIG))
    n_sel = min(SEL_TOPN, n_sb)
    if blocks_on_sublanes:
        return _topn_mask(sc.T, n_sel, 0).T
    return _topn_mask(sc, n_sel, 1)


def _page_parts(pages, perm, bd1_ref):
    permuted = [lax.dot_general(perm, p.astype(BF16), NT_DIMS, preferred_element_type=F32) for p in pages]
    outs = []
    for kv in range(2):
        for hp in range(NSA_KV // 2):
            c0 = kv * NSA_KVW + hp * 128
            lhs = jnp.concatenate(
                [jnp.concatenate([pp[s * 8:(s + 1) * 8, c0:c0 + 128] for s in range(CMP_STRIDE)], axis=1)
                 for pp in permuted], axis=0).astype(BF16)
            outs.append(jnp.dot(lhs, bd1_ref[kv], preferred_element_type=F32))
    return jnp.concatenate(outs, axis=1)


def _compress_finalize(p_ref, n, b1_ref, bd2_ref, b2_ref):
    outs = []
    for kv in range(2):
        c0 = kv * 2 * NSA_KVW
        p0 = jnp.concatenate([p_ref[pl.ds(0, n), c0 + 256 * hp:c0 + 256 * hp + 128] for hp in range(2)], axis=1)
        p1 = jnp.concatenate([p_ref[pl.ds(1, n), c0 + 256 * hp + 128:c0 + 256 * hp + 256] for hp in range(2)], axis=1)
        h = _gelu_tanh(p0 + p1 + b1_ref[kv])
        outs.append(jnp.dot(h.astype(BF16), bd2_ref[kv], preferred_element_type=F32) + b2_ref[kv])
    return outs


def _proj_kernel(x_ref, *refs, post):
    n = len(post)
    x = x_ref[...].astype(BF16)
    for w_ref, o_ref, p in zip(refs[:n], refs[n:], post):
        if p == "split128":
            for c in range(w_ref.shape[1] // 128):
                o_ref[:, c, :] = jnp.dot(x, w_ref[:, c * 128:(c + 1) * 128], preferred_element_type=F32)
            continue
        if p == "transposed":
            y = lax.dot_general(w_ref[...], x, NT_DIMS, preferred_element_type=F32)
        else:
            y = jnp.dot(x, w_ref[...], preferred_element_type=F32)
        if p in ("qscale", "q_heads", "q_heads_flag"):
            y = y * 0.125
        elif p == "sigmoid":
            y = jax.nn.sigmoid(y)
        if p in ("q_heads", "q_heads_flag"):
            lane = lax.broadcasted_iota(I32, (y.shape[0], AUG), 1)
            flag = jnp.where(lane == 0, NEG, 0.0).astype(o_ref.dtype)
            for c in range(y.shape[1] // 64):
                piece = y[:, c * 64:(c + 1) * 64].astype(o_ref.dtype)
                o_ref[c] = jnp.concatenate([piece, flag], axis=1) if p == "q_heads_flag" else piece
            continue
        o_ref[...] = y.astype(o_ref.dtype)


def _project(x, weights, post, dtypes, seq=None):
    nt, d = x.shape
    tm = _token_tile(nt if seq is None else math.gcd(nt, seq))
    in_specs = [pl.BlockSpec((tm, d), lambda i: (i, 0))]
    in_specs += [pl.BlockSpec(w.shape, lambda i: (0, 0)) for w in weights]
    out_specs, out_shape = [], []
    for w, p, dt in zip(weights, post, dtypes):
        if p == "transposed":
            per = seq // tm
            out_specs.append(pl.BlockSpec((None, w.shape[0], tm), lambda i, per=per: (i // per, 0, i % per)))
            out_shape.append(jax.ShapeDtypeStruct((nt // seq, w.shape[0], seq), dt))
        elif p == "split128":
            out_specs.append(pl.BlockSpec((tm, w.shape[1] // 128, 128), lambda i: (i, 0, 0)))
            out_shape.append(jax.ShapeDtypeStruct((nt, w.shape[1] // 128, 128), dt))
        elif p in ("q_heads", "q_heads_flag"):
            per = seq // tm
            nh, wd = w.shape[1] // 64, 64 + (AUG if p == "q_heads_flag" else 0)
            out_specs.append(pl.BlockSpec((None, nh, tm, wd), lambda i, per=per: (i // per, 0, i % per, 0)))
            out_shape.append(jax.ShapeDtypeStruct((nt // seq, nh, seq, wd), dt))
        else:
            out_specs.append(pl.BlockSpec((tm, w.shape[1]), lambda i: (i, 0)))
            out_shape.append(jax.ShapeDtypeStruct((nt, w.shape[1]), dt))
    return pl.pallas_call(
        functools.partial(_proj_kernel, post=post),
        grid=(nt // tm,), in_specs=in_specs, out_specs=out_specs, out_shape=out_shape,
        compiler_params=_cparams(("parallel",)), name="proj",
    )(x, *weights)


def _cmp_prompt_kernel(rows_ref, perm_ref, bd1_ref, b1_ref, bd2_ref, b2_ref, ck_ref, cv_ref, p_ref, *, rt, n_chunks):
    i = pl.program_id(1)

    @pl.when(i == 0)
    def _():
        p_ref[pl.ds(n_chunks, 8), :] = jnp.zeros((8, p_ref.shape[1]), F32)

    pages = [rows_ref[0, :, k * PAGE:(k + 1) * PAGE] for k in range(rt // PAGE)]
    nrow = rt // CMP_STRIDE
    p_ref[pl.ds(pl.multiple_of(i * nrow, nrow), nrow), :] = _page_parts(pages, perm_ref[...], bd1_ref)

    @pl.when(i == pl.num_programs(1) - 1)
    def _():
        ck, cv = _compress_finalize(p_ref, n_chunks, b1_ref, bd2_ref, b2_ref)
        ck_ref[0] = ck.astype(ck_ref.dtype)
        cv_ref[0] = cv.astype(cv_ref.dtype)


def _compress_prompt(rows_t, cw):
    b, w, s = rows_t.shape
    rt = min(2048, s)
    n_chunks = s // CMP_STRIDE
    const = lambda a: pl.BlockSpec(a.shape, lambda bi, i, _n=a.ndim: (0,) * _n)
    return pl.pallas_call(
        functools.partial(_cmp_prompt_kernel, rt=rt, n_chunks=n_chunks),
        grid=(b, s // rt),
        in_specs=[pl.BlockSpec((1, w, rt), lambda bi, i: (bi, 0, i)),
                  const(cw["perm"]), const(cw["bd1"]), const(cw["b1"]), const(cw["bd2"]), const(cw["b2"])],
        out_specs=[pl.BlockSpec((1, n_chunks, NSA_KVW), lambda bi, i: (bi, 0, 0))] * 2,
        out_shape=[jax.ShapeDtypeStruct((b, n_chunks, NSA_KVW), BF16)] * 2,
        scratch_shapes=[pltpu.VMEM((n_chunks + 8, 4 * NSA_KVW), F32)],
        compiler_params=_cparams(("parallel", "arbitrary")), name="cmp_prompt",
    )(rows_t, cw["perm"], cw["bd1"], cw["b1"], cw["bd2"], cw["b2"])


def _nsa_prompt_kernel(q_ref, gates_ref, ck_ref, cv_ref, sk_ref, sv_ref, wk_ref, wv_ref, tw_ref, tn_ref, cm_ref,
                       o_ref, *, n_sb):
    g = pl.program_id(1)
    i = pl.program_id(2)
    start = i * QB
    rows = NSA_J * QB
    q2 = q_ref[0].reshape(rows, NSA_QW)
    qpos = start + lax.broadcasted_iota(I32, (QB, 1), 0)

    ck = ck_ref[0, 0]
    nbp = ck.shape[0]
    lc = lax.dot_general(q2, ck, NT_DIMS, preferred_element_type=F32).reshape(NSA_J, QB, nbp)
    c_end = lax.broadcasted_iota(I32, (1, nbp), 1) * CMP_STRIDE + (CMP_BLOCK - 1)
    pc = _masked_softmax(lc, (c_end <= qpos)[None])
    o_cmp = jnp.dot(pc.reshape(rows, nbp).astype(BF16), cv_ref[0, 0], preferred_element_type=F32)
    ps = pc[0] + pc[1] + pc[2] + pc[3]
    score = _dot_exact_rhs(ps, cm_ref[...])
    sel = _select_blocks(score, qpos, n_sb, True)

    blk = lax.broadcasted_iota(I32, (1, sel.shape[1]), 1)
    near0 = 2 * i - 2
    neg_all = jnp.where(sel > 0.5, 0.0, NEG)
    neg_far = jnp.where(blk < near0, neg_all, NEG)
    q_far = jnp.concatenate([jnp.concatenate([neg_far.astype(BF16)] * NSA_J, axis=0), q2], axis=1)
    q_near = jnp.concatenate([jnp.concatenate([neg_all.astype(BF16)] * NSA_J, axis=0), q2], axis=1)
    n_far = (jnp.maximum(i - 1, 0) * QB + KT - 1) // KT

    def far_body(kt, carry):
        off = pl.multiple_of(PADF + kt * KT, KT)
        s = jnp.dot(q_far, sk_ref[0, 0, :, pl.ds(off, KT)], preferred_element_type=F32)
        return _flash_step(carry, s, sv_ref[0, 0, :, pl.ds(off, KT)])

    carry = lax.fori_loop(0, n_far, far_body, _flash_init(rows, NSA_QW))

    offn = pl.multiple_of(PADF + start - QB, QB)
    s = jnp.dot(q_near, sk_ref[0, 0, :, pl.ds(offn, 2 * QB)], preferred_element_type=F32)
    s = (s.reshape(NSA_J, QB, 2 * QB) + tn_ref[...]).reshape(rows, 2 * QB)
    _, acc = _flash_step(carry, s, sv_ref[0, 0, :, pl.ds(offn, 2 * QB)])
    o_slc = _flash_out(acc, NSA_DH)

    lw = WINDOW + QB
    offw = pl.multiple_of(PADF + start - WINDOW, QB)
    s = jnp.dot(q2, wk_ref[0, 0, :, pl.ds(offw, lw)], preferred_element_type=F32)
    s = (s.reshape(NSA_J, QB, lw) + tw_ref[...]).reshape(rows, lw)
    _, acc = _flash_step(_flash_init(rows, NSA_QW), s, wv_ref[0, 0, :, pl.ds(offw, lw)])
    o_win = _flash_out(acc, NSA_DH)

    gt = gates_ref[0]
    colid = lax.broadcasted_iota(I32, gt.shape, 1)
    heads = []
    for j in range(NSA_J):
        out = jnp.zeros((QB, NSA_DH), F32)
        for c, ob in enumerate((o_cmp, o_slc, o_win)):
            gcol = jnp.sum(jnp.where(colid == 3 * (NSA_J * g + j) + c, gt, 0.0), axis=-1, keepdims=True)
            out = out + gcol * ob[j * QB:(j + 1) * QB]
        heads.append(out.astype(o_ref.dtype))
    o_ref[0] = jnp.concatenate(heads, axis=1)


def _nsa_prompt_attention(q4, gates, ck, cv, sk, sv, wk, wv, t_win, t_near, cmat, n_sb):
    b, h, s, qw = q4.shape
    nq = s // QB
    per_bg = lambda a: pl.BlockSpec((1, 1) + a.shape[2:], lambda bi, g, i: (bi, g, 0, 0))
    return pl.pallas_call(
        functools.partial(_nsa_prompt_kernel, n_sb=n_sb),
        grid=(b, NSA_KV, nq),
        in_specs=[pl.BlockSpec((1, NSA_J, QB, qw), lambda bi, g, i: (bi, g, i, 0)),
                  pl.BlockSpec((1, QB, gates.shape[2]), lambda bi, g, i: (bi, i, 0)),
                  per_bg(ck), per_bg(cv), per_bg(sk), per_bg(sv), per_bg(wk), per_bg(wv),
                  pl.BlockSpec((NSA_J, QB, t_win.shape[2]), lambda bi, g, i: (g, 0, 0)),
                  pl.BlockSpec((NSA_J, QB, t_near.shape[2]), lambda bi, g, i: (g, 0, 0)),
                  pl.BlockSpec(cmat.shape, lambda bi, g, i: (0, 0))],
        out_specs=pl.BlockSpec((1, QB, NSA_J * NSA_DH), lambda bi, g, i: (bi, i, g)),
        out_shape=jax.ShapeDtypeStruct((b, s, h * NSA_DH), BF16),
        compiler_params=_cparams(("parallel", "parallel", "arbitrary")), name="nsa_prompt",
    )(q4, gates, ck, cv, sk, sv, wk, wv, t_win, t_near, cmat)


def _diff_lambda(lq1_ref, lk1_ref, lq2_ref, lk2_ref, lam_init):
    a = jnp.sum(lq1_ref[...] * lk1_ref[...], axis=-1, keepdims=True)
    b = jnp.sum(lq2_ref[...] * lk2_ref[...], axis=-1, keepdims=True)
    return jnp.exp(a) - jnp.exp(b) + lam_init


def _diff_prompt_kernel(q_ref, k_ref, v_ref, tn_ref, lq1_ref, lk1_ref, lq2_ref, lk2_ref, o_ref, *, lam_init):
    i = pl.program_id(2)
    start = i * QB
    rows = DIFF_J * QB
    far_end = jnp.maximum(start - QB, 0)
    n_far = (far_end + KT_DIFF - 1) // KT_DIFF
    q2 = [q_ref[0, 0, m].reshape(rows, DIFF_DH) for m in range(2)]

    def far_body(kt, carry):
        off = pl.multiple_of(PADF + kt * KT_DIFF, PADF)
        v = v_ref[0, 0, :, pl.ds(off, KT_DIFF)]
        keep = kt * KT_DIFF + lax.broadcasted_iota(I32, (1, KT_DIFF), 1) < far_end
        new = []
        for m in range(2):
            s = jnp.dot(q2[m], k_ref[0, 0, m, :, pl.ds(off, KT_DIFF)], preferred_element_type=F32)
            new.append(_flash_step(carry[m], jnp.where(keep, s, NEG), v))
        return tuple(new)

    init = _flash_init(rows, DIFF_VW)
    carry = lax.fori_loop(0, n_far, far_body, (init, init))
    offn = pl.multiple_of(PADF + start - QB, QB)
    v = v_ref[0, 0, :, pl.ds(offn, 2 * QB)]
    in_seq = start - QB + lax.broadcasted_iota(I32, (1, 2 * QB), 1) >= 0
    outs = []
    for m in range(2):
        s = jnp.dot(q2[m], k_ref[0, 0, m, :, pl.ds(offn, 2 * QB)], preferred_element_type=F32)
        s = (s.reshape(DIFF_J, QB, 2 * QB) + tn_ref[m]).reshape(rows, 2 * QB)
        _, acc = _flash_step(carry[m], jnp.where(in_seq, s, NEG), v)
        outs.append(_flash_out(acc, DIFF_DV))
    lam = _diff_lambda(lq1_ref, lk1_ref, lq2_ref, lk2_ref, lam_init)
    o = outs[0] - lam * outs[1]
    o_ref[0] = jnp.concatenate([o[j * QB:(j + 1) * QB] for j in range(DIFF_J)], axis=1)


def _diff_prompt_attention(q6, k5, v4, t_near, lams, lam_init):
    b, g, _, _, s, dh = q6.shape
    sp = k5.shape[4]
    nq = s // QB
    lam_spec = pl.BlockSpec((1, DIFF_DH), lambda bi, gi, i: (0, 0))
    return pl.pallas_call(
        functools.partial(_diff_prompt_kernel, lam_init=lam_init),
        grid=(b, g, nq),
        in_specs=[pl.BlockSpec((1, 1, 2, DIFF_J, QB, dh), lambda bi, gi, i: (bi, gi, 0, 0, i, 0)),
                  pl.BlockSpec((1, 1, 2, dh, sp), lambda bi, gi, i: (bi, gi, 0, 0, 0)),
                  pl.BlockSpec((1, 1, DIFF_VW, sp), lambda bi, gi, i: (bi, gi, 0, 0)),
                  pl.BlockSpec((None, 2, DIFF_J, QB, 2 * QB), lambda bi, gi, i: (gi, 0, 0, 0, 0)),
                  lam_spec, lam_spec, lam_spec, lam_spec],
        out_specs=pl.BlockSpec((1, QB, DIFF_J * DIFF_DV), lambda bi, gi, i: (bi, i, gi)),
        out_shape=jax.ShapeDtypeStruct((b, s, DIFF_HEADS * DIFF_DV), F32),
        compiler_params=_cparams(("parallel", "parallel", "arbitrary")), name="diff_prompt",
    )(q6, k5, v4, t_near, *lams)


def _nsa_samp_a_kernel(pt_ref, *refs, n_pp, n_pages, past, t_new, n_sb):
    page_refs = refs[:n_pp]
    (xc_ref, qbd_ref, perm_ref, bd1_ref, b1_ref, bd2_ref, b2_ref, cm_ref, ocmp_ref, sel_ref, p_ref) = refs[n_pp:]
    i = pl.program_id(1)
    nrow = n_pp * 8
    parts = _page_parts([r[...] for r in page_refs], perm_ref[...], bd1_ref)
    p_ref[pl.ds(pl.multiple_of(i * nrow, nrow), nrow), :] = parts

    @pl.when(i == pl.num_programs(1) - 1)
    def _():
        xc = xc_ref[0]
        accs = []
        for kv in range(2):
            for hp in range(NSA_KV // 2):
                c0 = kv * NSA_KVW + hp * 128
                lhs = jnp.concatenate([xc[s:s + 1, c0:c0 + 128] for s in range(CMP_STRIDE)], axis=1).astype(BF16)
                accs.append(jnp.dot(lhs, bd1_ref[kv], preferred_element_type=F32))
        n_cached = n_pages * 8
        p_ref[pl.ds(n_cached, 1), :] = jnp.concatenate(accs, axis=1)
        ck, cv = _compress_finalize(p_ref, n_cached, b1_ref, bd2_ref, b2_ref)
        qbd = qbd_ref[0]
        nrows = qbd.shape[0]
        lc = lax.dot_general(qbd, ck.astype(BF16), NT_DIMS, preferred_element_type=F32)
        qpos = past + lax.broadcasted_iota(I32, (nrows, 1), 0) % t_new
        c_end = lax.broadcasted_iota(I32, (1, n_cached), 1) * CMP_STRIDE + (CMP_BLOCK - 1)
        pc = _masked_softmax(lc, c_end <= qpos)
        ocmp_ref[0] = jnp.dot(pc.astype(BF16), cv.astype(BF16), preferred_element_type=F32)
        rg = nrows // NSA_J
        ps = pc[0:rg] + pc[rg:2 * rg] + pc[2 * rg:3 * rg] + pc[3 * rg:4 * rg]
        score = _dot_exact_rhs(ps, cm_ref[...])
        sel_ref[0] = _select_blocks(score, qpos[0:rg], n_sb, False)


def _page_specs(n_pp, page_shape):
    def spec(k):
        return pl.BlockSpec((None,) + page_shape, lambda b, i, pt, k=k: (pt[b, i * n_pp + k], 0, 0))
    return [spec(k) for k in range(n_pp)]


def _nsa_sample_a(page_table, cache, xc, qbd, cw, cmat, past, t_new, n_sb, n_pp):
    bd, n_pages = page_table.shape
    nrows = qbd.shape[1]
    rg = nrows // NSA_J
    nsbp = cmat.shape[1]
    const = lambda a: pl.BlockSpec(a.shape, lambda b, i, pt, _n=a.ndim: (0,) * _n)
    grid_spec = pltpu.PrefetchScalarGridSpec(
        num_scalar_prefetch=1, grid=(bd, n_pages // n_pp),
        in_specs=_page_specs(n_pp, cache.shape[1:]) + [
            pl.BlockSpec((1, CMP_STRIDE, xc.shape[2]), lambda b, i, pt: (b, 0, 0)),
            pl.BlockSpec((1, nrows, NSA_KVW), lambda b, i, pt: (b, 0, 0)),
            const(cw["perm"]), const(cw["bd1"]), const(cw["b1"]), const(cw["bd2"]), const(cw["b2"]), const(cmat)],
        out_specs=[pl.BlockSpec((1, nrows, NSA_KVW), lambda b, i, pt: (b, 0, 0)),
                   pl.BlockSpec((1, rg, nsbp), lambda b, i, pt: (b, 0, 0))],
        scratch_shapes=[pltpu.VMEM((n_pages * 8 + 8, 4 * NSA_KVW), F32)])
    return pl.pallas_call(
        functools.partial(_nsa_samp_a_kernel, n_pp=n_pp, n_pages=n_pages, past=past, t_new=t_new, n_sb=n_sb),
        grid_spec=grid_spec,
        out_shape=[jax.ShapeDtypeStruct((bd, nrows, NSA_KVW), F32), jax.ShapeDtypeStruct((bd, rg, nsbp), F32)],
        compiler_params=_cparams(("parallel", "arbitrary")), name="nsa_sample_a",
    )(page_table, *([cache] * n_pp), xc, qbd, cw["perm"], cw["bd1"], cw["b1"], cw["bd2"], cw["b2"], cmat)


def _nsa_samp_b_kernel(pt_ref, *refs, n_pp, n_pages):
    page_refs = refs[:n_pp]
    (qbd_ref, sel_ref, snew_ref, wst_ref, wnew_ref, ocmp_ref, gate_ref, tsn_ref, tsw_ref, fold_ref,
     o_ref, m_ref, l_ref, acc_ref) = refs[n_pp:]
    i = pl.program_id(1)

    @pl.when(i == 0)
    def _():
        m_ref[...] = jnp.full(m_ref.shape, NEG, F32)
        l_ref[...] = jnp.zeros(l_ref.shape, F32)
        acc_ref[...] = jnp.zeros(acc_ref.shape, F32)

    qbd = qbd_ref[0]
    sel = jnp.concatenate([sel_ref[0]] * NSA_J, axis=0)
    near0 = 2 * (n_pages - 1)
    blk = lax.broadcasted_iota(I32, (1, sel.shape[1]), 1)
    sel_far = jnp.where(blk < near0, sel, 0.0).astype(BF16)
    nk = n_pp * PAGE
    kcat = jnp.concatenate([r[0:NSA_KVW, :] for r in page_refs], axis=1).astype(BF16)
    vcat = jnp.concatenate([r[NSA_KVW:2 * NSA_KVW, :] for r in page_refs], axis=1).astype(BF16)
    s = jnp.dot(qbd, kcat, preferred_element_type=F32)
    keep = _block_expand(sel_far, i * (nk // SEL_BLOCK), nk) > 0.5
    carry = _online_update((m_ref[...], l_ref[...], acc_ref[...]), jnp.where(keep, s, NEG), _pv_t(vcat))
    m_ref[...], l_ref[...], acc_ref[...] = carry

    @pl.when(i == pl.num_programs(1) - 1)
    def _():
        last = page_refs[n_pp - 1]
        snew = snew_ref[0]
        kn = jnp.concatenate([last[0:NSA_KVW, :], snew[0:NSA_KVW, :]], axis=1).astype(BF16)
        vn = jnp.concatenate([last[NSA_KVW:2 * NSA_KVW, :], snew[NSA_KVW:2 * NSA_KVW, :]], axis=1).astype(BF16)
        s = jnp.dot(qbd, kn, preferred_element_type=F32) + tsn_ref[...]
        keep = _block_expand(sel.astype(BF16), near0, 2 * PAGE) > 0.5
        _, l, acc = _online_update((m_ref[...], l_ref[...], acc_ref[...]), jnp.where(keep, s, NEG), _pv_t(vn))
        o_slc = acc / l
        wst = wst_ref[0]
        wnew = wnew_ref[0]
        wk = jnp.concatenate([wst[0:NSA_KVW, :], wnew[0:NSA_KVW, :]], axis=1).astype(BF16)
        wv = jnp.concatenate([wst[NSA_KVW:2 * NSA_KVW, :], wnew[NSA_KVW:2 * NSA_KVW, :]], axis=1).astype(BF16)
        s = jnp.dot(qbd, wk, preferred_element_type=F32) + tsw_ref[...]
        o_win = _softmax_pv_t(s, wv)
        gt = gate_ref[0]
        o_full = gt[:, 0:1] * ocmp_ref[0] + gt[:, 1:2] * o_slc + gt[:, 2:3] * o_win
        nrows = o_full.shape[0]
        t_new = nrows // NSA_HEADS
        row_g = (lax.broadcasted_iota(I32, o_full.shape, 0) // t_new) % NSA_KV
        col_g = lax.broadcasted_iota(I32, o_full.shape, 1) // NSA_DH
        od = jnp.where(row_g == col_g, o_full, 0.0).astype(BF16)
        o_ref[0] = jnp.dot(od, fold_ref[...], preferred_element_type=F32).astype(o_ref.dtype)


def _nsa_sample_b(page_table, cache, qbd, sel, snew, wst, wnew, ocmp, gates, tsn, tsw, fold, n_pp):
    bd, n_pages = page_table.shape
    nrows = qbd.shape[1]
    per_b = lambda a: pl.BlockSpec((1,) + a.shape[1:], lambda b, i, pt, _n=a.ndim: (b,) + (0,) * (_n - 1))
    const = lambda a: pl.BlockSpec(a.shape, lambda b, i, pt, _n=a.ndim: (0,) * _n)
    grid_spec = pltpu.PrefetchScalarGridSpec(
        num_scalar_prefetch=1, grid=(bd, n_pages // n_pp),
        in_specs=_page_specs(n_pp, cache.shape[1:]) + [
            per_b(qbd), per_b(sel), per_b(snew), per_b(wst), per_b(wnew), per_b(ocmp), per_b(gates),
            const(tsn), const(tsw), const(fold)],
        out_specs=pl.BlockSpec((1, nrows, NSA_DH), lambda b, i, pt: (b, 0, 0)),
        scratch_shapes=[pltpu.VMEM((nrows, 1), F32), pltpu.VMEM((nrows, 1), F32), pltpu.VMEM((nrows, NSA_KVW), F32)])
    return pl.pallas_call(
        functools.partial(_nsa_samp_b_kernel, n_pp=n_pp, n_pages=n_pages),
        grid_spec=grid_spec,
        out_shape=jax.ShapeDtypeStruct((bd, nrows, NSA_DH), BF16),
        compiler_params=_cparams(("parallel", "arbitrary")), name="nsa_sample_b",
    )(page_table, *([cache] * n_pp), qbd, sel, snew, wst, wnew, ocmp, gates, tsn, tsw, fold)


def _diff_samp_kernel(pt_ref, *refs, n_pp, lam_init):
    page_refs = refs[:n_pp]
    (qbd_ref, dnew_ref, tdn_ref, lq1_ref, lk1_ref, lq2_ref, lk2_ref, o_ref, m_ref, l_ref, acc_ref) = refs[n_pp:]
    i = pl.program_id(1)
    last_step = i == pl.num_programs(1) - 1
    kw = DIFF_KV * 2 * DIFF_DH

    @pl.when(i == 0)
    def _():
        m_ref[...] = jnp.full(m_ref.shape, NEG, F32)
        l_ref[...] = jnp.zeros(l_ref.shape, F32)
        acc_ref[...] = jnp.zeros(acc_ref.shape, F32)

    qbd = qbd_ref[0]
    rg = qbd.shape[0] // DIFF_KV
    nk = n_pp * PAGE
    groups = range(DIFF_KV)

    def k_rows(ref, g):
        return ref[pl.ds(g, PAGE, stride=2 * DIFF_KV), :]

    def v_rows(ref, g):
        return ref[pl.ds(DIFF_KV + g, PAGE, stride=2 * DIFF_KV), :]

    def scores(k_of):
        return jnp.concatenate([lax.dot_general(qbd[g * rg:(g + 1) * rg], k_of(g), NT_DIMS,
                                                preferred_element_type=F32) for g in groups], axis=0)

    def pv(v_of):
        return lambda p: jnp.concatenate([jnp.dot(p[g * rg:(g + 1) * rg], v_of(g), preferred_element_type=F32)
                                          for g in groups], axis=0)

    s = scores(lambda g: jnp.concatenate([k_rows(r, g) for r in page_refs], axis=0).astype(BF16))
    limit = jnp.where(last_step, nk - PAGE, nk)
    s = jnp.where(lax.broadcasted_iota(I32, (1, nk), 1) < limit, s, NEG)
    carry = _online_update((m_ref[...], l_ref[...], acc_ref[...]), s,
                           pv(lambda g: jnp.concatenate([v_rows(r, g) for r in page_refs], axis=0).astype(BF16)))
    m_ref[...], l_ref[...], acc_ref[...] = carry

    @pl.when(last_step)
    def _():
        last = page_refs[n_pp - 1]
        dnew = dnew_ref[0]
        s = scores(lambda g: jnp.concatenate(
            [k_rows(last, g), dnew[:, g * DIFF_DV:(g + 1) * DIFF_DV]], axis=0).astype(BF16)) + tdn_ref[...]
        _, l, acc = _online_update((m_ref[...], l_ref[...], acc_ref[...]), s, pv(lambda g: jnp.concatenate(
            [v_rows(last, g), dnew[:, kw + g * DIFF_DV:kw + (g + 1) * DIFF_DV]], axis=0).astype(BF16)))
        o_full = acc / l
        lam = _diff_lambda(lq1_ref, lk1_ref, lq2_ref, lk2_ref, lam_init)
        h = rg // 2
        o_ref[0] = jnp.concatenate([o_full[g * rg:g * rg + h] - lam * o_full[g * rg + h:(g + 1) * rg]
                                    for g in groups], axis=0)


def _diff_sample(page_table, cache, qbd, dnew, tdn, lams, lam_init, n_pp):
    bd, n_pages = page_table.shape
    nrows = qbd.shape[1]
    kw = DIFF_KV * 2 * DIFF_DH
    per_b = lambda a: pl.BlockSpec((1,) + a.shape[1:], lambda b, i, pt, _n=a.ndim: (b,) + (0,) * (_n - 1))
    const = lambda a: pl.BlockSpec(a.shape, lambda b, i, pt, _n=a.ndim: (0,) * _n)
    grid_spec = pltpu.PrefetchScalarGridSpec(
        num_scalar_prefetch=1, grid=(bd, n_pages // n_pp),
        in_specs=_page_specs(n_pp, cache.shape[1:]) + [per_b(qbd), per_b(dnew), const(tdn)] + [const(a) for a in lams],
        out_specs=pl.BlockSpec((1, nrows // 2, DIFF_DV), lambda b, i, pt: (b, 0, 0)),
        scratch_shapes=[pltpu.VMEM((nrows, 1), F32), pltpu.VMEM((nrows, 1), F32), pltpu.VMEM((nrows, DIFF_DV), F32)])
    return pl.pallas_call(
        functools.partial(_diff_samp_kernel, n_pp=n_pp, lam_init=lam_init),
        grid_spec=grid_spec,
        out_shape=jax.ShapeDtypeStruct((bd, nrows // 2, DIFF_DV), F32),
        compiler_params=_cparams(("parallel", "arbitrary")), name="diff_sample",
    )(page_table, *([cache] * n_pp), qbd, dnew, tdn, *lams)


def _route(lt):
    mx = jnp.max(lt, axis=0, keepdims=True)
    e = jnp.exp(lt - mx)
    p = e / jnp.sum(e, axis=0, keepdims=True)
    best = None
    grp = None
    for g in range(N_GROUPS):
        r = [p[EPG * g + k:EPG * g + k + 1] for k in range(EPG)]
        a, b = jnp.maximum(r[0], r[1]), jnp.minimum(r[0], r[1])
        c, d = jnp.maximum(r[2], r[3]), jnp.minimum(r[2], r[3])
        sc = jnp.maximum(a, c) + jnp.maximum(jnp.minimum(a, c), jnp.maximum(b, d))
        if g == 0:
            best, grp = sc, jnp.zeros(sc.shape, I32)
        else:
            better = sc > best
            grp = jnp.where(better, g, grp)
            best = jnp.where(better, sc, best)
    lg = []
    for k in range(EPG):
        v = lt[k:k + 1]
        for g in range(1, N_GROUPS):
            v = jnp.where(grp == g, lt[EPG * g + k:EPG * g + k + 1], v)
        lg.append(v)

    def first_max(vals):
        vmax = jnp.maximum(jnp.maximum(vals[0], vals[1]), jnp.maximum(vals[2], vals[3]))
        idx = jnp.where(vals[0] == vmax, 0, jnp.where(vals[1] == vmax, 1, jnp.where(vals[2] == vmax, 2, 3)))
        return vmax, idx

    v1, i1 = first_max(lg)
    rest = [jnp.where(i1 == k, -jnp.inf, lg[k]) for k in range(EPG)]
    v2, i2 = first_max(rest)
    e2 = jnp.exp(v2 - v1)
    den = 1.0 + e2
    return grp * EPG + i1, grp * EPG + i2, 1.0 / den, e2 / den


def _post_attn_kernel(o_ref, x_ref, w_ref, g_ref, b_ref, wr_ref, rb_ref, gain_ref, x1_ref, eidx_ref, gate_ref, *,
                      merge_scale):
    o = o_ref[...]
    if merge_scale is not None:
        segs = []
        for h in range(DIFF_HEADS):
            seg = o[:, h * DIFF_DV:(h + 1) * DIFF_DV]
            ms = jnp.mean(seg * seg, axis=-1, keepdims=True)
            segs.append(seg * lax.rsqrt(ms + LN_EPS) * gain_ref[...] * merge_scale)
        o = jnp.concatenate(segs, axis=1)
    mix = jnp.dot(o.astype(BF16), w_ref[...], preferred_element_type=F32)
    x1 = _layer_norm(DN_ALPHA * x_ref[...] + mix, g_ref[...], b_ref[...])
    x1_ref[...] = x1
    lt = lax.dot_general(wr_ref[...], x1, NT_DIMS, precision=HI, preferred_element_type=F32) + rb_ref[...]
    e1, e2, g1, g2 = _route(lt)
    eidx_ref[0:1, :] = e1
    eidx_ref[1:2, :] = e2
    gate_ref[0:1, :] = g1
    gate_ref[1:2, :] = g2


def _post_attention(o, x, w_out, ln_g, ln_b, wr_t, rb, gain, merge_scale):
    nt, d = x.shape
    tm = _token_tile(nt)
    row = lambda i: (i, 0)
    const = lambda a: pl.BlockSpec(a.shape, lambda i: (0, 0))
    return pl.pallas_call(
        functools.partial(_post_attn_kernel, merge_scale=merge_scale),
        grid=(nt // tm,),
        in_specs=[pl.BlockSpec((tm, d), row), pl.BlockSpec((tm, d), row), const(w_out), const(ln_g), const(ln_b),
                  const(wr_t), const(rb), const(gain)],
        out_specs=[pl.BlockSpec((tm, d), row), pl.BlockSpec((TOP_K, tm), lambda i: (0, i)),
                   pl.BlockSpec((TOP_K, tm), lambda i: (0, i))],
        out_shape=[jax.ShapeDtypeStruct((nt, d), F32), jax.ShapeDtypeStruct((TOP_K, nt), I32),
                   jax.ShapeDtypeStruct((TOP_K, nt), F32)],
        compiler_params=_cparams(("parallel",)), name="post_attn",
    )(o, x, w_out, ln_g, ln_b, wr_t, rb, gain)


def _start_rows(idx_ref, src_hbm, buf, slot, sem, n):
    def body(r0, c):
        for k in range(DMA_UNROLL):
            r = r0 * DMA_UNROLL + k
            pltpu.make_async_copy(src_hbm.at[pl.ds(idx_ref[0, r], 1), :], buf.at[slot, pl.ds(r, 1), :],
                                  sem.at[slot]).start(priority=k % 2)
        return c
    lax.fori_loop(0, n // DMA_UNROLL, body, 0)


def _wait_rows(src_hbm, buf, slot, sem, n):
    pltpu.make_async_copy(src_hbm.at[pl.ds(0, n), :], buf.at[slot], sem.at[slot]).wait()


def _moe_kernel(blk_e_ref, nv_ref, tok0_ref, tokn_ref, x_hbm, wgu_ref, wdn_ref, o_ref, xbuf, sem):
    i = pl.program_id(0)
    nv = nv_ref[0]
    tm = xbuf.shape[1]
    de = wdn_ref.shape[1]

    @pl.when(i == 0)
    def _():
        _start_rows(tok0_ref, x_hbm, xbuf, 0, sem, tm)

    @pl.when(i + 1 < nv)
    def _():
        _start_rows(tokn_ref, x_hbm, xbuf, (i + 1) % 2, sem, tm)

    @pl.when(i < nv)
    def _():
        slot = i % 2
        _wait_rows(x_hbm, xbuf, slot, sem, tm)
        xb = xbuf[slot].astype(BF16)
        acc = jnp.zeros(o_ref.shape, F32)
        half = de // 2
        for c in range(2):
            gate = jnp.dot(xb, wgu_ref[0, :, c * half:(c + 1) * half], preferred_element_type=F32)
            up = jnp.dot(xb, wgu_ref[0, :, de + c * half:de + (c + 1) * half], preferred_element_type=F32)
            act = (gate * jax.nn.sigmoid(gate) * up).astype(BF16)
            acc = acc + jnp.dot(act, wdn_ref[0, c * half:(c + 1) * half, :], preferred_element_type=F32)
        o_ref[...] = acc

    @pl.when(i >= nv)
    def _():
        o_ref[...] = jnp.zeros(o_ref.shape, F32)


def _moe_experts(x1, tok_blocks, blk_e, n_valid, w_gu, w_dn):
    nt, d = x1.shape
    n_blocks, _, tm = tok_blocks.shape
    grid_spec = pltpu.PrefetchScalarGridSpec(
        num_scalar_prefetch=2, grid=(n_blocks,),
        in_specs=[pl.BlockSpec((None, 1, tm), lambda i, be, nv: (i, 0, 0), memory_space=pltpu.SMEM),
                  pl.BlockSpec((None, 1, tm), lambda i, be, nv: (jnp.minimum(i + 1, n_blocks - 1), 0, 0),
                               memory_space=pltpu.SMEM),
                  pl.BlockSpec(memory_space=pl.ANY),
                  pl.BlockSpec((1,) + w_gu.shape[1:], lambda i, be, nv: (be[i], 0, 0)),
                  pl.BlockSpec((1,) + w_dn.shape[1:], lambda i, be, nv: (be[i], 0, 0))],
        out_specs=pl.BlockSpec((tm, d), lambda i, be, nv: (i, 0)),
        scratch_shapes=[pltpu.VMEM((2, tm, d), F32), pltpu.SemaphoreType.DMA((2,))])
    return pl.pallas_call(
        _moe_kernel, grid_spec=grid_spec,
        out_shape=jax.ShapeDtypeStruct((n_blocks * tm, d), F32),
        compiler_params=_cparams(("arbitrary",)), name="moe_experts",
    )(blk_e, n_valid, tok_blocks, tok_blocks, x1, w_gu, w_dn)


def _moe_combine_kernel(pos0_ref, posn_ref, y_hbm, x1_ref, gate_ref, g_ref, b_ref, o_ref, ybuf, sem):
    i = pl.program_id(0)
    n = pl.num_programs(0)
    tm = x1_ref.shape[0]

    @pl.when(i == 0)
    def _():
        _start_rows(pos0_ref, y_hbm, ybuf, 0, sem, TOP_K * tm)

    @pl.when(i + 1 < n)
    def _():
        _start_rows(posn_ref, y_hbm, ybuf, (i + 1) % 2, sem, TOP_K * tm)

    slot = i % 2
    _wait_rows(y_hbm, ybuf, slot, sem, TOP_K * tm)
    gt = gate_ref[...]
    f = gt[:, 0:1] * ybuf[slot, pl.ds(0, tm), :] + gt[:, 1:2] * ybuf[slot, pl.ds(tm, tm), :]
    o_ref[...] = _layer_norm(DN_ALPHA * x1_ref[...] + f, g_ref[...], b_ref[...])


def _moe_combine(y_sorted, pos_blocks, x1, gates_t, ln_g, ln_b):
    nt, d = x1.shape
    n_tiles, _, tm2 = pos_blocks.shape
    tm = tm2 // TOP_K
    const = lambda a: pl.BlockSpec(a.shape, lambda i: (0, 0))
    return pl.pallas_call(
        _moe_combine_kernel, grid=(n_tiles,),
        in_specs=[pl.BlockSpec((None, 1, tm2), lambda i: (i, 0, 0), memory_space=pltpu.SMEM),
                  pl.BlockSpec((None, 1, tm2), lambda i: (jnp.minimum(i + 1, n_tiles - 1), 0, 0),
                               memory_space=pltpu.SMEM),
                  pl.BlockSpec(memory_space=pl.ANY),
                  pl.BlockSpec((tm, d), lambda i: (i, 0)), pl.BlockSpec((tm, TOP_K), lambda i: (i, 0)),
                  const(ln_g), const(ln_b)],
        out_specs=pl.BlockSpec((tm, d), lambda i: (i, 0)),
        out_shape=jax.ShapeDtypeStruct((nt, d), F32),
        scratch_shapes=[pltpu.VMEM((2, tm2, d), F32), pltpu.SemaphoreType.DMA((2,))],
        compiler_params=_cparams(("arbitrary",)), name="moe_combine",
    )(pos_blocks, pos_blocks, y_sorted, x1, gates_t, ln_g, ln_b)


def _moe_layer(x1, eidx, gate, w_gu, w_dn, ln_g, ln_b):
    nt, _ = x1.shape
    a = nt * TOP_K
    flat_e = eidx.T.reshape(a)
    onehot = (flat_e[:, None] == jnp.arange(N_EXPERTS, dtype=I32)[None, :]).astype(I32)
    running = jnp.cumsum(onehot, axis=0)
    sizes = running[-1]
    rank = jnp.sum(onehot * running, axis=1) - 1
    padded = (sizes + MOE_TM - 1) // MOE_TM * MOE_TM
    pad_end = jnp.cumsum(padded)
    pad_start = pad_end - padded
    pos = (jnp.sum(onehot * pad_start[None, :], axis=1) + rank).astype(I32)
    n_blocks = -(-a // MOE_TM) + N_EXPERTS
    tok_pad = jnp.zeros((n_blocks * MOE_TM,), I32).at[pos].set(jnp.arange(a, dtype=I32) // TOP_K)
    blk_start = jnp.arange(n_blocks, dtype=I32)[:, None] * MOE_TM
    blk_e = jnp.minimum(jnp.sum((pad_end[None, :] <= blk_start).astype(I32), axis=1), N_EXPERTS - 1).astype(I32)
    n_valid = (pad_end[-1:] // MOE_TM).astype(I32)
    y_sorted = _moe_experts(x1, tok_pad.reshape(n_blocks, 1, MOE_TM), blk_e, n_valid, w_gu, w_dn)
    tm = _token_tile(nt)
    pos_blocks = pos.reshape(nt // tm, tm, TOP_K).transpose(0, 2, 1).reshape(nt // tm, 1, TOP_K * tm)
    return _moe_combine(y_sorted, pos_blocks, x1, gate.T, ln_g, ln_b)


def _compress_weights(cmp_k, cmp_v):
    eye = jnp.eye(NSA_KV, dtype=F32)
    eye2 = jnp.eye(2, dtype=F32)
    bd1, b1, bd2, b2 = [], [], [], []
    for (w1, bias1, w2, bias2) in (cmp_k, cmp_v):
        w1r = w1.reshape(CMP_SPAN, CMP_STRIDE, NSA_DH, NSA_DH)
        bd1.append(jnp.einsum("rsdh,ab->sadrbh", w1r, eye2).reshape(CMP_STRIDE * 2 * NSA_DH, CMP_SPAN * 2 * NSA_DH))
        bd2.append(jnp.einsum("hd,ab->ahbd", w2, eye).reshape(NSA_KVW, NSA_KVW))
        b1.append(jnp.tile(bias1, NSA_KV)[None])
        b2.append(jnp.tile(bias2, NSA_KV)[None])
    r = np.arange(PAGE)
    perm = np.zeros((PAGE, PAGE), np.float32)
    perm[r, CMP_STRIDE * (r % 8) + r // 8] = 1.0
    return {"perm": jnp.asarray(perm, BF16), "bd1": jnp.stack(bd1).astype(BF16), "b1": jnp.stack(b1),
            "bd2": jnp.stack(bd2).astype(BF16), "b2": jnp.stack(b2)}


def _chunk_score_matrix(n_blocks_in, n_sb):
    lanes = -(-n_sb // 128) * 128
    n = np.arange(n_blocks_in)[:, None]
    b = np.arange(lanes)[None, :]
    m = ((n >= 4 * b) & (n <= 4 * b + 3)).astype(np.float32) + ((n + 1 >= 4 * b) & (n + 1 <= 4 * b + 3)).astype(np.float32)
    m = np.where(b < n_sb, m, 0.0)
    return jnp.asarray(m, BF16)


def kernel(x_prompt, x_sample, cache_cmp_kv, cache_slc_kv, state_win_kv, cache_diff_kv, page_table,
           nsa_w_in, nsa_w_out, cmp_k_w1, cmp_k_b1, cmp_k_w2, cmp_k_b2, cmp_v_w1, cmp_v_b1, cmp_v_w2, cmp_v_b2,
           diff_w_in, diff_w_out, lambda_q1, lambda_k1, lambda_q2, lambda_k2, diff_subln_gain, rel_bias,
           ln_gain, ln_bias, router_w, router_b, moe_w_gate_up, moe_w_down):
    b, s, d = x_prompt.shape
    bd, t_new, _ = x_sample.shape
    n_pages = page_table.shape[1]
    past = n_pages * PAGE
    assert s % KT == 0 and s >= WINDOW and past >= WINDOW and state_win_kv.shape[1] == WINDOW
    assert t_new <= CMP_STRIDE and (NSA_HEADS * t_new) % 8 == 0 and (DIFF_J * t_new) % 8 == 0
    n_p = b * s
    n_s = bd * t_new
    x_p = x_prompt.reshape(n_p, d)
    x_s = x_sample.reshape(n_s, d)

    hq = NSA_HEADS * NSA_DH
    wr_t = router_w.T
    rb = router_b[:, None]
    w_gu = moe_w_gate_up.astype(BF16)
    w_dn = moe_w_down.astype(BF16)
    t640 = _toeplitz_bias(rel_bias, QB, WINDOW + QB, WINDOW, 0, WINDOW, False)
    t_near = _toeplitz_bias(rel_bias, QB, 2 * QB, QB, 0, 1 << 30, True)
    n_pp = 16 if n_pages % 16 == 0 else 8
    assert n_pages % n_pp == 0

    w = nsa_w_in.astype(BF16)
    wg = jnp.pad(w[:, hq + 6 * NSA_KVW:], ((0, 0), (0, 128 - 3 * NSA_HEADS)))
    kv_w = [w[:, hq + 2 * k * NSA_KVW: hq + 2 * (k + 1) * NSA_KVW] for k in range(3)]
    f5 = (BF16, F32, F32, F32, F32)
    q4, cmp_t, slc_t, win_t, gates_p = _project(
        x_p, [w[:, :hq]] + [a.T for a in kv_w] + [wg],
        ("q_heads_flag", "transposed", "transposed", "transposed", "sigmoid"), f5, s)
    q_s, cmp_s, slc_s, win_s, gates_s = _project(x_s, [w[:, :hq]] + kv_w + [wg],
                                                 ("qscale", None, None, None, "sigmoid"), f5)
    cw = _compress_weights((cmp_k_w1, cmp_k_b1, cmp_k_w2, cmp_k_b2), (cmp_v_w1, cmp_v_b1, cmp_v_w2, cmp_v_b2))

    kvw = 2 * NSA_KVW
    ck, cv = _compress_prompt(cmp_t, cw)

    def per_group(a):
        return a.reshape(b, a.shape[1], NSA_KV, NSA_DH).transpose(0, 2, 1, 3)

    n_sb_p = s // SEL_BLOCK
    sp = s + 2 * PADF
    pos = np.arange(sp) - PADF
    inside = (pos >= 0) & (pos < s)
    blk_rows = -(-n_sb_p // 128) * 128
    onehot = (pos[None, :] // SEL_BLOCK == np.arange(blk_rows)[:, None]) & inside[None, :]
    flag_rows = np.zeros((AUG, sp), np.float32)
    flag_rows[0] = ~inside
    ones_rows = np.zeros((AUG, sp), np.float32)
    ones_rows[0] = 1.0

    def per_bg(a):
        return jnp.broadcast_to(jnp.asarray(a, BF16), (b, NSA_KV) + a.shape)

    def kv_t(rows_t):
        r = jnp.pad(rows_t.astype(BF16), ((0, 0), (0, 0), (PADF, PADF))).reshape(b, 2, NSA_KV, NSA_DH, sp)
        return r[:, 0], r[:, 1]

    skt, svt = kv_t(slc_t)
    wkt, wvt = kv_t(win_t)
    sk = jnp.concatenate([per_bg(onehot), skt, per_bg(flag_rows)], axis=2)
    sv = jnp.concatenate([svt, per_bg(ones_rows)], axis=2)
    wk = jnp.concatenate([wkt, per_bg(flag_rows)], axis=2)
    wv = jnp.concatenate([wvt, per_bg(ones_rows)], axis=2)
    ck4 = jnp.pad(per_group(ck), ((0, 0), (0, 0), (0, 0), (0, AUG)))
    o4 = _nsa_prompt_attention(q4, gates_p.reshape(b, s, 128), ck4, per_group(cv), sk, sv, wk, wv,
                               t640, t_near, _chunk_score_matrix(s // CMP_STRIDE, n_sb_p), n_sb_p)
    o_p = o4.reshape(n_p, hq)

    nrows = NSA_HEADS * t_new
    n_sb_s = -(-(past + t_new) // SEL_BLOCK)
    qs = q_s.reshape(bd, t_new, NSA_KV, NSA_J, NSA_DH).transpose(0, 3, 2, 1, 4)
    qbd = jnp.einsum("bjgtd,gh->bjgthd", qs, jnp.eye(NSA_KV, dtype=BF16)).reshape(bd, nrows, NSA_KVW)

    def native_t(a):
        return a.transpose(0, 2, 3, 4, 1).reshape(a.shape[0], kvw, a.shape[1])

    def new_rows_t(rows):
        return jnp.pad(rows.reshape(bd, t_new, kvw).transpose(0, 2, 1), ((0, 0), (0, 0), (0, PAGE - t_new)))

    xc = jnp.pad(cmp_s.reshape(bd, t_new, kvw), ((0, 0), (0, CMP_STRIDE - t_new), (0, 0)))
    ocmp, sel = _nsa_sample_a(page_table, native_t(cache_cmp_kv), xc, qbd, cw,
                              _chunk_score_matrix(n_pages * 8, n_sb_s), past, t_new, n_sb_s, n_pp)
    head_of_row = (np.arange(nrows) // t_new % NSA_KV) * NSA_J + np.arange(nrows) // (t_new * NSA_KV)
    tok_of_row = np.arange(nrows) % t_new
    tsn = _toeplitz_bias(rel_bias, t_new, 2 * PAGE, PAGE, 0, 1 << 30, True)[head_of_row, tok_of_row]
    tsw_full = _toeplitz_bias(rel_bias, t_new, WINDOW + PAGE, WINDOW, 0, WINDOW, False)
    tsw_full = jnp.where(jnp.arange(WINDOW + PAGE)[None, None, :] < WINDOW + t_new, tsw_full, NEG)
    tsw = tsw_full[head_of_row, tok_of_row]
    gs = gates_s[:, :3 * NSA_HEADS].reshape(bd, t_new, NSA_KV, NSA_J, 3).transpose(0, 3, 2, 1, 4).reshape(bd, nrows, 3)
    fold = jnp.asarray(np.tile(np.eye(NSA_DH, dtype=np.float32), (NSA_KV, 1)), BF16)
    o_s = _nsa_sample_b(page_table, native_t(cache_slc_kv), qbd, sel, new_rows_t(slc_s), native_t(state_win_kv),
                        new_rows_t(win_s), ocmp, gs, tsn, tsw, fold, n_pp)
    o_s = o_s.reshape(bd, NSA_J, NSA_KV, t_new, NSA_DH).transpose(0, 3, 2, 1, 4).reshape(n_s, hq)

    ones = jnp.ones((1, DIFF_DV), F32)
    x_all = jnp.concatenate([x_p, x_s], axis=0)
    x1, eidx, gate = _post_attention(jnp.concatenate([o_p, o_s], axis=0), x_all, nsa_w_out.astype(BF16),
                                     ln_gain[0, 0][None], ln_bias[0, 0][None], wr_t, rb, ones, None)
    x_all = _moe_layer(x1, eidx, gate, w_gu[0], w_dn[0], ln_gain[0, 1][None], ln_bias[0, 1][None])

    lam_init = 0.8 - 0.6 * math.exp(-0.3 * 1)
    lams = [a[None] for a in (lambda_q1, lambda_k1, lambda_q2, lambda_k2)]
    wd = diff_w_in.astype(BF16)
    dq = DIFF_HEADS * 2 * DIFF_DH
    dkw = DIFF_KV * DIFF_DV
    wq_gmj = wd[:, :dq].reshape(d, DIFF_KV, DIFF_J, 2, DIFF_DH).transpose(0, 1, 3, 2, 4).reshape(d, dq)
    q6, diff_p, diff_t = _project(x_all[:n_p], [wq_gmj, wd[:, dq:], wd[:, dq:].T],
                                  ("q_heads", "split128", "transposed"), (BF16, F32, BF16), s)
    q6 = q6.reshape(b, DIFF_KV, 2, DIFF_J, s, DIFF_DH)
    qd_s, diff_s = _project(x_all[n_p:], [wd[:, :dq], wd[:, dq:]], ("qscale", "split128"), (BF16, F32))
    tn_d = t_near.reshape(DIFF_KV, DIFF_J, 2, QB, 2 * QB).transpose(0, 2, 1, 3, 4)
    spd = s + PADF + KT_DIFF
    dt = jnp.pad(diff_t, ((0, 0), (0, 0), (PADF, KT_DIFF)))
    k5 = dt[:, :dkw].reshape(b, DIFF_KV, 2, DIFF_DH, spd)
    ones_d = np.zeros((AUG, spd), np.float32)
    ones_d[0] = 1.0
    ones_d = jnp.broadcast_to(jnp.asarray(ones_d, BF16), (b, DIFF_KV, AUG, spd))
    v4 = jnp.concatenate([dt[:, dkw:].reshape(b, DIFF_KV, DIFF_DV, spd), ones_d], axis=2)
    o_p = _diff_prompt_attention(q6, k5, v4, tn_d, lams, lam_init).reshape(n_p, DIFF_HEADS * DIFF_DV)

    nrows_d = 2 * DIFF_HEADS * t_new
    qsd = qd_s.reshape(bd, t_new, DIFF_KV, DIFF_J, 2, DIFF_DH).transpose(0, 2, 4, 3, 1, 5)
    qbd_d = jnp.einsum("bgmjtd,mM->bgmjtMd", qsd, jnp.eye(2, dtype=BF16)).reshape(bd, nrows_d, 2 * DIFF_DH)
    r = np.arange(nrows_d)
    col_of_row = (r // (2 * DIFF_J * t_new)) * 4 + (r // t_new % DIFF_J) * 2 + r // (DIFF_J * t_new) % 2
    tdn_full = _toeplitz_bias(rel_bias, t_new, 2 * PAGE, PAGE, 0, 1 << 30, True)
    tdn_full = jnp.where(jnp.arange(2 * PAGE)[None, None, :] < PAGE + t_new, tdn_full, NEG)
    tdn = tdn_full[col_of_row, r % t_new]
    dnew = jnp.pad(diff_s.reshape(bd, t_new, 2 * dkw), ((0, 0), (0, PAGE - t_new), (0, 0)))
    n_pp_d = 8
    cache_d = cache_diff_kv.reshape(-1, PAGE * 2 * DIFF_KV, DIFF_DV)
    o_s = _diff_sample(page_table, cache_d, qbd_d, dnew, tdn, lams, lam_init, n_pp_d)
    o_s = o_s.reshape(bd, DIFF_KV, DIFF_J, t_new, DIFF_DV).transpose(0, 3, 1, 2, 4).reshape(n_s, DIFF_HEADS * DIFF_DV)

    x1, eidx, gate = _post_attention(jnp.concatenate([o_p, o_s], axis=0), x_all, diff_w_out.astype(BF16),
                                     ln_gain[1, 0][None], ln_bias[1, 0][None], wr_t, rb, diff_subln_gain[None],
                                     1.0 - lam_init)
    x_all = _moe_layer(x1, eidx, gate, w_gu[1], w_dn[1], ln_gain[1, 1][None], ln_bias[1, 1][None])

    kv5 = (2, NSA_KV, NSA_DH)

    def rows_out(a_t):
        return a_t.reshape((b,) + kv5 + (a_t.shape[2],)).transpose(0, 4, 1, 2, 3)

    win_out_s = jnp.concatenate([state_win_kv, win_s.reshape((bd, t_new) + kv5)], axis=1)[:, t_new:]
    dshape = (2, DIFF_KV, DIFF_DV)
    return (x_all[:n_p].reshape(b, s, d), x_all[n_p:].reshape(bd, t_new, d),
            rows_out(cmp_t), cmp_s.reshape((bd, t_new) + kv5), rows_out(slc_t), slc_s.reshape((bd, t_new) + kv5),
            rows_out(win_t[:, :, s - WINDOW:]), win_out_s,
            diff_p.reshape((b, s) + dshape), diff_s.reshape((bd, t_new) + dshape))
```

```python
import functools
import math

import jax
import jax.numpy as jnp
import numpy as np
from jax import lax
from jax.experimental import pallas as pl
from jax.experimental.pallas import tpu as pltpu

F32 = jnp.float32
BF16 = jnp.bfloat16
I32 = jnp.int32
HI = lax.Precision.HIGHEST
NT_DIMS = (((1,), (1,)), ((), ()))

NSA_HEADS = 16
NSA_KV = 4
NSA_J = NSA_HEADS // NSA_KV
NSA_DH = 64
NSA_KVW = NSA_KV * NSA_DH
CMP_BLOCK = 32
CMP_STRIDE = 16
CMP_SPAN = CMP_BLOCK // CMP_STRIDE
SEL_BLOCK = 64
SEL_CHUNKS = SEL_BLOCK // CMP_STRIDE
SEL_TOPN = 16
WINDOW = 512
QB = 128
DIFF_HEADS = 8
DIFF_KV = 4
DIFF_J = DIFF_HEADS // DIFF_KV
DIFF_DH = 64
DIFF_DV = 2 * DIFF_DH
NUM_BUCKETS = 32
MAX_DISTANCE = 128
N_EXPERTS = 16
N_GROUPS = 4
EPG = N_EXPERTS // N_GROUPS
TOP_K = 2
DEPTH = 2
DN_ALPHA = (2 * DEPTH) ** 0.25
LN_EPS = 1e-5

PAGE = 128
KT = 512
KT_DIFF = 1024
PADF = 512
AUG = 16
NSA_QW = NSA_DH + AUG
DIFF_VW = DIFF_DV + AUG
MOE_TM = 512
DMA_UNROLL = 8
NEG = -1e30
BIG = 1e30
VMEM_LIMIT = 56 * 1024 * 1024


def _cparams(sem):
    return pltpu.CompilerParams(dimension_semantics=sem, vmem_limit_bytes=VMEM_LIMIT)


def _token_tile(nt):
    for tm in (512, 384, 256, 128):
        if nt % tm == 0:
            return tm
    raise ValueError(f"token count {nt} is not a multiple of 128")


def _bucket_np(dist):
    n = np.maximum(dist, 0)
    max_exact = NUM_BUCKETS // 2
    nf = np.maximum(n, 1).astype(np.float32)
    large = max_exact + (np.log(nf / np.float32(max_exact)) / np.float32(math.log(MAX_DISTANCE / max_exact))
                         * np.float32(NUM_BUCKETS - max_exact)).astype(np.int32)
    large = np.minimum(large, NUM_BUCKETS - 1)
    return np.where(n < max_exact, n, large).astype(np.int32)


def _toeplitz_bias(rel_bias, n_rows, n_cols, offset, lo, hi, shift_far):
    d = offset + (n_rows - 1) - np.arange(n_rows + n_cols - 1)
    u = rel_bias[_bucket_np(d)]
    if shift_far:
        u = u - rel_bias[NUM_BUCKETS - 1][None, :]
    u = jnp.where(jnp.asarray((d >= lo) & (d < hi))[:, None], u, NEG).T.astype(F32)
    return jnp.stack([u[:, n_rows - 1 - t:n_rows - 1 - t + n_cols] for t in range(n_rows)], axis=1)


def _gelu_tanh(x):
    return 0.5 * x * (1.0 + jnp.tanh(math.sqrt(2.0 / math.pi) * (x + 0.044715 * (x * x * x))))


def _layer_norm(y, g, b):
    mu = jnp.mean(y, axis=-1, keepdims=True)
    d = y - mu
    var = jnp.mean(d * d, axis=-1, keepdims=True)
    return d * lax.rsqrt(var + LN_EPS) * g + b


def _online_update(carry, s, pv):
    m, l, acc = carry
    m_new = jnp.maximum(m, jnp.max(s, axis=-1, keepdims=True))
    alpha = jnp.exp(m - m_new)
    p = jnp.exp(s - m_new)
    l = alpha * l + jnp.sum(p, axis=-1, keepdims=True)
    acc = alpha * acc + pv(p.astype(BF16))
    return m_new, l, acc


def _pv_t(vt):
    return lambda p: lax.dot_general(p, vt, NT_DIMS, preferred_element_type=F32)


def _flash_step(carry, s, vt_aug):
    m, acc = carry
    m_new = jnp.maximum(m, jnp.max(s, axis=-1, keepdims=True))
    alpha = jnp.exp(m - m_new)
    p = jnp.exp((s - m_new).astype(BF16))
    acc = alpha * acc + lax.dot_general(p, vt_aug, NT_DIMS, preferred_element_type=F32)
    return m_new, acc


def _softmax_pv_t(s, vt):
    m = jnp.max(s, axis=-1, keepdims=True)
    e = jnp.exp(s - m)
    l = jnp.sum(e, axis=-1, keepdims=True)
    return lax.dot_general(e.astype(BF16), vt, NT_DIMS, preferred_element_type=F32) / l


def _flash_init(rows, width):
    return jnp.full((rows, 1), NEG, F32), jnp.zeros((rows, width), F32)


def _flash_out(acc, dv):
    return acc[:, 0:dv] / acc[:, dv:dv + 1]


def _dot_exact_rhs(a, b):
    hi = a.astype(BF16)
    r1 = a - hi.astype(F32)
    mid = r1.astype(BF16)
    lo = (r1 - mid.astype(F32)).astype(BF16)
    return (jnp.dot(hi, b, preferred_element_type=F32) + jnp.dot(mid, b, preferred_element_type=F32)
            + jnp.dot(lo, b, preferred_element_type=F32))


def _masked_softmax(s, valid):
    sm = jnp.where(valid, s, NEG)
    m = jnp.max(sm, axis=-1, keepdims=True)
    e = jnp.where(valid, jnp.exp(sm - m), 0.0)
    z = jnp.sum(e, axis=-1, keepdims=True)
    return e / jnp.where(z > 0, z, 1.0)


def _topn_mask(score, n_sel, axis):
    n = score.shape[axis]
    pos = lax.broadcasted_iota(I32, score.shape, axis).astype(F32)

    def body(_, c):
        sc, sel = c
        m = jnp.max(sc, axis=axis, keepdims=True)
        first = jnp.min(jnp.where(sc == m, pos, float(n)), axis=axis, keepdims=True)
        one = pos == first
        sel = jnp.where(jnp.logical_and(one, m > -BIG), 1.0, sel)
        sc = jnp.where(one, -BIG, sc)
        return sc, sel

    _, sel = lax.fori_loop(0, n_sel, body, (score, jnp.zeros_like(score)))
    return sel


def _block_expand(sel_bf16, first_block, n_keys):
    nb = sel_bf16.shape[-1]
    kb = lax.broadcasted_iota(I32, (nb, n_keys), 1) // SEL_BLOCK + first_block
    e = (kb == lax.broadcasted_iota(I32, (nb, n_keys), 0)).astype(BF16)
    return jnp.dot(sel_bf16, e, preferred_element_type=F32)


def _select_blocks(score, qpos, n_sb, blocks_on_sublanes):
    nl = score.shape[-1]
    blk = lax.broadcasted_iota(I32, (1, nl), 1)
    cur = qpos // SEL_BLOCK
    validb = jnp.logical_and(blk * SEL_BLOCK <= qpos, blk < n_sb)
    forced = jnp.logical_or(blk == 0, jnp.logical_or(blk == cur, blk == cur - 1))
    forced = jnp.logical_and(forced, blk < n_sb)
    sc = jnp.where(forced, BIG, jnp.where(validb, score, -BIG))
    n_sel = min(SEL_TOPN, n_sb)
    if blocks_on_sublanes:
        return _topn_mask(sc.T, n_sel, 0).T
    return _topn_mask(sc, n_sel, 1)


def _page_parts(pages, perm, bd1_ref):
    permuted = [lax.dot_general(perm, p.astype(BF16), NT_DIMS, preferred_element_type=F32) for p in pages]
    outs = []
    for kv in range(2):
        for hp in range(NSA_KV // 2):
            c0 = kv * NSA_KVW + hp * 128
            lhs = jnp.concatenate(
                [jnp.concatenate([pp[s * 8:(s + 1) * 8, c0:c0 + 128] for s in range(CMP_STRIDE)], axis=1)
                 for pp in permuted], axis=0).astype(BF16)
            outs.append(jnp.dot(lhs, bd1_ref[kv], preferred_element_type=F32))
    return jnp.concatenate(outs, axis=1)


def _compress_finalize(p_ref, n, b1_ref, bd2_ref, b2_ref):
    outs = []
    for kv in range(2):
        c0 = kv * 2 * NSA_KVW
        p0 = jnp.concatenate([p_ref[pl.ds(0, n), c0 + 256 * hp:c0 + 256 * hp + 128] for hp in range(2)], axis=1)
        p1 = jnp.concatenate([p_ref[pl.ds(1, n), c0 + 256 * hp + 128:c0 + 256 * hp + 256] for hp in range(2)], axis=1)
        h = _gelu_tanh(p0 + p1 + b1_ref[kv])
        outs.append(jnp.dot(h.astype(BF16), bd2_ref[kv], preferred_element_type=F32) + b2_ref[kv])
    return outs


def _proj_kernel(x_ref, *refs, post):
    n = len(post)
    x = x_ref[...].astype(BF16)
    for w_ref, o_ref, p in zip(refs[:n], refs[n:], post):
        if p == "split128":
            for c in range(w_ref.shape[1] // 128):
                o_ref[:, c, :] = jnp.dot(x, w_ref[:, c * 128:(c + 1) * 128], preferred_element_type=F32)
            continue
        if p == "transposed":
            y = lax.dot_general(w_ref[...], x, NT_DIMS, preferred_element_type=F32)
        else:
            y = jnp.dot(x, w_ref[...], preferred_element_type=F32)
        if p in ("qscale", "q_heads", "q_heads_flag"):
            y = y * 0.125
        elif p == "sigmoid":
            y = jax.nn.sigmoid(y)
        if p in ("q_heads", "q_heads_flag"):
            lane = lax.broadcasted_iota(I32, (y.shape[0], AUG), 1)
            flag = jnp.where(lane == 0, NEG, 0.0).astype(o_ref.dtype)
            for c in range(y.shape[1] // 64):
                piece = y[:, c * 64:(c + 1) * 64].astype(o_ref.dtype)
                o_ref[c] = jnp.concatenate([piece, flag], axis=1) if p == "q_heads_flag" else piece
            continue
        o_ref[...] = y.astype(o_ref.dtype)


def _project(x, weights, post, dtypes, seq=None, rows=None):
    row0, nt = (0, x.shape[0]) if rows is None else rows
    d = x.shape[1]
    tm = _token_tile(nt if seq is None else math.gcd(nt, seq))
    assert row0 % tm == 0
    in_specs = [pl.BlockSpec((tm, d), lambda i, blk0=row0 // tm: (i + blk0, 0))]
    in_specs += [pl.BlockSpec(w.shape, lambda i: (0, 0)) for w in weights]
    out_specs, out_shape = [], []
    for w, p, dt in zip(weights, post, dtypes):
        if p == "transposed":
            per = seq // tm
            out_specs.append(pl.BlockSpec((None, w.shape[0], tm), lambda i, per=per: (i // per, 0, i % per)))
            out_shape.append(jax.ShapeDtypeStruct((nt // seq, w.shape[0], seq), dt))
        elif p == "split128":
            out_specs.append(pl.BlockSpec((tm, w.shape[1] // 128, 128), lambda i: (i, 0, 0)))
            out_shape.append(jax.ShapeDtypeStruct((nt, w.shape[1] // 128, 128), dt))
        elif p in ("q_heads", "q_heads_flag"):
            per = seq // tm
            nh, wd = w.shape[1] // 64, 64 + (AUG if p == "q_heads_flag" else 0)
            out_specs.append(pl.BlockSpec((None, nh, tm, wd), lambda i, per=per: (i // per, 0, i % per, 0)))
            out_shape.append(jax.ShapeDtypeStruct((nt // seq, nh, seq, wd), dt))
        else:
            out_specs.append(pl.BlockSpec((tm, w.shape[1]), lambda i: (i, 0)))
            out_shape.append(jax.ShapeDtypeStruct((nt, w.shape[1]), dt))
    return pl.pallas_call(
        functools.partial(_proj_kernel, post=post),
        grid=(nt // tm,), in_specs=in_specs, out_specs=out_specs, out_shape=out_shape,
        compiler_params=_cparams(("parallel",)), name="proj",
    )(x, *weights)


def _cmp_prompt_kernel(rows_ref, perm_ref, bd1_ref, b1_ref, bd2_ref, b2_ref, ck_ref, cv_ref, p_ref, *, rt, n_chunks):
    i = pl.program_id(1)

    @pl.when(i == 0)
    def _():
        p_ref[pl.ds(n_chunks, 8), :] = jnp.zeros((8, p_ref.shape[1]), F32)

    pages = [rows_ref[0, :, k * PAGE:(k + 1) * PAGE] for k in range(rt // PAGE)]
    nrow = rt // CMP_STRIDE
    p_ref[pl.ds(pl.multiple_of(i * nrow, nrow), nrow), :] = _page_parts(pages, perm_ref[...], bd1_ref)

    @pl.when(i == pl.num_programs(1) - 1)
    def _():
        ck, cv = _compress_finalize(p_ref, n_chunks, b1_ref, bd2_ref, b2_ref)
        ck_ref[0] = ck.astype(ck_ref.dtype)
        cv_ref[0] = cv.astype(cv_ref.dtype)


def _compress_prompt(rows_t, cw):
    b, w, s = rows_t.shape
    rt = min(2048, s)
    n_chunks = s // CMP_STRIDE
    const = lambda a: pl.BlockSpec(a.shape, lambda bi, i, _n=a.ndim: (0,) * _n)
    return pl.pallas_call(
        functools.partial(_cmp_prompt_kernel, rt=rt, n_chunks=n_chunks),
        grid=(b, s // rt),
        in_specs=[pl.BlockSpec((1, w, rt), lambda bi, i: (bi, 0, i)),
                  const(cw["perm"]), const(cw["bd1"]), const(cw["b1"]), const(cw["bd2"]), const(cw["b2"])],
        out_specs=[pl.BlockSpec((1, n_chunks, NSA_KVW), lambda bi, i: (bi, 0, 0))] * 2,
        out_shape=[jax.ShapeDtypeStruct((b, n_chunks, NSA_KVW), BF16)] * 2,
        scratch_shapes=[pltpu.VMEM((n_chunks + 8, 4 * NSA_KVW), F32)],
        compiler_params=_cparams(("parallel", "arbitrary")), name="cmp_prompt",
    )(rows_t, cw["perm"], cw["bd1"], cw["b1"], cw["bd2"], cw["b2"])


def _nsa_prompt_kernel(q_ref, gates_ref, ck_ref, cv_ref, sk_ref, sv_ref, wk_ref, wv_ref, tw_ref, tn_ref, cm_ref,
                       o_ref, *, n_sb):
    g = pl.program_id(1)
    i = pl.program_id(2)
    start = i * QB
    rows = NSA_J * QB
    q2 = q_ref[0].reshape(rows, NSA_QW)
    qpos = start + lax.broadcasted_iota(I32, (QB, 1), 0)

    ck = ck_ref[0, 0]
    nbp = ck.shape[0]
    lc = lax.dot_general(q2, ck, NT_DIMS, preferred_element_type=F32).reshape(NSA_J, QB, nbp)
    c_end = lax.broadcasted_iota(I32, (1, nbp), 1) * CMP_STRIDE + (CMP_BLOCK - 1)
    pc = _masked_softmax(lc, (c_end <= qpos)[None])
    o_cmp = jnp.dot(pc.reshape(rows, nbp).astype(BF16), cv_ref[0, 0], preferred_element_type=F32)
    ps = pc[0] + pc[1] + pc[2] + pc[3]
    score = _dot_exact_rhs(ps, cm_ref[...])
    sel = _select_blocks(score, qpos, n_sb, True)

    blk = lax.broadcasted_iota(I32, (1, sel.shape[1]), 1)
    near0 = 2 * i - 2
    neg_all = jnp.where(sel > 0.5, 0.0, NEG)
    neg_far = jnp.where(blk < near0, neg_all, NEG)
    q_far = jnp.concatenate([jnp.concatenate([neg_far.astype(BF16)] * NSA_J, axis=0), q2], axis=1)
    q_near = jnp.concatenate([jnp.concatenate([neg_all.astype(BF16)] * NSA_J, axis=0), q2], axis=1)
    n_far = (jnp.maximum(i - 1, 0) * QB + KT - 1) // KT

    def far_body(kt, carry):
        off = pl.multiple_of(PADF + kt * KT, KT)
        s = jnp.dot(q_far, sk_ref[0, 0, :, pl.ds(off, KT)], preferred_element_type=F32)
        return _flash_step(carry, s, sv_ref[0, 0, :, pl.ds(off, KT)])

    carry = lax.fori_loop(0, n_far, far_body, _flash_init(rows, NSA_QW))

    offn = pl.multiple_of(PADF + start - QB, QB)
    s = jnp.dot(q_near, sk_ref[0, 0, :, pl.ds(offn, 2 * QB)], preferred_element_type=F32)
    s = (s.reshape(NSA_J, QB, 2 * QB) + tn_ref[...]).reshape(rows, 2 * QB)
    _, acc = _flash_step(carry, s, sv_ref[0, 0, :, pl.ds(offn, 2 * QB)])
    o_slc = _flash_out(acc, NSA_DH)

    lw = WINDOW + QB
    offw = pl.multiple_of(PADF + start - WINDOW, QB)
    s = jnp.dot(q2, wk_ref[0, 0, :, pl.ds(offw, lw)], preferred_element_type=F32)
    s = (s.reshape(NSA_J, QB, lw) + tw_ref[...]).reshape(rows, lw)
    _, acc = _flash_step(_flash_init(rows, NSA_QW), s, wv_ref[0, 0, :, pl.ds(offw, lw)])
    o_win = _flash_out(acc, NSA_DH)

    gt = gates_ref[0]
    colid = lax.broadcasted_iota(I32, gt.shape, 1)
    heads = []
    for j in range(NSA_J):
        out = jnp.zeros((QB, NSA_DH), F32)
        for c, ob in enumerate((o_cmp, o_slc, o_win)):
            gcol = jnp.sum(jnp.where(colid == 3 * (NSA_J * g + j) + c, gt, 0.0), axis=-1, keepdims=True)
            out = out + gcol * ob[j * QB:(j + 1) * QB]
        heads.append(out.astype(o_ref.dtype))
    o_ref[0] = jnp.concatenate(heads, axis=1)


def _nsa_prompt_attention(q4, gates, ck, cv, sk, sv, wk, wv, t_win, t_near, cmat, n_sb):
    b, h, s, qw = q4.shape
    nq = s // QB
    per_bg = lambda a: pl.BlockSpec((1, 1) + a.shape[2:], lambda bi, g, i: (bi, g, 0, 0))
    return pl.pallas_call(
        functools.partial(_nsa_prompt_kernel, n_sb=n_sb),
        grid=(b, NSA_KV, nq),
        in_specs=[pl.BlockSpec((1, NSA_J, QB, qw), lambda bi, g, i: (bi, g, i, 0)),
                  pl.BlockSpec((1, QB, gates.shape[2]), lambda bi, g, i: (bi, i, 0)),
                  per_bg(ck), per_bg(cv), per_bg(sk), per_bg(sv), per_bg(wk), per_bg(wv),
                  pl.BlockSpec((NSA_J, QB, t_win.shape[2]), lambda bi, g, i: (g, 0, 0)),
                  pl.BlockSpec((NSA_J, QB, t_near.shape[2]), lambda bi, g, i: (g, 0, 0)),
                  pl.BlockSpec(cmat.shape, lambda bi, g, i: (0, 0))],
        out_specs=pl.BlockSpec((1, QB, NSA_J * NSA_DH), lambda bi, g, i: (bi, i, g)),
        out_shape=jax.ShapeDtypeStruct((b, s, h * NSA_DH), BF16),
        compiler_params=_cparams(("parallel", "parallel", "arbitrary")), name="nsa_prompt",
    )(q4, gates, ck, cv, sk, sv, wk, wv, t_win, t_near, cmat)


def _diff_lambda(lq1_ref, lk1_ref, lq2_ref, lk2_ref, lam_init):
    a = jnp.sum(lq1_ref[...] * lk1_ref[...], axis=-1, keepdims=True)
    b = jnp.sum(lq2_ref[...] * lk2_ref[...], axis=-1, keepdims=True)
    return jnp.exp(a) - jnp.exp(b) + lam_init


def _diff_prompt_kernel(q_ref, k_ref, v_ref, tn_ref, lq1_ref, lk1_ref, lq2_ref, lk2_ref, o_ref, *, lam_init):
    i = pl.program_id(2)
    start = i * QB
    rows = DIFF_J * QB
    far_end = jnp.maximum(start - QB, 0)
    n_full = far_end // KT_DIFF
    rem_start = n_full * KT_DIFF
    n_edge = (far_end - rem_start + KT - 1) // KT
    q2 = [q_ref[0, 0, m].reshape(rows, DIFF_DH) for m in range(2)]

    def far_step(carry, off, width, keep):
        v = v_ref[0, 0, :, pl.ds(off, width)]
        new = []
        for m in range(2):
            s = jnp.dot(q2[m], k_ref[0, 0, m, :, pl.ds(off, width)], preferred_element_type=F32)
            new.append(_flash_step(carry[m], s if keep is None else jnp.where(keep, s, NEG), v))
        return tuple(new)

    def full_body(kt, carry):
        return far_step(carry, pl.multiple_of(PADF + kt * KT_DIFF, PADF), KT_DIFF, None)

    def edge_body(h, carry):
        base = rem_start + h * KT
        keep = base + lax.broadcasted_iota(I32, (1, KT), 1) < far_end
        return far_step(carry, pl.multiple_of(PADF + base, KT), KT, keep)

    init = _flash_init(rows, DIFF_VW)
    carry = lax.fori_loop(0, n_full, full_body, (init, init))
    carry = lax.fori_loop(0, n_edge, edge_body, carry)
    offn = pl.multiple_of(PADF + start - QB, QB)
    v = v_ref[0, 0, :, pl.ds(offn, 2 * QB)]
    in_seq = start - QB + lax.broadcasted_iota(I32, (1, 2 * QB), 1) >= 0
    outs = []
    for m in range(2):
        s = jnp.dot(q2[m], k_ref[0, 0, m, :, pl.ds(offn, 2 * QB)], preferred_element_type=F32)
        s = (s.reshape(DIFF_J, QB, 2 * QB) + tn_ref[m]).reshape(rows, 2 * QB)
        _, acc = _flash_step(carry[m], jnp.where(in_seq, s, NEG), v)
        outs.append(_flash_out(acc, DIFF_DV))
    lam = _diff_lambda(lq1_ref, lk1_ref, lq2_ref, lk2_ref, lam_init)
    o = outs[0] - lam * outs[1]
    o_ref[0] = jnp.concatenate([o[j * QB:(j + 1) * QB] for j in range(DIFF_J)], axis=1)


def _diff_prompt_attention(q6, k5, v4, t_near, lams, lam_init):
    b, g, _, _, s, dh = q6.shape
    sp = k5.shape[4]
    nq = s // QB
    lam_spec = pl.BlockSpec((1, DIFF_DH), lambda bi, gi, i: (0, 0))
    return pl.pallas_call(
        functools.partial(_diff_prompt_kernel, lam_init=lam_init),
        grid=(b, g, nq),
        in_specs=[pl.BlockSpec((1, 1, 2, DIFF_J, QB, dh), lambda bi, gi, i: (bi, gi, 0, 0, i, 0)),
                  pl.BlockSpec((1, 1, 2, dh, sp), lambda bi, gi, i: (bi, gi, 0, 0, 0)),
                  pl.BlockSpec((1, 1, DIFF_VW, sp), lambda bi, gi, i: (bi, gi, 0, 0)),
                  pl.BlockSpec((None, 2, DIFF_J, QB, 2 * QB), lambda bi, gi, i: (gi, 0, 0, 0, 0)),
                  lam_spec, lam_spec, lam_spec, lam_spec],
        out_specs=pl.BlockSpec((1, QB, DIFF_J * DIFF_DV), lambda bi, gi, i: (bi, i, gi)),
        out_shape=jax.ShapeDtypeStruct((b, s, DIFF_HEADS * DIFF_DV), F32),
        compiler_params=_cparams(("parallel", "parallel", "arbitrary")), name="diff_prompt",
    )(q6, k5, v4, t_near, *lams)


def _nsa_samp_a_kernel(pt_ref, *refs, n_pp, n_pages, past, t_new, n_sb):
    page_refs = refs[:n_pp]
    (xc_ref, qbd_ref, perm_ref, bd1_ref, b1_ref, bd2_ref, b2_ref, cm_ref, ocmp_ref, sel_ref, p_ref) = refs[n_pp:]
    i = pl.program_id(1)
    nrow = n_pp * 8
    parts = _page_parts([r[...] for r in page_refs], perm_ref[...], bd1_ref)
    p_ref[pl.ds(pl.multiple_of(i * nrow, nrow), nrow), :] = parts

    @pl.when(i == pl.num_programs(1) - 1)
    def _():
        xc = xc_ref[0]
        accs = []
        for kv in range(2):
            for hp in range(NSA_KV // 2):
                c0 = kv * NSA_KVW + hp * 128
                lhs = jnp.concatenate([xc[s:s + 1, c0:c0 + 128] for s in range(CMP_STRIDE)], axis=1).astype(BF16)
                accs.append(jnp.dot(lhs, bd1_ref[kv], preferred_element_type=F32))
        n_cached = n_pages * 8
        p_ref[pl.ds(n_cached, 1), :] = jnp.concatenate(accs, axis=1)
        ck, cv = _compress_finalize(p_ref, n_cached, b1_ref, bd2_ref, b2_ref)
        qbd = qbd_ref[0]
        nrows = qbd.shape[0]
        lc = lax.dot_general(qbd, ck.astype(BF16), NT_DIMS, preferred_element_type=F32)
        qpos = past + lax.broadcasted_iota(I32, (nrows, 1), 0) % t_new
        c_end = lax.broadcasted_iota(I32, (1, n_cached), 1) * CMP_STRIDE + (CMP_BLOCK - 1)
        pc = _masked_softmax(lc, c_end <= qpos)
        ocmp_ref[0] = jnp.dot(pc.astype(BF16), cv.astype(BF16), preferred_element_type=F32)
        rg = nrows // NSA_J
        ps = pc[0:rg] + pc[rg:2 * rg] + pc[2 * rg:3 * rg] + pc[3 * rg:4 * rg]
        score = _dot_exact_rhs(ps, cm_ref[...])
        sel_ref[0] = _select_blocks(score, qpos[0:rg], n_sb, False)


def _page_specs(n_pp, page_shape):
    def spec(k):
        return pl.BlockSpec((None,) + page_shape, lambda b, i, pt, k=k: (pt[b, i * n_pp + k], 0, 0))
    return [spec(k) for k in range(n_pp)]


def _nsa_sample_a(page_table, cache, xc, qbd, cw, cmat, past, t_new, n_sb, n_pp):
    bd, n_pages = page_table.shape
    nrows = qbd.shape[1]
    rg = nrows // NSA_J
    nsbp = cmat.shape[1]
    const = lambda a: pl.BlockSpec(a.shape, lambda b, i, pt, _n=a.ndim: (0,) * _n)
    grid_spec = pltpu.PrefetchScalarGridSpec(
        num_scalar_prefetch=1, grid=(bd, n_pages // n_pp),
        in_specs=_page_specs(n_pp, cache.shape[1:]) + [
            pl.BlockSpec((1, CMP_STRIDE, xc.shape[2]), lambda b, i, pt: (b, 0, 0)),
            pl.BlockSpec((1, nrows, NSA_KVW), lambda b, i, pt: (b, 0, 0)),
            const(cw["perm"]), const(cw["bd1"]), const(cw["b1"]), const(cw["bd2"]), const(cw["b2"]), const(cmat)],
        out_specs=[pl.BlockSpec((1, nrows, NSA_KVW), lambda b, i, pt: (b, 0, 0)),
                   pl.BlockSpec((1, rg, nsbp), lambda b, i, pt: (b, 0, 0))],
        scratch_shapes=[pltpu.VMEM((n_pages * 8 + 8, 4 * NSA_KVW), F32)])
    return pl.pallas_call(
        functools.partial(_nsa_samp_a_kernel, n_pp=n_pp, n_pages=n_pages, past=past, t_new=t_new, n_sb=n_sb),
        grid_spec=grid_spec,
        out_shape=[jax.ShapeDtypeStruct((bd, nrows, NSA_KVW), F32), jax.ShapeDtypeStruct((bd, rg, nsbp), F32)],
        compiler_params=_cparams(("parallel", "arbitrary")), name="nsa_sample_a",
    )(page_table, *([cache] * n_pp), xc, qbd, cw["perm"], cw["bd1"], cw["b1"], cw["bd2"], cw["b2"], cmat)


def _nsa_samp_b_kernel(pt_ref, *refs, n_pp, n_pages):
    page_refs = refs[:n_pp]
    (qbd_ref, sel_ref, snew_ref, wst_ref, wnew_ref, ocmp_ref, gate_ref, tsn_ref, tsw_ref, fold_ref,
     o_ref, m_ref, l_ref, acc_ref) = refs[n_pp:]
    i = pl.program_id(1)

    @pl.when(i == 0)
    def _():
        m_ref[...] = jnp.full(m_ref.shape, NEG, F32)
        l_ref[...] = jnp.zeros(l_ref.shape, F32)
        acc_ref[...] = jnp.zeros(acc_ref.shape, F32)

    qbd = qbd_ref[0]
    sel = jnp.concatenate([sel_ref[0]] * NSA_J, axis=0)
    near0 = 2 * (n_pages - 1)
    blk = lax.broadcasted_iota(I32, (1, sel.shape[1]), 1)
    sel_far = jnp.where(blk < near0, sel, 0.0).astype(BF16)
    nk = n_pp * PAGE
    kcat = jnp.concatenate([r[0:NSA_KVW, :] for r in page_refs], axis=1).astype(BF16)
    vcat = jnp.concatenate([r[NSA_KVW:2 * NSA_KVW, :] for r in page_refs], axis=1).astype(BF16)
    s = jnp.dot(qbd, kcat, preferred_element_type=F32)
    keep = _block_expand(sel_far, i * (nk // SEL_BLOCK), nk) > 0.5
    carry = _online_update((m_ref[...], l_ref[...], acc_ref[...]), jnp.where(keep, s, NEG), _pv_t(vcat))
    m_ref[...], l_ref[...], acc_ref[...] = carry

    @pl.when(i == pl.num_programs(1) - 1)
    def _():
        last = page_refs[n_pp - 1]
        snew = snew_ref[0]
        kn = jnp.concatenate([last[0:NSA_KVW, :], snew[0:NSA_KVW, :]], axis=1).astype(BF16)
        vn = jnp.concatenate([last[NSA_KVW:2 * NSA_KVW, :], snew[NSA_KVW:2 * NSA_KVW, :]], axis=1).astype(BF16)
        s = jnp.dot(qbd, kn, preferred_element_type=F32) + tsn_ref[...]
        keep = _block_expand(sel.astype(BF16), near0, 2 * PAGE) > 0.5
        _, l, acc = _online_update((m_ref[...], l_ref[...], acc_ref[...]), jnp.where(keep, s, NEG), _pv_t(vn))
        o_slc = acc / l
        wst = wst_ref[0]
        wnew = wnew_ref[0]
        wk = jnp.concatenate([wst[0:NSA_KVW, :], wnew[0:NSA_KVW, :]], axis=1).astype(BF16)
        wv = jnp.concatenate([wst[NSA_KVW:2 * NSA_KVW, :], wnew[NSA_KVW:2 * NSA_KVW, :]], axis=1).astype(BF16)
        s = jnp.dot(qbd, wk, preferred_element_type=F32) + tsw_ref[...]
        o_win = _softmax_pv_t(s, wv)
        gt = gate_ref[0]
        o_full = gt[:, 0:1] * ocmp_ref[0] + gt[:, 1:2] * o_slc + gt[:, 2:3] * o_win
        nrows = o_full.shape[0]
        t_new = nrows // NSA_HEADS
        row_g = (lax.broadcasted_iota(I32, o_full.shape, 0) // t_new) % NSA_KV
        col_g = lax.broadcasted_iota(I32, o_full.shape, 1) // NSA_DH
        od = jnp.where(row_g == col_g, o_full, 0.0).astype(BF16)
        o_ref[0] = jnp.dot(od, fold_ref[...], preferred_element_type=F32).astype(o_ref.dtype)


def _nsa_sample_b(page_table, cache, qbd, sel, snew, wst, wnew, ocmp, gates, tsn, tsw, fold, n_pp):
    bd, n_pages = page_table.shape
    nrows = qbd.shape[1]
    per_b = lambda a: pl.BlockSpec((1,) + a.shape[1:], lambda b, i, pt, _n=a.ndim: (b,) + (0,) * (_n - 1))
    const = lambda a: pl.BlockSpec(a.shape, lambda b, i, pt, _n=a.ndim: (0,) * _n)
    grid_spec = pltpu.PrefetchScalarGridSpec(
        num_scalar_prefetch=1, grid=(bd, n_pages // n_pp),
        in_specs=_page_specs(n_pp, cache.shape[1:]) + [
            per_b(qbd), per_b(sel), per_b(snew), per_b(wst), per_b(wnew), per_b(ocmp), per_b(gates),
            const(tsn), const(tsw), const(fold)],
        out_specs=pl.BlockSpec((1, nrows, NSA_DH), lambda b, i, pt: (b, 0, 0)),
        scratch_shapes=[pltpu.VMEM((nrows, 1), F32), pltpu.VMEM((nrows, 1), F32), pltpu.VMEM((nrows, NSA_KVW), F32)])
    return pl.pallas_call(
        functools.partial(_nsa_samp_b_kernel, n_pp=n_pp, n_pages=n_pages),
        grid_spec=grid_spec,
        out_shape=jax.ShapeDtypeStruct((bd, nrows, NSA_DH), BF16),
        compiler_params=_cparams(("parallel", "arbitrary")), name="nsa_sample_b",
    )(page_table, *([cache] * n_pp), qbd, sel, snew, wst, wnew, ocmp, gates, tsn, tsw, fold)


def _diff_samp_kernel(pt_ref, *refs, n_pp, lam_init):
    page_refs = refs[:n_pp]
    (qbd_ref, dnew_ref, tdn_ref, lq1_ref, lk1_ref, lq2_ref, lk2_ref, o_ref, m_ref, l_ref, acc_ref) = refs[n_pp:]
    i = pl.program_id(1)
    last_step = i == pl.num_programs(1) - 1
    kw = DIFF_KV * 2 * DIFF_DH

    @pl.when(i == 0)
    def _():
        m_ref[...] = jnp.full(m_ref.shape, NEG, F32)
        l_ref[...] = jnp.zeros(l_ref.shape, F32)
        acc_ref[...] = jnp.zeros(acc_ref.shape, F32)

    qbd = qbd_ref[0]
    rg = qbd.shape[0] // DIFF_KV
    nk = n_pp * PAGE
    groups = range(DIFF_KV)

    def k_rows(ref, g):
        return ref[pl.ds(g, PAGE, stride=2 * DIFF_KV), :]

    def v_rows(ref, g):
        return ref[pl.ds(DIFF_KV + g, PAGE, stride=2 * DIFF_KV), :]

    def scores(k_of):
        return jnp.concatenate([lax.dot_general(qbd[g * rg:(g + 1) * rg], k_of(g), NT_DIMS,
                                                preferred_element_type=F32) for g in groups], axis=0)

    def pv(v_of):
        return lambda p: jnp.concatenate([jnp.dot(p[g * rg:(g + 1) * rg], v_of(g), preferred_element_type=F32)
                                          for g in groups], axis=0)

    s = scores(lambda g: jnp.concatenate([k_rows(r, g) for r in page_refs], axis=0).astype(BF16))
    limit = jnp.where(last_step, nk - PAGE, nk)
    s = jnp.where(lax.broadcasted_iota(I32, (1, nk), 1) < limit, s, NEG)
    carry = _online_update((m_ref[...], l_ref[...], acc_ref[...]), s,
                           pv(lambda g: jnp.concatenate([v_rows(r, g) for r in page_refs], axis=0).astype(BF16)))
    m_ref[...], l_ref[...], acc_ref[...] = carry

    @pl.when(last_step)
    def _():
        last = page_refs[n_pp - 1]
        dnew = dnew_ref[0]
        s = scores(lambda g: jnp.concatenate(
            [k_rows(last, g), dnew[:, g * DIFF_DV:(g + 1) * DIFF_DV]], axis=0).astype(BF16)) + tdn_ref[...]
        _, l, acc = _online_update((m_ref[...], l_ref[...], acc_ref[...]), s, pv(lambda g: jnp.concatenate(
            [v_rows(last, g), dnew[:, kw + g * DIFF_DV:kw + (g + 1) * DIFF_DV]], axis=0).astype(BF16)))
        o_full = acc / l
        lam = _diff_lambda(lq1_ref, lk1_ref, lq2_ref, lk2_ref, lam_init)
        h = rg // 2
        o_ref[0] = jnp.concatenate([o_full[g * rg:g * rg + h] - lam * o_full[g * rg + h:(g + 1) * rg]
                                    for g in groups], axis=0)


def _diff_sample(page_table, cache, qbd, dnew, tdn, lams, lam_init, n_pp):
    bd, n_pages = page_table.shape
    nrows = qbd.shape[1]
    kw = DIFF_KV * 2 * DIFF_DH
    per_b = lambda a: pl.BlockSpec((1,) + a.shape[1:], lambda b, i, pt, _n=a.ndim: (b,) + (0,) * (_n - 1))
    const = lambda a: pl.BlockSpec(a.shape, lambda b, i, pt, _n=a.ndim: (0,) * _n)
    grid_spec = pltpu.PrefetchScalarGridSpec(
        num_scalar_prefetch=1, grid=(bd, n_pages // n_pp),
        in_specs=_page_specs(n_pp, cache.shape[1:]) + [per_b(qbd), per_b(dnew), const(tdn)] + [const(a) for a in lams],
        out_specs=pl.BlockSpec((1, nrows // 2, DIFF_DV), lambda b, i, pt: (b, 0, 0)),
        scratch_shapes=[pltpu.VMEM((nrows, 1), F32), pltpu.VMEM((nrows, 1), F32), pltpu.VMEM((nrows, DIFF_DV), F32)])
    return pl.pallas_call(
        functools.partial(_diff_samp_kernel, n_pp=n_pp, lam_init=lam_init),
        grid_spec=grid_spec,
        out_shape=jax.ShapeDtypeStruct((bd, nrows // 2, DIFF_DV), F32),
        compiler_params=_cparams(("parallel", "arbitrary")), name="diff_sample",
    )(page_table, *([cache] * n_pp), qbd, dnew, tdn, *lams)


def _route(lt):
    mx = jnp.max(lt, axis=0, keepdims=True)
    e = jnp.exp(lt - mx)
    p = e / jnp.sum(e, axis=0, keepdims=True)
    best = None
    grp = None
    for g in range(N_GROUPS):
        r = [p[EPG * g + k:EPG * g + k + 1] for k in range(EPG)]
        a, b = jnp.maximum(r[0], r[1]), jnp.minimum(r[0], r[1])
        c, d = jnp.maximum(r[2], r[3]), jnp.minimum(r[2], r[3])
        sc = jnp.maximum(a, c) + jnp.maximum(jnp.minimum(a, c), jnp.maximum(b, d))
        if g == 0:
            best, grp = sc, jnp.zeros(sc.shape, I32)
        else:
            better = sc > best
            grp = jnp.where(better, g, grp)
            best = jnp.where(better, sc, best)
    lg = []
    for k in range(EPG):
        v = lt[k:k + 1]
        for g in range(1, N_GROUPS):
            v = jnp.where(grp == g, lt[EPG * g + k:EPG * g + k + 1], v)
        lg.append(v)

    def first_max(vals):
        vmax = jnp.maximum(jnp.maximum(vals[0], vals[1]), jnp.maximum(vals[2], vals[3]))
        idx = jnp.where(vals[0] == vmax, 0, jnp.where(vals[1] == vmax, 1, jnp.where(vals[2] == vmax, 2, 3)))
        return vmax, idx

    v1, i1 = first_max(lg)
    rest = [jnp.where(i1 == k, -jnp.inf, lg[k]) for k in range(EPG)]
    v2, i2 = first_max(rest)
    e2 = jnp.exp(v2 - v1)
    den = 1.0 + e2
    return grp * EPG + i1, grp * EPG + i2, 1.0 / den, e2 / den


def _post_attn_kernel(o_ref, x_ref, w_ref, g_ref, b_ref, wr_ref, rb_ref, gain_ref, x1_ref, eidx_ref, gate_ref, *,
                      merge_scale):
    o = o_ref[...]
    if merge_scale is not None:
        segs = []
        for h in range(DIFF_HEADS):
            seg = o[:, h * DIFF_DV:(h + 1) * DIFF_DV]
            ms = jnp.mean(seg * seg, axis=-1, keepdims=True)
            segs.append(seg * lax.rsqrt(ms + LN_EPS) * gain_ref[...] * merge_scale)
        o = jnp.concatenate(segs, axis=1)
    mix = jnp.dot(o.astype(BF16), w_ref[...], preferred_element_type=F32)
    x1 = _layer_norm(DN_ALPHA * x_ref[...] + mix, g_ref[...], b_ref[...])
    x1_ref[...] = x1
    lt = lax.dot_general(wr_ref[...], x1, NT_DIMS, precision=HI, preferred_element_type=F32) + rb_ref[...]
    e1, e2, g1, g2 = _route(lt)
    eidx_ref[0:1, :] = e1
    eidx_ref[1:2, :] = e2
    gate_ref[0:1, :] = g1
    gate_ref[1:2, :] = g2


def _post_attention(o, x, w_out, ln_g, ln_b, wr_t, rb, gain, merge_scale):
    nt, d = x.shape
    tm = _token_tile(nt)
    row = lambda i: (i, 0)
    const = lambda a: pl.BlockSpec(a.shape, lambda i: (0, 0))
    return pl.pallas_call(
        functools.partial(_post_attn_kernel, merge_scale=merge_scale),
        grid=(nt // tm,),
        in_specs=[pl.BlockSpec((tm, d), row), pl.BlockSpec((tm, d), row), const(w_out), const(ln_g), const(ln_b),
                  const(wr_t), const(rb), const(gain)],
        out_specs=[pl.BlockSpec((tm, d), row), pl.BlockSpec((TOP_K, tm), lambda i: (0, i)),
                   pl.BlockSpec((TOP_K, tm), lambda i: (0, i))],
        out_shape=[jax.ShapeDtypeStruct((nt, d), F32), jax.ShapeDtypeStruct((TOP_K, nt), I32),
                   jax.ShapeDtypeStruct((TOP_K, nt), F32)],
        compiler_params=_cparams(("parallel",)), name="post_attn",
    )(o, x, w_out, ln_g, ln_b, wr_t, rb, gain)


def _start_rows(idx_ref, src_hbm, buf, slot, sem, n):
    def body(r0, c):
        for k in range(DMA_UNROLL):
            r = r0 * DMA_UNROLL + k
            pltpu.make_async_copy(src_hbm.at[pl.ds(idx_ref[0, r], 1), :], buf.at[slot, pl.ds(r, 1), :],
                                  sem.at[slot]).start(priority=k % 2)
        return c
    lax.fori_loop(0, n // DMA_UNROLL, body, 0)


def _wait_rows(src_hbm, buf, slot, sem, n):
    pltpu.make_async_copy(src_hbm.at[pl.ds(0, n), :], buf.at[slot], sem.at[slot]).wait()


def _moe_kernel(blk_e_ref, nv_ref, tok0_ref, tokn_ref, x_hbm, wgu_ref, wdn_ref, o_ref, xbuf, sem):
    i = pl.program_id(0)
    nv = nv_ref[0]
    tm = xbuf.shape[1]
    de = wdn_ref.shape[1]

    @pl.when(i == 0)
    def _():
        _start_rows(tok0_ref, x_hbm, xbuf, 0, sem, tm)

    @pl.when(i + 1 < nv)
    def _():
        _start_rows(tokn_ref, x_hbm, xbuf, (i + 1) % 2, sem, tm)

    @pl.when(i < nv)
    def _():
        slot = i % 2
        _wait_rows(x_hbm, xbuf, slot, sem, tm)
        xb = xbuf[slot].astype(BF16)
        acc = jnp.zeros(o_ref.shape, F32)
        half = de // 2
        for c in range(2):
            gate = jnp.dot(xb, wgu_ref[0, :, c * half:(c + 1) * half], preferred_element_type=F32)
            up = jnp.dot(xb, wgu_ref[0, :, de + c * half:de + (c + 1) * half], preferred_element_type=F32)
            act = (gate * jax.nn.sigmoid(gate) * up).astype(BF16)
            acc = acc + jnp.dot(act, wdn_ref[0, c * half:(c + 1) * half, :], preferred_element_type=F32)
        o_ref[...] = acc

    @pl.when(i >= nv)
    def _():
        o_ref[...] = jnp.zeros(o_ref.shape, F32)


def _moe_experts(x1, tok_blocks, blk_e, n_valid, w_gu, w_dn):
    nt, d = x1.shape
    n_blocks, _, tm = tok_blocks.shape
    grid_spec = pltpu.PrefetchScalarGridSpec(
        num_scalar_prefetch=2, grid=(n_blocks,),
        in_specs=[pl.BlockSpec((None, 1, tm), lambda i, be, nv: (i, 0, 0), memory_space=pltpu.SMEM),
                  pl.BlockSpec((None, 1, tm), lambda i, be, nv: (jnp.minimum(i + 1, n_blocks - 1), 0, 0),
                               memory_space=pltpu.SMEM),
                  pl.BlockSpec(memory_space=pl.ANY),
                  pl.BlockSpec((1,) + w_gu.shape[1:], lambda i, be, nv: (be[i], 0, 0)),
                  pl.BlockSpec((1,) + w_dn.shape[1:], lambda i, be, nv: (be[i], 0, 0))],
        out_specs=pl.BlockSpec((tm, d), lambda i, be, nv: (i, 0)),
        scratch_shapes=[pltpu.VMEM((2, tm, d), F32), pltpu.SemaphoreType.DMA((2,))])
    return pl.pallas_call(
        _moe_kernel, grid_spec=grid_spec,
        out_shape=jax.ShapeDtypeStruct((n_blocks * tm, d), F32),
        compiler_params=_cparams(("arbitrary",)), name="moe_experts",
    )(blk_e, n_valid, tok_blocks, tok_blocks, x1, w_gu, w_dn)


def _moe_combine_kernel(pos0_ref, posn_ref, y_hbm, x1_ref, gate_ref, g_ref, b_ref, o_ref, ybuf, sem):
    i = pl.program_id(0)
    n = pl.num_programs(0)
    tm = x1_ref.shape[0]

    @pl.when(i == 0)
    def _():
        _start_rows(pos0_ref, y_hbm, ybuf, 0, sem, TOP_K * tm)

    @pl.when(i + 1 < n)
    def _():
        _start_rows(posn_ref, y_hbm, ybuf, (i + 1) % 2, sem, TOP_K * tm)

    slot = i % 2
    _wait_rows(y_hbm, ybuf, slot, sem, TOP_K * tm)
    gt = gate_ref[...]
    f = gt[:, 0:1] * ybuf[slot, pl.ds(0, tm), :] + gt[:, 1:2] * ybuf[slot, pl.ds(tm, tm), :]
    o_ref[...] = _layer_norm(DN_ALPHA * x1_ref[...] + f, g_ref[...], b_ref[...])


def _moe_combine(y_sorted, pos_blocks, x1, gates_t, ln_g, ln_b):
    nt, d = x1.shape
    n_tiles, _, tm2 = pos_blocks.shape
    tm = tm2 // TOP_K
    const = lambda a: pl.BlockSpec(a.shape, lambda i: (0, 0))
    return pl.pallas_call(
        _moe_combine_kernel, grid=(n_tiles,),
        in_specs=[pl.BlockSpec((None, 1, tm2), lambda i: (i, 0, 0), memory_space=pltpu.SMEM),
                  pl.BlockSpec((None, 1, tm2), lambda i: (jnp.minimum(i + 1, n_tiles - 1), 0, 0),
                               memory_space=pltpu.SMEM),
                  pl.BlockSpec(memory_space=pl.ANY),
                  pl.BlockSpec((tm, d), lambda i: (i, 0)), pl.BlockSpec((tm, TOP_K), lambda i: (i, 0)),
                  const(ln_g), const(ln_b)],
        out_specs=pl.BlockSpec((tm, d), lambda i: (i, 0)),
        out_shape=jax.ShapeDtypeStruct((nt, d), F32),
        scratch_shapes=[pltpu.VMEM((2, tm2, d), F32), pltpu.SemaphoreType.DMA((2,))],
        compiler_params=_cparams(("arbitrary",)), name="moe_combine",
    )(pos_blocks, pos_blocks, y_sorted, x1, gates_t, ln_g, ln_b)


def _moe_layer(x1, eidx, gate, w_gu, w_dn, ln_g, ln_b):
    nt, _ = x1.shape
    a = nt * TOP_K
    flat_e = eidx.T.reshape(a)
    onehot = (flat_e[:, None] == jnp.arange(N_EXPERTS, dtype=I32)[None, :]).astype(I32)
    running = jnp.cumsum(onehot, axis=0)
    sizes = running[-1]
    rank = jnp.sum(onehot * running, axis=1) - 1
    padded = (sizes + MOE_TM - 1) // MOE_TM * MOE_TM
    pad_end = jnp.cumsum(padded)
    pad_start = pad_end - padded
    pos = (jnp.sum(onehot * pad_start[None, :], axis=1) + rank).astype(I32)
    n_blocks = -(-a // MOE_TM) + N_EXPERTS
    tok_pad = jnp.zeros((n_blocks * MOE_TM,), I32).at[pos].set(jnp.arange(a, dtype=I32) // TOP_K)
    blk_start = jnp.arange(n_blocks, dtype=I32)[:, None] * MOE_TM
    blk_e = jnp.minimum(jnp.sum((pad_end[None, :] <= blk_start).astype(I32), axis=1), N_EXPERTS - 1).astype(I32)
    n_valid = (pad_end[-1:] // MOE_TM).astype(I32)
    y_sorted = _moe_experts(x1, tok_pad.reshape(n_blocks, 1, MOE_TM), blk_e, n_valid, w_gu, w_dn)
    tm = _token_tile(nt)
    pos_blocks = pos.reshape(nt // tm, tm, TOP_K).transpose(0, 2, 1).reshape(nt // tm, 1, TOP_K * tm)
    return _moe_combine(y_sorted, pos_blocks, x1, gate.T, ln_g, ln_b)


def _compress_weights(cmp_k, cmp_v):
    eye = jnp.eye(NSA_KV, dtype=F32)
    eye2 = jnp.eye(2, dtype=F32)
    bd1, b1, bd2, b2 = [], [], [], []
    for (w1, bias1, w2, bias2) in (cmp_k, cmp_v):
        w1r = w1.reshape(CMP_SPAN, CMP_STRIDE, NSA_DH, NSA_DH)
        bd1.append(jnp.einsum("rsdh,ab->sadrbh", w1r, eye2).reshape(CMP_STRIDE * 2 * NSA_DH, CMP_SPAN * 2 * NSA_DH))
        bd2.append(jnp.einsum("hd,ab->ahbd", w2, eye).reshape(NSA_KVW, NSA_KVW))
        b1.append(jnp.tile(bias1, NSA_KV)[None])
        b2.append(jnp.tile(bias2, NSA_KV)[None])
    r = np.arange(PAGE)
    perm = np.zeros((PAGE, PAGE), np.float32)
    perm[r, CMP_STRIDE * (r % 8) + r // 8] = 1.0
    return {"perm": jnp.asarray(perm, BF16), "bd1": jnp.stack(bd1).astype(BF16), "b1": jnp.stack(b1),
            "bd2": jnp.stack(bd2).astype(BF16), "b2": jnp.stack(b2)}


def _chunk_score_matrix(n_blocks_in, n_sb):
    lanes = -(-n_sb // 128) * 128
    n = np.arange(n_blocks_in)[:, None]
    b = np.arange(lanes)[None, :]
    m = ((n >= 4 * b) & (n <= 4 * b + 3)).astype(np.float32) + ((n + 1 >= 4 * b) & (n + 1 <= 4 * b + 3)).astype(np.float32)
    m = np.where(b < n_sb, m, 0.0)
    return jnp.asarray(m, BF16)


def kernel(x_prompt, x_sample, cache_cmp_kv, cache_slc_kv, state_win_kv, cache_diff_kv, page_table,
           nsa_w_in, nsa_w_out, cmp_k_w1, cmp_k_b1, cmp_k_w2, cmp_k_b2, cmp_v_w1, cmp_v_b1, cmp_v_w2, cmp_v_b2,
           diff_w_in, diff_w_out, lambda_q1, lambda_k1, lambda_q2, lambda_k2, diff_subln_gain, rel_bias,
           ln_gain, ln_bias, router_w, router_b, moe_w_gate_up, moe_w_down):
    b, s, d = x_prompt.shape
    bd, t_new, _ = x_sample.shape
    n_pages = page_table.shape[1]
    past = n_pages * PAGE
    assert s % KT == 0 and s >= WINDOW and past >= WINDOW and state_win_kv.shape[1] == WINDOW
    assert t_new <= CMP_STRIDE and (NSA_HEADS * t_new) % 8 == 0 and (DIFF_J * t_new) % 8 == 0
    n_p = b * s
    n_s = bd * t_new
    x_p = x_prompt.reshape(n_p, d)
    x_s = x_sample.reshape(n_s, d)

    hq = NSA_HEADS * NSA_DH
    wr_t = router_w.T
    rb = router_b[:, None]
    w_gu = moe_w_gate_up.astype(BF16)
    w_dn = moe_w_down.astype(BF16)
    t640 = _toeplitz_bias(rel_bias, QB, WINDOW + QB, WINDOW, 0, WINDOW, False)
    t_near = _toeplitz_bias(rel_bias, QB, 2 * QB, QB, 0, 1 << 30, True)
    n_pp = 16 if n_pages % 16 == 0 else 8
    assert n_pages % n_pp == 0

    w = nsa_w_in.astype(BF16)
    wg = jnp.pad(w[:, hq + 6 * NSA_KVW:], ((0, 0), (0, 128 - 3 * NSA_HEADS)))
    kv_w = [w[:, hq + 2 * k * NSA_KVW: hq + 2 * (k + 1) * NSA_KVW] for k in range(3)]
    f5 = (BF16, F32, F32, F32, F32)
    q4, cmp_t, slc_t, win_t, gates_p = _project(
        x_p, [w[:, :hq]] + [a.T for a in kv_w] + [wg],
        ("q_heads_flag", "transposed", "transposed", "transposed", "sigmoid"), f5, s)
    q_s, cmp_s, slc_s, win_s, gates_s = _project(x_s, [w[:, :hq]] + kv_w + [wg],
                                                 ("qscale", None, None, None, "sigmoid"), f5)
    cw = _compress_weights((cmp_k_w1, cmp_k_b1, cmp_k_w2, cmp_k_b2), (cmp_v_w1, cmp_v_b1, cmp_v_w2, cmp_v_b2))

    kvw = 2 * NSA_KVW
    ck, cv = _compress_prompt(cmp_t, cw)

    def per_group(a):
        return a.reshape(b, a.shape[1], NSA_KV, NSA_DH).transpose(0, 2, 1, 3)

    n_sb_p = s // SEL_BLOCK
    sp = s + 2 * PADF
    pos = np.arange(sp) - PADF
    inside = (pos >= 0) & (pos < s)
    blk_rows = -(-n_sb_p // 128) * 128
    onehot = (pos[None, :] // SEL_BLOCK == np.arange(blk_rows)[:, None]) & inside[None, :]
    flag_rows = np.zeros((AUG, sp), np.float32)
    flag_rows[0] = ~inside
    ones_rows = np.zeros((AUG, sp), np.float32)
    ones_rows[0] = 1.0

    def per_bg(a):
        return jnp.broadcast_to(jnp.asarray(a, BF16), (b, NSA_KV) + a.shape)

    def kv_t(rows_t):
        r = jnp.pad(rows_t.astype(BF16), ((0, 0), (0, 0), (PADF, PADF))).reshape(b, 2, NSA_KV, NSA_DH, sp)
        return r[:, 0], r[:, 1]

    skt, svt = kv_t(slc_t)
    wkt, wvt = kv_t(win_t)
    sk = jnp.concatenate([per_bg(onehot), skt, per_bg(flag_rows)], axis=2)
    sv = jnp.concatenate([svt, per_bg(ones_rows)], axis=2)
    wk = jnp.concatenate([wkt, per_bg(flag_rows)], axis=2)
    wv = jnp.concatenate([wvt, per_bg(ones_rows)], axis=2)
    ck4 = jnp.pad(per_group(ck), ((0, 0), (0, 0), (0, 0), (0, AUG)))
    o4 = _nsa_prompt_attention(q4, gates_p.reshape(b, s, 128), ck4, per_group(cv), sk, sv, wk, wv,
                               t640, t_near, _chunk_score_matrix(s // CMP_STRIDE, n_sb_p), n_sb_p)
    o_p = o4.reshape(n_p, hq)

    nrows = NSA_HEADS * t_new
    n_sb_s = -(-(past + t_new) // SEL_BLOCK)
    qs = q_s.reshape(bd, t_new, NSA_KV, NSA_J, NSA_DH).transpose(0, 3, 2, 1, 4)
    qbd = jnp.einsum("bjgtd,gh->bjgthd", qs, jnp.eye(NSA_KV, dtype=BF16)).reshape(bd, nrows, NSA_KVW)

    def native_t(a):
        return a.transpose(0, 2, 3, 4, 1).reshape(a.shape[0], kvw, a.shape[1])

    def new_rows_t(rows):
        return jnp.pad(rows.reshape(bd, t_new, kvw).transpose(0, 2, 1), ((0, 0), (0, 0), (0, PAGE - t_new)))

    xc = jnp.pad(cmp_s.reshape(bd, t_new, kvw), ((0, 0), (0, CMP_STRIDE - t_new), (0, 0)))
    ocmp, sel = _nsa_sample_a(page_table, native_t(cache_cmp_kv), xc, qbd, cw,
                              _chunk_score_matrix(n_pages * 8, n_sb_s), past, t_new, n_sb_s, n_pp)
    head_of_row = (np.arange(nrows) // t_new % NSA_KV) * NSA_J + np.arange(nrows) // (t_new * NSA_KV)
    tok_of_row = np.arange(nrows) % t_new
    tsn = _toeplitz_bias(rel_bias, t_new, 2 * PAGE, PAGE, 0, 1 << 30, True)[head_of_row, tok_of_row]
    tsw_full = _toeplitz_bias(rel_bias, t_new, WINDOW + PAGE, WINDOW, 0, WINDOW, False)
    tsw_full = jnp.where(jnp.arange(WINDOW + PAGE)[None, None, :] < WINDOW + t_new, tsw_full, NEG)
    tsw = tsw_full[head_of_row, tok_of_row]
    gs = gates_s[:, :3 * NSA_HEADS].reshape(bd, t_new, NSA_KV, NSA_J, 3).transpose(0, 3, 2, 1, 4).reshape(bd, nrows, 3)
    fold = jnp.asarray(np.tile(np.eye(NSA_DH, dtype=np.float32), (NSA_KV, 1)), BF16)
    o_s = _nsa_sample_b(page_table, native_t(cache_slc_kv), qbd, sel, new_rows_t(slc_s), native_t(state_win_kv),
                        new_rows_t(win_s), ocmp, gs, tsn, tsw, fold, n_pp)
    o_s = o_s.reshape(bd, NSA_J, NSA_KV, t_new, NSA_DH).transpose(0, 3, 2, 1, 4).reshape(n_s, hq)

    ones = jnp.ones((1, DIFF_DV), F32)
    x_all = jnp.concatenate([x_p, x_s], axis=0)
    x1, eidx, gate = _post_attention(jnp.concatenate([o_p, o_s], axis=0), x_all, nsa_w_out.astype(BF16),
                                     ln_gain[0, 0][None], ln_bias[0, 0][None], wr_t, rb, ones, None)
    x_all = _moe_layer(x1, eidx, gate, w_gu[0], w_dn[0], ln_gain[0, 1][None], ln_bias[0, 1][None])

    lam_init = 0.8 - 0.6 * math.exp(-0.3 * 1)
    lams = [a[None] for a in (lambda_q1, lambda_k1, lambda_q2, lambda_k2)]
    wd = diff_w_in.astype(BF16)
    dq = DIFF_HEADS * 2 * DIFF_DH
    dkw = DIFF_KV * DIFF_DV
    wq_gmj = wd[:, :dq].reshape(d, DIFF_KV, DIFF_J, 2, DIFF_DH).transpose(0, 1, 3, 2, 4).reshape(d, dq)
    q6, diff_p, diff_t = _project(x_all, [wq_gmj, wd[:, dq:], wd[:, dq:].T],
                                  ("q_heads", "split128", "transposed"), (BF16, F32, BF16), s, (0, n_p))
    q6 = q6.reshape(b, DIFF_KV, 2, DIFF_J, s, DIFF_DH)
    qd_s, diff_s = _project(x_all, [wd[:, :dq], wd[:, dq:]], ("qscale", "split128"), (BF16, F32), None, (n_p, n_s))
    tn_d = t_near.reshape(DIFF_KV, DIFF_J, 2, QB, 2 * QB).transpose(0, 2, 1, 3, 4)
    spd = s + PADF + KT_DIFF
    dt = jnp.pad(diff_t, ((0, 0), (0, 0), (PADF, KT_DIFF)))
    k5 = dt[:, :dkw].reshape(b, DIFF_KV, 2, DIFF_DH, spd)
    ones_d = np.zeros((AUG, spd), np.float32)
    ones_d[0] = 1.0
    ones_d = jnp.broadcast_to(jnp.asarray(ones_d, BF16), (b, DIFF_KV, AUG, spd))
    v4 = jnp.concatenate([dt[:, dkw:].reshape(b, DIFF_KV, DIFF_DV, spd), ones_d], axis=2)
    o_p = _diff_prompt_attention(q6, k5, v4, tn_d, lams, lam_init).reshape(n_p, DIFF_HEADS * DIFF_DV)

    nrows_d = 2 * DIFF_HEADS * t_new
    qsd = qd_s.reshape(bd, t_new, DIFF_KV, DIFF_J, 2, DIFF_DH).transpose(0, 2, 4, 3, 1, 5)
    qbd_d = jnp.einsum("bgmjtd,mM->bgmjtMd", qsd, jnp.eye(2, dtype=BF16)).reshape(bd, nrows_d, 2 * DIFF_DH)
    r = np.arange(nrows_d)
    col_of_row = (r // (2 * DIFF_J * t_new)) * 4 + (r // t_new % DIFF_J) * 2 + r // (DIFF_J * t_new) % 2
    tdn_full = _toeplitz_bias(rel_bias, t_new, 2 * PAGE, PAGE, 0, 1 << 30, True)
    tdn_full = jnp.where(jnp.arange(2 * PAGE)[None, None, :] < PAGE + t_new, tdn_full, NEG)
    tdn = tdn_full[col_of_row, r % t_new]
    dnew = jnp.pad(diff_s.reshape(bd, t_new, 2 * dkw), ((0, 0), (0, PAGE - t_new), (0, 0)))
    cache_d = cache_diff_kv.reshape(-1, PAGE * 2 * DIFF_KV, DIFF_DV)
    o_s = _diff_sample(page_table, cache_d, qbd_d, dnew, tdn, lams, lam_init, n_pp)
    o_s = o_s.reshape(bd, DIFF_KV, DIFF_J, t_new, DIFF_DV).transpose(0, 3, 1, 2, 4).reshape(n_s, DIFF_HEADS * DIFF_DV)

    x1, eidx, gate = _post_attention(jnp.concatenate([o_p, o_s], axis=0), x_all, diff_w_out.astype(BF16),
                                     ln_gain[1, 0][None], ln_bias[1, 0][None], wr_t, rb, diff_subln_gain[None],
                                     1.0 - lam_init)
    x_all = _moe_layer(x1, eidx, gate, w_gu[1], w_dn[1], ln_gain[1, 1][None], ln_bias[1, 1][None])

    kv5 = (2, NSA_KV, NSA_DH)

    def rows_out(a_t):
        return a_t.reshape((b,) + kv5 + (a_t.shape[2],)).transpose(0, 4, 1, 2, 3)

    win_out_s = jnp.concatenate([state_win_kv, win_s.reshape((bd, t_new) + kv5)], axis=1)[:, t_new:]
    dshape = (2, DIFF_KV, DIFF_DV)
    return (x_all[:n_p].reshape(b, s, d), x_all[n_p:].reshape(bd, t_new, d),
            rows_out(cmp_t), cmp_s.reshape((bd, t_new) + kv5), rows_out(slc_t), slc_s.reshape((bd, t_new) + kv5),
            rows_out(win_t[:, :, s - WINDOW:]), win_out_s,
            diff_p.reshape((b, s) + dshape), diff_s.reshape((bd, t_new) + dshape))
```

```python
import functools
import math

import jax
import jax.numpy as jnp
import numpy as np
from jax import lax
from jax.experimental import pallas as pl
from jax.experimental.pallas import tpu as pltpu

F32 = jnp.float32
BF16 = jnp.bfloat16
I32 = jnp.int32
HI = lax.Precision.HIGHEST
NT_DIMS = (((1,), (1,)), ((), ()))

NSA_HEADS = 16
NSA_KV = 4
NSA_J = NSA_HEADS // NSA_KV
NSA_DH = 64
NSA_KVW = NSA_KV * NSA_DH
CMP_BLOCK = 32
CMP_STRIDE = 16
CMP_SPAN = CMP_BLOCK // CMP_STRIDE
SEL_BLOCK = 64
SEL_CHUNKS = SEL_BLOCK // CMP_STRIDE
SEL_TOPN = 16
WINDOW = 512
QB = 128
DIFF_HEADS = 8
DIFF_KV = 4
DIFF_J = DIFF_HEADS // DIFF_KV
DIFF_DH = 64
DIFF_DV = 2 * DIFF_DH
NUM_BUCKETS = 32
MAX_DISTANCE = 128
N_EXPERTS = 16
N_GROUPS = 4
EPG = N_EXPERTS // N_GROUPS
TOP_K = 2
DEPTH = 2
DN_ALPHA = (2 * DEPTH) ** 0.25
LN_EPS = 1e-5

PAGE = 128
KT = 512
KT_DIFF = 1024
PADF = 512
AUG = 16
NSA_QW = NSA_DH + AUG
DIFF_VW = DIFF_DV + AUG
MOE_TM = 512
DMA_UNROLL = 8
NEG = -1e30
BIG = 1e30
VMEM_LIMIT = 56 * 1024 * 1024


def _cparams(sem):
    return pltpu.CompilerParams(dimension_semantics=sem, vmem_limit_bytes=VMEM_LIMIT)


def _token_tile(nt):
    for tm in (512, 384, 256, 128):
        if nt % tm == 0:
            return tm
    raise ValueError(f"token count {nt} is not a multiple of 128")


def _bucket_np(dist):
    n = np.maximum(dist, 0)
    max_exact = NUM_BUCKETS // 2
    nf = np.maximum(n, 1).astype(np.float32)
    large = max_exact + (np.log(nf / np.float32(max_exact)) / np.float32(math.log(MAX_DISTANCE / max_exact))
                         * np.float32(NUM_BUCKETS - max_exact)).astype(np.int32)
    large = np.minimum(large, NUM_BUCKETS - 1)
    return np.where(n < max_exact, n, large).astype(np.int32)


def _toeplitz_bias(rel_bias, n_rows, n_cols, offset, lo, hi, shift_far):
    d = offset + (n_rows - 1) - np.arange(n_rows + n_cols - 1)
    u = rel_bias[_bucket_np(d)]
    if shift_far:
        u = u - rel_bias[NUM_BUCKETS - 1][None, :]
    u = jnp.where(jnp.asarray((d >= lo) & (d < hi))[:, None], u, NEG).T.astype(F32)
    return jnp.stack([u[:, n_rows - 1 - t:n_rows - 1 - t + n_cols] for t in range(n_rows)], axis=1)


def _gelu_tanh(x):
    return 0.5 * x * (1.0 + jnp.tanh(math.sqrt(2.0 / math.pi) * (x + 0.044715 * (x * x * x))))


def _layer_norm(y, g, b):
    mu = jnp.mean(y, axis=-1, keepdims=True)
    d = y - mu
    var = jnp.mean(d * d, axis=-1, keepdims=True)
    return d * lax.rsqrt(var + LN_EPS) * g + b


def _online_update(carry, s, pv):
    m, l, acc = carry
    m_new = jnp.maximum(m, jnp.max(s, axis=-1, keepdims=True))
    alpha = jnp.exp(m - m_new)
    p = jnp.exp(s - m_new)
    l = alpha * l + jnp.sum(p, axis=-1, keepdims=True)
    acc = alpha * acc + pv(p.astype(BF16))
    return m_new, l, acc


def _pv_t(vt):
    return lambda p: lax.dot_general(p, vt, NT_DIMS, preferred_element_type=F32)


def _flash_step(carry, s, vt_aug):
    m, acc = carry
    m_new = jnp.maximum(m, jnp.max(s, axis=-1, keepdims=True))
    alpha = jnp.exp(m - m_new)
    p = jnp.exp((s - m_new).astype(BF16))
    acc = alpha * acc + lax.dot_general(p, vt_aug, NT_DIMS, preferred_element_type=F32)
    return m_new, acc


def _softmax_pv_t(s, vt):
    m = jnp.max(s, axis=-1, keepdims=True)
    e = jnp.exp(s - m)
    l = jnp.sum(e, axis=-1, keepdims=True)
    return lax.dot_general(e.astype(BF16), vt, NT_DIMS, preferred_element_type=F32) / l


def _flash_init(rows, width):
    return jnp.full((rows, 1), NEG, F32), jnp.zeros((rows, width), F32)


def _flash_out(acc, dv):
    return acc[:, 0:dv] / acc[:, dv:dv + 1]


def _dot_exact_rhs(a, b):
    hi = a.astype(BF16)
    r1 = a - hi.astype(F32)
    mid = r1.astype(BF16)
    lo = (r1 - mid.astype(F32)).astype(BF16)
    return (jnp.dot(hi, b, preferred_element_type=F32) + jnp.dot(mid, b, preferred_element_type=F32)
            + jnp.dot(lo, b, preferred_element_type=F32))


def _masked_softmax(s, valid):
    sm = jnp.where(valid, s, NEG)
    m = jnp.max(sm, axis=-1, keepdims=True)
    e = jnp.where(valid, jnp.exp(sm - m), 0.0)
    z = jnp.sum(e, axis=-1, keepdims=True)
    return e / jnp.where(z > 0, z, 1.0)


def _topn_mask(score, n_sel, axis):
    n = score.shape[axis]
    pos = lax.broadcasted_iota(I32, score.shape, axis).astype(F32)

    def body(_, c):
        sc, sel = c
        m = jnp.max(sc, axis=axis, keepdims=True)
        first = jnp.min(jnp.where(sc == m, pos, float(n)), axis=axis, keepdims=True)
        one = pos == first
        sel = jnp.where(jnp.logical_and(one, m > -BIG), 1.0, sel)
        sc = jnp.where(one, -BIG, sc)
        return sc, sel

    _, sel = lax.fori_loop(0, n_sel, body, (score, jnp.zeros_like(score)))
    return sel


def _block_expand(sel_bf16, first_block, n_keys):
    nb = sel_bf16.shape[-1]
    kb = lax.broadcasted_iota(I32, (nb, n_keys), 1) // SEL_BLOCK + first_block
    e = (kb == lax.broadcasted_iota(I32, (nb, n_keys), 0)).astype(BF16)
    return jnp.dot(sel_bf16, e, preferred_element_type=F32)


def _select_blocks(score, qpos, n_sb, blocks_on_sublanes):
    nl = score.shape[-1]
    blk = lax.broadcasted_iota(I32, (1, nl), 1)
    cur = qpos // SEL_BLOCK
    validb = jnp.logical_and(blk * SEL_BLOCK <= qpos, blk < n_sb)
    forced = jnp.logical_or(blk == 0, jnp.logical_or(blk == cur, blk == cur - 1))
    forced = jnp.logical_and(forced, blk < n_sb)
    sc = jnp.where(forced, BIG, jnp.where(validb, score, -BIG))
    n_sel = min(SEL_TOPN, n_sb)
    if blocks_on_sublanes:
        return _topn_mask(sc.T, n_sel, 0).T
    return _topn_mask(sc, n_sel, 1)


def _page_parts(pages, perm, bd1_ref):
    permuted = [lax.dot_general(perm, p.astype(BF16), NT_DIMS, preferred_element_type=F32) for p in pages]
    outs = []
    for kv in range(2):
        for hp in range(NSA_KV // 2):
            c0 = kv * NSA_KVW + hp * 128
            lhs = jnp.concatenate(
                [jnp.concatenate([pp[s * 8:(s + 1) * 8, c0:c0 + 128] for s in range(CMP_STRIDE)], axis=1)
                 for pp in permuted], axis=0).astype(BF16)
            outs.append(jnp.dot(lhs, bd1_ref[kv], preferred_element_type=F32))
    return jnp.concatenate(outs, axis=1)


def _compress_finalize(p_ref, n, b1_ref, bd2_ref, b2_ref):
    outs = []
    for kv in range(2):
        c0 = kv * 2 * NSA_KVW
        p0 = jnp.concatenate([p_ref[pl.ds(0, n), c0 + 256 * hp:c0 + 256 * hp + 128] for hp in range(2)], axis=1)
        p1 = jnp.concatenate([p_ref[pl.ds(1, n), c0 + 256 * hp + 128:c0 + 256 * hp + 256] for hp in range(2)], axis=1)
        h = _gelu_tanh(p0 + p1 + b1_ref[kv])
        outs.append(jnp.dot(h.astype(BF16), bd2_ref[kv], preferred_element_type=F32) + b2_ref[kv])
    return outs


def _proj_kernel(x_ref, *refs, post):
    n = len(post)
    x = x_ref[...].astype(BF16)
    for w_ref, o_ref, p in zip(refs[:n], refs[n:], post):
        if p == "split128":
            for c in range(w_ref.shape[1] // 128):
                o_ref[:, c, :] = jnp.dot(x, w_ref[:, c * 128:(c + 1) * 128], preferred_element_type=F32)
            continue
        if p == "transposed":
            y = lax.dot_general(w_ref[...], x, NT_DIMS, preferred_element_type=F32)
        else:
            y = jnp.dot(x, w_ref[...], preferred_element_type=F32)
        if p in ("qscale", "q_heads", "q_heads_flag"):
            y = y * 0.125
        elif p == "sigmoid":
            y = jax.nn.sigmoid(y)
        if p in ("q_heads", "q_heads_flag"):
            lane = lax.broadcasted_iota(I32, (y.shape[0], AUG), 1)
            flag = jnp.where(lane == 0, NEG, 0.0).astype(o_ref.dtype)
            for c in range(y.shape[1] // 64):
                piece = y[:, c * 64:(c + 1) * 64].astype(o_ref.dtype)
                o_ref[c] = jnp.concatenate([piece, flag], axis=1) if p == "q_heads_flag" else piece
            continue
        o_ref[...] = y.astype(o_ref.dtype)


def _project(x, weights, post, dtypes, seq=None, rows=None):
    row0, nt = (0, x.shape[0]) if rows is None else rows
    d = x.shape[1]
    tm = _token_tile(nt if seq is None else math.gcd(nt, seq))
    assert row0 % tm == 0
    in_specs = [pl.BlockSpec((tm, d), lambda i, blk0=row0 // tm: (i + blk0, 0))]
    in_specs += [pl.BlockSpec(w.shape, lambda i: (0, 0)) for w in weights]
    out_specs, out_shape = [], []
    for w, p, dt in zip(weights, post, dtypes):
        if p == "transposed":
            per = seq // tm
            out_specs.append(pl.BlockSpec((None, w.shape[0], tm), lambda i, per=per: (i // per, 0, i % per)))
            out_shape.append(jax.ShapeDtypeStruct((nt // seq, w.shape[0], seq), dt))
        elif p == "split128":
            out_specs.append(pl.BlockSpec((tm, w.shape[1] // 128, 128), lambda i: (i, 0, 0)))
            out_shape.append(jax.ShapeDtypeStruct((nt, w.shape[1] // 128, 128), dt))
        elif p in ("q_heads", "q_heads_flag"):
            per = seq // tm
            nh, wd = w.shape[1] // 64, 64 + (AUG if p == "q_heads_flag" else 0)
            out_specs.append(pl.BlockSpec((None, nh, tm, wd), lambda i, per=per: (i // per, 0, i % per, 0)))
            out_shape.append(jax.ShapeDtypeStruct((nt // seq, nh, seq, wd), dt))
        else:
            out_specs.append(pl.BlockSpec((tm, w.shape[1]), lambda i: (i, 0)))
            out_shape.append(jax.ShapeDtypeStruct((nt, w.shape[1]), dt))
    return pl.pallas_call(
        functools.partial(_proj_kernel, post=post),
        grid=(nt // tm,), in_specs=in_specs, out_specs=out_specs, out_shape=out_shape,
        compiler_params=_cparams(("parallel",)), name="proj",
    )(x, *weights)


def _cmp_prompt_kernel(rows_ref, perm_ref, bd1_ref, b1_ref, bd2_ref, b2_ref, ck_ref, cv_ref, p_ref, *, rt, n_chunks):
    i = pl.program_id(1)

    @pl.when(i == 0)
    def _():
        p_ref[pl.ds(n_chunks, 8), :] = jnp.zeros((8, p_ref.shape[1]), F32)

    pages = [rows_ref[0, :, k * PAGE:(k + 1) * PAGE] for k in range(rt // PAGE)]
    nrow = rt // CMP_STRIDE
    p_ref[pl.ds(pl.multiple_of(i * nrow, nrow), nrow), :] = _page_parts(pages, perm_ref[...], bd1_ref)

    @pl.when(i == pl.num_programs(1) - 1)
    def _():
        ck, cv = _compress_finalize(p_ref, n_chunks, b1_ref, bd2_ref, b2_ref)
        ck_ref[0] = ck.astype(ck_ref.dtype)
        cv_ref[0] = cv.astype(cv_ref.dtype)


def _compress_prompt(rows_t, cw):
    b, w, s = rows_t.shape
    rt = min(2048, s)
    n_chunks = s // CMP_STRIDE
    const = lambda a: pl.BlockSpec(a.shape, lambda bi, i, _n=a.ndim: (0,) * _n)
    return pl.pallas_call(
        functools.partial(_cmp_prompt_kernel, rt=rt, n_chunks=n_chunks),
        grid=(b, s // rt),
        in_specs=[pl.BlockSpec((1, w, rt), lambda bi, i: (bi, 0, i)),
                  const(cw["perm"]), const(cw["bd1"]), const(cw["b1"]), const(cw["bd2"]), const(cw["b2"])],
        out_specs=[pl.BlockSpec((1, n_chunks, NSA_KVW), lambda bi, i: (bi, 0, 0))] * 2,
        out_shape=[jax.ShapeDtypeStruct((b, n_chunks, NSA_KVW), BF16)] * 2,
        scratch_shapes=[pltpu.VMEM((n_chunks + 8, 4 * NSA_KVW), F32)],
        compiler_params=_cparams(("parallel", "arbitrary")), name="cmp_prompt",
    )(rows_t, cw["perm"], cw["bd1"], cw["b1"], cw["bd2"], cw["b2"])


def _nsa_prompt_kernel(q_ref, gates_ref, ck_ref, cv_ref, sk_ref, sv_ref, wk_ref, wv_ref, tw_ref, tn_ref, cm_ref,
                       o_ref, *, n_sb):
    g = pl.program_id(1)
    i = pl.program_id(2)
    start = i * QB
    rows = NSA_J * QB
    q2 = q_ref[0].reshape(rows, NSA_QW)
    qpos = start + lax.broadcasted_iota(I32, (QB, 1), 0)

    ck = ck_ref[0, 0]
    nbp = ck.shape[0]
    lc = lax.dot_general(q2, ck, NT_DIMS, preferred_element_type=F32).reshape(NSA_J, QB, nbp)
    c_end = lax.broadcasted_iota(I32, (1, nbp), 1) * CMP_STRIDE + (CMP_BLOCK - 1)
    pc = _masked_softmax(lc, (c_end <= qpos)[None])
    o_cmp = jnp.dot(pc.reshape(rows, nbp).astype(BF16), cv_ref[0, 0], preferred_element_type=F32)
    ps = pc[0] + pc[1] + pc[2] + pc[3]
    score = _dot_exact_rhs(ps, cm_ref[...])
    sel = _select_blocks(score, qpos, n_sb, True)

    blk = lax.broadcasted_iota(I32, (1, sel.shape[1]), 1)
    near0 = 2 * i - 2
    neg_all = jnp.where(sel > 0.5, 0.0, NEG)
    neg_far = jnp.where(blk < near0, neg_all, NEG)
    q_far = jnp.concatenate([jnp.concatenate([neg_far.astype(BF16)] * NSA_J, axis=0), q2], axis=1)
    q_near = jnp.concatenate([jnp.concatenate([neg_all.astype(BF16)] * NSA_J, axis=0), q2], axis=1)
    n_far = (jnp.maximum(i - 1, 0) * QB + KT - 1) // KT

    def far_body(kt, carry):
        off = pl.multiple_of(PADF + kt * KT, KT)
        s = jnp.dot(q_far, sk_ref[0, 0, :, pl.ds(off, KT)], preferred_element_type=F32)
        return _flash_step(carry, s, sv_ref[0, 0, :, pl.ds(off, KT)])

    carry = lax.fori_loop(0, n_far, far_body, _flash_init(rows, NSA_QW))

    offn = pl.multiple_of(PADF + start - QB, QB)
    s = jnp.dot(q_near, sk_ref[0, 0, :, pl.ds(offn, 2 * QB)], preferred_element_type=F32)
    s = (s.reshape(NSA_J, QB, 2 * QB) + tn_ref[...]).reshape(rows, 2 * QB)
    _, acc = _flash_step(carry, s, sv_ref[0, 0, :, pl.ds(offn, 2 * QB)])
    o_slc = _flash_out(acc, NSA_DH)

    lw = WINDOW + QB
    offw = pl.multiple_of(PADF + start - WINDOW, QB)
    s = jnp.dot(q2, wk_ref[0, 0, :, pl.ds(offw, lw)], preferred_element_type=F32)
    s = (s.reshape(NSA_J, QB, lw) + tw_ref[...]).reshape(rows, lw)
    _, acc = _flash_step(_flash_init(rows, NSA_QW), s, wv_ref[0, 0, :, pl.ds(offw, lw)])
    o_win = _flash_out(acc, NSA_DH)

    gt = gates_ref[0]
    colid = lax.broadcasted_iota(I32, gt.shape, 1)
    heads = []
    for j in range(NSA_J):
        out = jnp.zeros((QB, NSA_DH), F32)
        for c, ob in enumerate((o_cmp, o_slc, o_win)):
            gcol = jnp.sum(jnp.where(colid == 3 * (NSA_J * g + j) + c, gt, 0.0), axis=-1, keepdims=True)
            out = out + gcol * ob[j * QB:(j + 1) * QB]
        heads.append(out.astype(o_ref.dtype))
    o_ref[0] = jnp.concatenate(heads, axis=1)


def _nsa_prompt_attention(q4, gates, ck, cv, sk, sv, wk, wv, t_win, t_near, cmat, n_sb):
    b, h, s, qw = q4.shape
    nq = s // QB
    per_bg = lambda a: pl.BlockSpec((1, 1) + a.shape[2:], lambda bi, g, i: (bi, g, 0, 0))
    return pl.pallas_call(
        functools.partial(_nsa_prompt_kernel, n_sb=n_sb),
        grid=(b, NSA_KV, nq),
        in_specs=[pl.BlockSpec((1, NSA_J, QB, qw), lambda bi, g, i: (bi, g, i, 0)),
                  pl.BlockSpec((1, QB, gates.shape[2]), lambda bi, g, i: (bi, i, 0)),
                  per_bg(ck), per_bg(cv), per_bg(sk), per_bg(sv), per_bg(wk), per_bg(wv),
                  pl.BlockSpec((NSA_J, QB, t_win.shape[2]), lambda bi, g, i: (g, 0, 0)),
                  pl.BlockSpec((NSA_J, QB, t_near.shape[2]), lambda bi, g, i: (g, 0, 0)),
                  pl.BlockSpec(cmat.shape, lambda bi, g, i: (0, 0))],
        out_specs=pl.BlockSpec((1, QB, NSA_J * NSA_DH), lambda bi, g, i: (bi, i, g)),
        out_shape=jax.ShapeDtypeStruct((b, s, h * NSA_DH), BF16),
        compiler_params=_cparams(("parallel", "parallel", "arbitrary")), name="nsa_prompt",
    )(q4, gates, ck, cv, sk, sv, wk, wv, t_win, t_near, cmat)


def _diff_lambda(lq1_ref, lk1_ref, lq2_ref, lk2_ref, lam_init):
    a = jnp.sum(lq1_ref[...] * lk1_ref[...], axis=-1, keepdims=True)
    b = jnp.sum(lq2_ref[...] * lk2_ref[...], axis=-1, keepdims=True)
    return jnp.exp(a) - jnp.exp(b) + lam_init


def _diff_prompt_kernel(q_ref, k_ref, v_ref, tn_ref, lq1_ref, lk1_ref, lq2_ref, lk2_ref, o_ref, *, lam_init):
    i = pl.program_id(2)
    start = i * QB
    rows = DIFF_J * QB
    far_end = jnp.maximum(start - QB, 0)
    n_full = far_end // KT_DIFF
    rem_start = n_full * KT_DIFF
    n_edge = (far_end - rem_start + KT - 1) // KT
    q2 = [q_ref[0, 0, m].reshape(rows, DIFF_DH) for m in range(2)]

    def far_step(carry, off, width, keep):
        v = v_ref[0, 0, :, pl.ds(off, width)]
        new = []
        for m in range(2):
            s = jnp.dot(q2[m], k_ref[0, 0, m, :, pl.ds(off, width)], preferred_element_type=F32)
            new.append(_flash_step(carry[m], s if keep is None else jnp.where(keep, s, NEG), v))
        return tuple(new)

    def full_body(kt, carry):
        return far_step(carry, pl.multiple_of(PADF + kt * KT_DIFF, PADF), KT_DIFF, None)

    def edge_body(h, carry):
        base = rem_start + h * KT
        keep = base + lax.broadcasted_iota(I32, (1, KT), 1) < far_end
        return far_step(carry, pl.multiple_of(PADF + base, KT), KT, keep)

    init = _flash_init(rows, DIFF_VW)
    carry = lax.fori_loop(0, n_full, full_body, (init, init))
    carry = lax.fori_loop(0, n_edge, edge_body, carry)
    offn = pl.multiple_of(PADF + start - QB, QB)
    v = v_ref[0, 0, :, pl.ds(offn, 2 * QB)]
    in_seq = start - QB + lax.broadcasted_iota(I32, (1, 2 * QB), 1) >= 0
    outs = []
    for m in range(2):
        s = jnp.dot(q2[m], k_ref[0, 0, m, :, pl.ds(offn, 2 * QB)], preferred_element_type=F32)
        s = (s.reshape(DIFF_J, QB, 2 * QB) + tn_ref[m]).reshape(rows, 2 * QB)
        _, acc = _flash_step(carry[m], jnp.where(in_seq, s, NEG), v)
        outs.append(_flash_out(acc, DIFF_DV))
    lam = _diff_lambda(lq1_ref, lk1_ref, lq2_ref, lk2_ref, lam_init)
    o = outs[0] - lam * outs[1]
    o_ref[0] = jnp.concatenate([o[j * QB:(j + 1) * QB] for j in range(DIFF_J)], axis=1)


def _diff_prompt_attention(q6, k5, v4, t_near, lams, lam_init):
    b, g, _, _, s, dh = q6.shape
    sp = k5.shape[4]
    nq = s // QB
    lam_spec = pl.BlockSpec((1, DIFF_DH), lambda bi, gi, i: (0, 0))
    return pl.pallas_call(
        functools.partial(_diff_prompt_kernel, lam_init=lam_init),
        grid=(b, g, nq),
        in_specs=[pl.BlockSpec((1, 1, 2, DIFF_J, QB, dh), lambda bi, gi, i: (bi, gi, 0, 0, i, 0)),
                  pl.BlockSpec((1, 1, 2, dh, sp), lambda bi, gi, i: (bi, gi, 0, 0, 0)),
                  pl.BlockSpec((1, 1, DIFF_VW, sp), lambda bi, gi, i: (bi, gi, 0, 0)),
                  pl.BlockSpec((None, 2, DIFF_J, QB, 2 * QB), lambda bi, gi, i: (gi, 0, 0, 0, 0)),
                  lam_spec, lam_spec, lam_spec, lam_spec],
        out_specs=pl.BlockSpec((1, QB, DIFF_J * DIFF_DV), lambda bi, gi, i: (bi, i, gi)),
        out_shape=jax.ShapeDtypeStruct((b, s, DIFF_HEADS * DIFF_DV), F32),
        compiler_params=_cparams(("parallel", "parallel", "arbitrary")), name="diff_prompt",
    )(q6, k5, v4, t_near, *lams)


def _nsa_samp_a_kernel(pt_ref, *refs, n_pp, n_pages, past, t_new, n_sb):
    page_refs = refs[:n_pp]
    (xc_ref, qbd_ref, perm_ref, bd1_ref, b1_ref, bd2_ref, b2_ref, cm_ref, ocmp_ref, sel_ref, p_ref) = refs[n_pp:]
    i = pl.program_id(1)
    nrow = n_pp * 8
    parts = _page_parts([r[...] for r in page_refs], perm_ref[...], bd1_ref)
    p_ref[pl.ds(pl.multiple_of(i * nrow, nrow), nrow), :] = parts

    @pl.when(i == pl.num_programs(1) - 1)
    def _():
        xc = xc_ref[0]
        accs = []
        for kv in range(2):
            for hp in range(NSA_KV // 2):
                c0 = kv * NSA_KVW + hp * 128
                lhs = jnp.concatenate([xc[s:s + 1, c0:c0 + 128] for s in range(CMP_STRIDE)], axis=1).astype(BF16)
                accs.append(jnp.dot(lhs, bd1_ref[kv], preferred_element_type=F32))
        n_cached = n_pages * 8
        p_ref[pl.ds(n_cached, 1), :] = jnp.concatenate(accs, axis=1)
        ck, cv = _compress_finalize(p_ref, n_cached, b1_ref, bd2_ref, b2_ref)
        qbd = qbd_ref[0]
        nrows = qbd.shape[0]
        lc = lax.dot_general(qbd, ck.astype(BF16), NT_DIMS, preferred_element_type=F32)
        qpos = past + lax.broadcasted_iota(I32, (nrows, 1), 0) % t_new
        c_end = lax.broadcasted_iota(I32, (1, n_cached), 1) * CMP_STRIDE + (CMP_BLOCK - 1)
        pc = _masked_softmax(lc, c_end <= qpos)
        ocmp_ref[0] = jnp.dot(pc.astype(BF16), cv.astype(BF16), preferred_element_type=F32)
        rg = nrows // NSA_J
        ps = pc[0:rg] + pc[rg:2 * rg] + pc[2 * rg:3 * rg] + pc[3 * rg:4 * rg]
        score = _dot_exact_rhs(ps, cm_ref[...])
        sel_ref[0] = _select_blocks(score, qpos[0:rg], n_sb, False)


def _page_specs(n_pp, page_shape):
    def spec(k):
        return pl.BlockSpec((None,) + page_shape, lambda b, i, pt, k=k: (pt[b, i * n_pp + k], 0, 0))
    return [spec(k) for k in range(n_pp)]


def _nsa_sample_a(page_table, cache, xc, qbd, cw, cmat, past, t_new, n_sb, n_pp):
    bd, n_pages = page_table.shape
    nrows = qbd.shape[1]
    rg = nrows // NSA_J
    nsbp = cmat.shape[1]
    const = lambda a: pl.BlockSpec(a.shape, lambda b, i, pt, _n=a.ndim: (0,) * _n)
    grid_spec = pltpu.PrefetchScalarGridSpec(
        num_scalar_prefetch=1, grid=(bd, n_pages // n_pp),
        in_specs=_page_specs(n_pp, cache.shape[1:]) + [
            pl.BlockSpec((1, CMP_STRIDE, xc.shape[2]), lambda b, i, pt: (b, 0, 0)),
            pl.BlockSpec((1, nrows, NSA_KVW), lambda b, i, pt: (b, 0, 0)),
            const(cw["perm"]), const(cw["bd1"]), const(cw["b1"]), const(cw["bd2"]), const(cw["b2"]), const(cmat)],
        out_specs=[pl.BlockSpec((1, nrows, NSA_KVW), lambda b, i, pt: (b, 0, 0)),
                   pl.BlockSpec((1, rg, nsbp), lambda b, i, pt: (b, 0, 0))],
        scratch_shapes=[pltpu.VMEM((n_pages * 8 + 8, 4 * NSA_KVW), F32)])
    return pl.pallas_call(
        functools.partial(_nsa_samp_a_kernel, n_pp=n_pp, n_pages=n_pages, past=past, t_new=t_new, n_sb=n_sb),
        grid_spec=grid_spec,
        out_shape=[jax.ShapeDtypeStruct((bd, nrows, NSA_KVW), F32), jax.ShapeDtypeStruct((bd, rg, nsbp), F32)],
        compiler_params=_cparams(("parallel", "arbitrary")), name="nsa_sample_a",
    )(page_table, *([cache] * n_pp), xc, qbd, cw["perm"], cw["bd1"], cw["b1"], cw["bd2"], cw["b2"], cmat)


def _nsa_samp_b_kernel(pt_ref, *refs, n_pp, n_pages):
    page_refs = refs[:n_pp]
    (qbd_ref, sel_ref, snew_ref, wst_ref, wnew_ref, ocmp_ref, gate_ref, tsn_ref, tsw_ref, fold_ref,
     o_ref, m_ref, l_ref, acc_ref) = refs[n_pp:]
    i = pl.program_id(1)

    @pl.when(i == 0)
    def _():
        m_ref[...] = jnp.full(m_ref.shape, NEG, F32)
        l_ref[...] = jnp.zeros(l_ref.shape, F32)
        acc_ref[...] = jnp.zeros(acc_ref.shape, F32)

    qbd = qbd_ref[0]
    sel = jnp.concatenate([sel_ref[0]] * NSA_J, axis=0)
    near0 = 2 * (n_pages - 1)
    blk = lax.broadcasted_iota(I32, (1, sel.shape[1]), 1)
    sel_far = jnp.where(blk < near0, sel, 0.0).astype(BF16)
    nk = n_pp * PAGE
    kcat = jnp.concatenate([r[0:NSA_KVW, :] for r in page_refs], axis=1).astype(BF16)
    vcat = jnp.concatenate([r[NSA_KVW:2 * NSA_KVW, :] for r in page_refs], axis=1).astype(BF16)
    s = jnp.dot(qbd, kcat, preferred_element_type=F32)
    keep = _block_expand(sel_far, i * (nk // SEL_BLOCK), nk) > 0.5
    carry = _online_update((m_ref[...], l_ref[...], acc_ref[...]), jnp.where(keep, s, NEG), _pv_t(vcat))
    m_ref[...], l_ref[...], acc_ref[...] = carry

    @pl.when(i == pl.num_programs(1) - 1)
    def _():
        last = page_refs[n_pp - 1]
        snew = snew_ref[0]
        kn = jnp.concatenate([last[0:NSA_KVW, :], snew[0:NSA_KVW, :]], axis=1).astype(BF16)
        vn = jnp.concatenate([last[NSA_KVW:2 * NSA_KVW, :], snew[NSA_KVW:2 * NSA_KVW, :]], axis=1).astype(BF16)
        s = jnp.dot(qbd, kn, preferred_element_type=F32) + tsn_ref[...]
        keep = _block_expand(sel.astype(BF16), near0, 2 * PAGE) > 0.5
        _, l, acc = _online_update((m_ref[...], l_ref[...], acc_ref[...]), jnp.where(keep, s, NEG), _pv_t(vn))
        o_slc = acc / l
        wst = wst_ref[0]
        wnew = wnew_ref[0]
        wk = jnp.concatenate([wst[0:NSA_KVW, :], wnew[0:NSA_KVW, :]], axis=1).astype(BF16)
        wv = jnp.concatenate([wst[NSA_KVW:2 * NSA_KVW, :], wnew[NSA_KVW:2 * NSA_KVW, :]], axis=1).astype(BF16)
        s = jnp.dot(qbd, wk, preferred_element_type=F32) + tsw_ref[...]
        o_win = _softmax_pv_t(s, wv)
        gt = gate_ref[0]
        o_full = gt[:, 0:1] * ocmp_ref[0] + gt[:, 1:2] * o_slc + gt[:, 2:3] * o_win
        nrows = o_full.shape[0]
        t_new = nrows // NSA_HEADS
        row_g = (lax.broadcasted_iota(I32, o_full.shape, 0) // t_new) % NSA_KV
        col_g = lax.broadcasted_iota(I32, o_full.shape, 1) // NSA_DH
        od = jnp.where(row_g == col_g, o_full, 0.0).astype(BF16)
        o_ref[0] = jnp.dot(od, fold_ref[...], preferred_element_type=F32).astype(o_ref.dtype)


def _nsa_sample_b(page_table, cache, qbd, sel, snew, wst, wnew, ocmp, gates, tsn, tsw, fold, n_pp):
    bd, n_pages = page_table.shape
    nrows = qbd.shape[1]
    per_b = lambda a: pl.BlockSpec((1,) + a.shape[1:], lambda b, i, pt, _n=a.ndim: (b,) + (0,) * (_n - 1))
    const = lambda a: pl.BlockSpec(a.shape, lambda b, i, pt, _n=a.ndim: (0,) * _n)
    grid_spec = pltpu.PrefetchScalarGridSpec(
        num_scalar_prefetch=1, grid=(bd, n_pages // n_pp),
        in_specs=_page_specs(n_pp, cache.shape[1:]) + [
            per_b(qbd), per_b(sel), per_b(snew), per_b(wst), per_b(wnew), per_b(ocmp), per_b(gates),
            const(tsn), const(tsw), const(fold)],
        out_specs=pl.BlockSpec((1, nrows, NSA_DH), lambda b, i, pt: (b, 0, 0)),
        scratch_shapes=[pltpu.VMEM((nrows, 1), F32), pltpu.VMEM((nrows, 1), F32), pltpu.VMEM((nrows, NSA_KVW), F32)])
    return pl.pallas_call(
        functools.partial(_nsa_samp_b_kernel, n_pp=n_pp, n_pages=n_pages),
        grid_spec=grid_spec,
        out_shape=jax.ShapeDtypeStruct((bd, nrows, NSA_DH), BF16),
        compiler_params=_cparams(("parallel", "arbitrary")), name="nsa_sample_b",
    )(page_table, *([cache] * n_pp), qbd, sel, snew, wst, wnew, ocmp, gates, tsn, tsw, fold)


def _diff_samp_kernel(pt_ref, *refs, n_pp, lam_init):
    page_refs = refs[:n_pp]
    (qbd_ref, dnew_ref, tdn_ref, lq1_ref, lk1_ref, lq2_ref, lk2_ref, o_ref, m_ref, l_ref, acc_ref) = refs[n_pp:]
    i = pl.program_id(1)
    last_step = i == pl.num_programs(1) - 1
    kw = DIFF_KV * 2 * DIFF_DH

    @pl.when(i == 0)
    def _():
        m_ref[...] = jnp.full(m_ref.shape, NEG, F32)
        l_ref[...] = jnp.zeros(l_ref.shape, F32)
        acc_ref[...] = jnp.zeros(acc_ref.shape, F32)

    qbd = qbd_ref[0]
    rg = qbd.shape[0] // DIFF_KV
    nk = n_pp * PAGE
    groups = range(DIFF_KV)

    def k_rows(ref, g):
        return ref[pl.ds(g, PAGE, stride=2 * DIFF_KV), :]

    def v_rows(ref, g):
        return ref[pl.ds(DIFF_KV + g, PAGE, stride=2 * DIFF_KV), :]

    def scores(k_of):
        return jnp.concatenate([lax.dot_general(qbd[g * rg:(g + 1) * rg], k_of(g), NT_DIMS,
                                                preferred_element_type=F32) for g in groups], axis=0)

    def pv(v_of):
        return lambda p: jnp.concatenate([jnp.dot(p[g * rg:(g + 1) * rg], v_of(g), preferred_element_type=F32)
                                          for g in groups], axis=0)

    s = scores(lambda g: jnp.concatenate([k_rows(r, g) for r in page_refs], axis=0).astype(BF16))
    limit = jnp.where(last_step, nk - PAGE, nk)
    s = jnp.where(lax.broadcasted_iota(I32, (1, nk), 1) < limit, s, NEG)
    carry = _online_update((m_ref[...], l_ref[...], acc_ref[...]), s,
                           pv(lambda g: jnp.concatenate([v_rows(r, g) for r in page_refs], axis=0).astype(BF16)))
    m_ref[...], l_ref[...], acc_ref[...] = carry

    @pl.when(last_step)
    def _():
        last = page_refs[n_pp - 1]
        dnew = dnew_ref[0]
        s = scores(lambda g: jnp.concatenate(
            [k_rows(last, g), dnew[:, g * DIFF_DV:(g + 1) * DIFF_DV]], axis=0).astype(BF16)) + tdn_ref[...]
        _, l, acc = _online_update((m_ref[...], l_ref[...], acc_ref[...]), s, pv(lambda g: jnp.concatenate(
            [v_rows(last, g), dnew[:, kw + g * DIFF_DV:kw + (g + 1) * DIFF_DV]], axis=0).astype(BF16)))
        o_full = acc / l
        lam = _diff_lambda(lq1_ref, lk1_ref, lq2_ref, lk2_ref, lam_init)
        h = rg // 2
        o_ref[0] = jnp.concatenate([o_full[g * rg:g * rg + h] - lam * o_full[g * rg + h:(g + 1) * rg]
                                    for g in groups], axis=0)


def _diff_sample(page_table, cache, qbd, dnew, tdn, lams, lam_init, n_pp):
    bd, n_pages = page_table.shape
    nrows = qbd.shape[1]
    kw = DIFF_KV * 2 * DIFF_DH
    per_b = lambda a: pl.BlockSpec((1,) + a.shape[1:], lambda b, i, pt, _n=a.ndim: (b,) + (0,) * (_n - 1))
    const = lambda a: pl.BlockSpec(a.shape, lambda b, i, pt, _n=a.ndim: (0,) * _n)
    grid_spec = pltpu.PrefetchScalarGridSpec(
        num_scalar_prefetch=1, grid=(bd, n_pages // n_pp),
        in_specs=_page_specs(n_pp, cache.shape[1:]) + [per_b(qbd), per_b(dnew), const(tdn)] + [const(a) for a in lams],
        out_specs=pl.BlockSpec((1, nrows // 2, DIFF_DV), lambda b, i, pt: (b, 0, 0)),
        scratch_shapes=[pltpu.VMEM((nrows, 1), F32), pltpu.VMEM((nrows, 1), F32), pltpu.VMEM((nrows, DIFF_DV), F32)])
    return pl.pallas_call(
        functools.partial(_diff_samp_kernel, n_pp=n_pp, lam_init=lam_init),
        grid_spec=grid_spec,
        out_shape=jax.ShapeDtypeStruct((bd, nrows // 2, DIFF_DV), F32),
        compiler_params=_cparams(("parallel", "arbitrary")), name="diff_sample",
    )(page_table, *([cache] * n_pp), qbd, dnew, tdn, *lams)


def _route(lt):
    mx = jnp.max(lt, axis=0, keepdims=True)
    e = jnp.exp(lt - mx)
    p = e / jnp.sum(e, axis=0, keepdims=True)
    best = None
    grp = None
    for g in range(N_GROUPS):
        r = [p[EPG * g + k:EPG * g + k + 1] for k in range(EPG)]
        a, b = jnp.maximum(r[0], r[1]), jnp.minimum(r[0], r[1])
        c, d = jnp.maximum(r[2], r[3]), jnp.minimum(r[2], r[3])
        sc = jnp.maximum(a, c) + jnp.maximum(jnp.minimum(a, c), jnp.maximum(b, d))
        if g == 0:
            best, grp = sc, jnp.zeros(sc.shape, I32)
        else:
            better = sc > best
            grp = jnp.where(better, g, grp)
            best = jnp.where(better, sc, best)
    lg = []
    for k in range(EPG):
        v = lt[k:k + 1]
        for g in range(1, N_GROUPS):
            v = jnp.where(grp == g, lt[EPG * g + k:EPG * g + k + 1], v)
        lg.append(v)

    def first_max(vals):
        vmax = jnp.maximum(jnp.maximum(vals[0], vals[1]), jnp.maximum(vals[2], vals[3]))
        idx = jnp.where(vals[0] == vmax, 0, jnp.where(vals[1] == vmax, 1, jnp.where(vals[2] == vmax, 2, 3)))
        return vmax, idx

    v1, i1 = first_max(lg)
    rest = [jnp.where(i1 == k, -jnp.inf, lg[k]) for k in range(EPG)]
    v2, i2 = first_max(rest)
    e2 = jnp.exp(v2 - v1)
    den = 1.0 + e2
    return grp * EPG + i1, grp * EPG + i2, 1.0 / den, e2 / den


def _post_attn_kernel(o_ref, x_ref, w_ref, g_ref, b_ref, wr_ref, rb_ref, gain_ref, x1_ref, eidx_ref, gate_ref, *,
                      merge_scale):
    o = o_ref[...]
    if merge_scale is not None:
        segs = []
        for h in range(DIFF_HEADS):
            seg = o[:, h * DIFF_DV:(h + 1) * DIFF_DV]
            ms = jnp.mean(seg * seg, axis=-1, keepdims=True)
            segs.append(seg * lax.rsqrt(ms + LN_EPS) * gain_ref[...] * merge_scale)
        o = jnp.concatenate(segs, axis=1)
    mix = jnp.dot(o.astype(BF16), w_ref[...], preferred_element_type=F32)
    x1 = _layer_norm(DN_ALPHA * x_ref[...] + mix, g_ref[...], b_ref[...])
    x1_ref[...] = x1
    lt = lax.dot_general(wr_ref[...], x1, NT_DIMS, precision=HI, preferred_element_type=F32) + rb_ref[...]
    e1, e2, g1, g2 = _route(lt)
    eidx_ref[0:1, :] = e1
    eidx_ref[1:2, :] = e2
    gate_ref[0:1, :] = g1
    gate_ref[1:2, :] = g2


def _post_attention(o, x, w_out, ln_g, ln_b, wr_t, rb, gain, merge_scale):
    nt, d = x.shape
    tm = _token_tile(nt)
    row = lambda i: (i, 0)
    const = lambda a: pl.BlockSpec(a.shape, lambda i: (0, 0))
    return pl.pallas_call(
        functools.partial(_post_attn_kernel, merge_scale=merge_scale),
        grid=(nt // tm,),
        in_specs=[pl.BlockSpec((tm, d), row), pl.BlockSpec((tm, d), row), const(w_out), const(ln_g), const(ln_b),
                  const(wr_t), const(rb), const(gain)],
        out_specs=[pl.BlockSpec((tm, d), row), pl.BlockSpec((TOP_K, tm), lambda i: (0, i)),
                   pl.BlockSpec((TOP_K, tm), lambda i: (0, i))],
        out_shape=[jax.ShapeDtypeStruct((nt, d), F32), jax.ShapeDtypeStruct((TOP_K, nt), I32),
                   jax.ShapeDtypeStruct((TOP_K, nt), F32)],
        compiler_params=_cparams(("parallel",)), name="post_attn",
    )(o, x, w_out, ln_g, ln_b, wr_t, rb, gain)


def _start_rows(idx_ref, src_hbm, buf, slot, sem, n):
    def body(r0, c):
        for k in range(DMA_UNROLL):
            r = r0 * DMA_UNROLL + k
            pltpu.make_async_copy(src_hbm.at[pl.ds(idx_ref[0, r], 1), :], buf.at[slot, pl.ds(r, 1), :],
                                  sem.at[slot]).start(priority=k % 2)
        return c
    lax.fori_loop(0, n // DMA_UNROLL, body, 0)


def _wait_rows(src_hbm, buf, slot, sem, n):
    pltpu.make_async_copy(src_hbm.at[pl.ds(0, n), :], buf.at[slot], sem.at[slot]).wait()


def _moe_kernel(blk_e_ref, nv_ref, tok0_ref, tokn_ref, x_hbm, wgu_ref, wdn_ref, o_ref, xbuf, sem):
    i = pl.program_id(0)
    nv = nv_ref[0]
    tm = xbuf.shape[1]
    de = wdn_ref.shape[1]

    @pl.when(i == 0)
    def _():
        _start_rows(tok0_ref, x_hbm, xbuf, 0, sem, tm)

    @pl.when(i + 1 < nv)
    def _():
        _start_rows(tokn_ref, x_hbm, xbuf, (i + 1) % 2, sem, tm)

    @pl.when(i < nv)
    def _():
        slot = i % 2
        _wait_rows(x_hbm, xbuf, slot, sem, tm)
        xb = xbuf[slot].astype(BF16)
        acc = jnp.zeros(o_ref.shape, F32)
        half = de // 2
        for c in range(2):
            gate = jnp.dot(xb, wgu_ref[0, :, c * half:(c + 1) * half].astype(BF16), preferred_element_type=F32)
            up = jnp.dot(xb, wgu_ref[0, :, de + c * half:de + (c + 1) * half].astype(BF16),
                         preferred_element_type=F32)
            act = (gate * jax.nn.sigmoid(gate) * up).astype(BF16)
            acc = acc + jnp.dot(act, wdn_ref[0, c * half:(c + 1) * half, :].astype(BF16),
                                preferred_element_type=F32)
        o_ref[...] = acc

    @pl.when(i >= nv)
    def _():
        o_ref[...] = jnp.zeros(o_ref.shape, F32)


def _moe_experts(x1, tok_blocks, blk_e, n_valid, w_gu, w_dn):
    nt, d = x1.shape
    n_blocks, _, tm = tok_blocks.shape
    grid_spec = pltpu.PrefetchScalarGridSpec(
        num_scalar_prefetch=2, grid=(n_blocks,),
        in_specs=[pl.BlockSpec((None, 1, tm), lambda i, be, nv: (i, 0, 0), memory_space=pltpu.SMEM),
                  pl.BlockSpec((None, 1, tm), lambda i, be, nv: (jnp.minimum(i + 1, n_blocks - 1), 0, 0),
                               memory_space=pltpu.SMEM),
                  pl.BlockSpec(memory_space=pl.ANY),
                  pl.BlockSpec((1,) + w_gu.shape[1:], lambda i, be, nv: (be[i], 0, 0)),
                  pl.BlockSpec((1,) + w_dn.shape[1:], lambda i, be, nv: (be[i], 0, 0))],
        out_specs=pl.BlockSpec((tm, d), lambda i, be, nv: (i, 0)),
        scratch_shapes=[pltpu.VMEM((2, tm, d), F32), pltpu.SemaphoreType.DMA((2,))])
    return pl.pallas_call(
        _moe_kernel, grid_spec=grid_spec,
        out_shape=jax.ShapeDtypeStruct((n_blocks * tm, d), F32),
        compiler_params=_cparams(("arbitrary",)), name="moe_experts",
    )(blk_e, n_valid, tok_blocks, tok_blocks, x1, w_gu, w_dn)


def _moe_combine_kernel(pos0_ref, posn_ref, y_hbm, x1_ref, gate_ref, g_ref, b_ref, o_ref, ybuf, sem):
    i = pl.program_id(0)
    n = pl.num_programs(0)
    tm = x1_ref.shape[0]

    @pl.when(i == 0)
    def _():
        _start_rows(pos0_ref, y_hbm, ybuf, 0, sem, TOP_K * tm)

    @pl.when(i + 1 < n)
    def _():
        _start_rows(posn_ref, y_hbm, ybuf, (i + 1) % 2, sem, TOP_K * tm)

    slot = i % 2
    _wait_rows(y_hbm, ybuf, slot, sem, TOP_K * tm)
    gt = gate_ref[...]
    f = gt[:, 0:1] * ybuf[slot, pl.ds(0, tm), :] + gt[:, 1:2] * ybuf[slot, pl.ds(tm, tm), :]
    o_ref[...] = _layer_norm(DN_ALPHA * x1_ref[...] + f, g_ref[...], b_ref[...])


def _moe_combine(y_sorted, pos_blocks, x1, gates_t, ln_g, ln_b):
    nt, d = x1.shape
    n_tiles, _, tm2 = pos_blocks.shape
    tm = tm2 // TOP_K
    const = lambda a: pl.BlockSpec(a.shape, lambda i: (0, 0))
    return pl.pallas_call(
        _moe_combine_kernel, grid=(n_tiles,),
        in_specs=[pl.BlockSpec((None, 1, tm2), lambda i: (i, 0, 0), memory_space=pltpu.SMEM),
                  pl.BlockSpec((None, 1, tm2), lambda i: (jnp.minimum(i + 1, n_tiles - 1), 0, 0),
                               memory_space=pltpu.SMEM),
                  pl.BlockSpec(memory_space=pl.ANY),
                  pl.BlockSpec((tm, d), lambda i: (i, 0)), pl.BlockSpec((tm, TOP_K), lambda i: (i, 0)),
                  const(ln_g), const(ln_b)],
        out_specs=pl.BlockSpec((tm, d), lambda i: (i, 0)),
        out_shape=jax.ShapeDtypeStruct((nt, d), F32),
        scratch_shapes=[pltpu.VMEM((2, tm2, d), F32), pltpu.SemaphoreType.DMA((2,))],
        compiler_params=_cparams(("arbitrary",)), name="moe_combine",
    )(pos_blocks, pos_blocks, y_sorted, x1, gates_t, ln_g, ln_b)


def _moe_layer(x1, eidx, gate, w_gu, w_dn, ln_g, ln_b):
    nt, _ = x1.shape
    a = nt * TOP_K
    flat_e = eidx.T.reshape(a)
    onehot = (flat_e[:, None] == jnp.arange(N_EXPERTS, dtype=I32)[None, :]).astype(I32)
    running = jnp.cumsum(onehot, axis=0)
    sizes = running[-1]
    rank = jnp.sum(onehot * running, axis=1) - 1
    padded = (sizes + MOE_TM - 1) // MOE_TM * MOE_TM
    pad_end = jnp.cumsum(padded)
    pad_start = pad_end - padded
    pos = (jnp.sum(onehot * pad_start[None, :], axis=1) + rank).astype(I32)
    n_blocks = -(-a // MOE_TM) + N_EXPERTS
    tok_pad = jnp.zeros((n_blocks * MOE_TM,), I32).at[pos].set(jnp.arange(a, dtype=I32) // TOP_K)
    blk_start = jnp.arange(n_blocks, dtype=I32)[:, None] * MOE_TM
    blk_e = jnp.minimum(jnp.sum((pad_end[None, :] <= blk_start).astype(I32), axis=1), N_EXPERTS - 1).astype(I32)
    n_valid = (pad_end[-1:] // MOE_TM).astype(I32)
    y_sorted = _moe_experts(x1, tok_pad.reshape(n_blocks, 1, MOE_TM), blk_e, n_valid, w_gu, w_dn)
    tm = _token_tile(nt)
    pos_blocks = pos.reshape(nt // tm, tm, TOP_K).transpose(0, 2, 1).reshape(nt // tm, 1, TOP_K * tm)
    return _moe_combine(y_sorted, pos_blocks, x1, gate.T, ln_g, ln_b)


def _compress_weights(cmp_k, cmp_v):
    eye = jnp.eye(NSA_KV, dtype=F32)
    eye2 = jnp.eye(2, dtype=F32)
    bd1, b1, bd2, b2 = [], [], [], []
    for (w1, bias1, w2, bias2) in (cmp_k, cmp_v):
        w1r = w1.reshape(CMP_SPAN, CMP_STRIDE, NSA_DH, NSA_DH)
        bd1.append(jnp.einsum("rsdh,ab->sadrbh", w1r, eye2).reshape(CMP_STRIDE * 2 * NSA_DH, CMP_SPAN * 2 * NSA_DH))
        bd2.append(jnp.einsum("hd,ab->ahbd", w2, eye).reshape(NSA_KVW, NSA_KVW))
        b1.append(jnp.tile(bias1, NSA_KV)[None])
        b2.append(jnp.tile(bias2, NSA_KV)[None])
    r = np.arange(PAGE)
    perm = np.zeros((PAGE, PAGE), np.float32)
    perm[r, CMP_STRIDE * (r % 8) + r // 8] = 1.0
    return {"perm": jnp.asarray(perm, BF16), "bd1": jnp.stack(bd1).astype(BF16), "b1": jnp.stack(b1),
            "bd2": jnp.stack(bd2).astype(BF16), "b2": jnp.stack(b2)}


def _chunk_score_matrix(n_blocks_in, n_sb):
    lanes = -(-n_sb // 128) * 128
    n = np.arange(n_blocks_in)[:, None]
    b = np.arange(lanes)[None, :]
    m = ((n >= 4 * b) & (n <= 4 * b + 3)).astype(np.float32) + ((n + 1 >= 4 * b) & (n + 1 <= 4 * b + 3)).astype(np.float32)
    m = np.where(b < n_sb, m, 0.0)
    return jnp.asarray(m, BF16)


def kernel(x_prompt, x_sample, cache_cmp_kv, cache_slc_kv, state_win_kv, cache_diff_kv, page_table,
           nsa_w_in, nsa_w_out, cmp_k_w1, cmp_k_b1, cmp_k_w2, cmp_k_b2, cmp_v_w1, cmp_v_b1, cmp_v_w2, cmp_v_b2,
           diff_w_in, diff_w_out, lambda_q1, lambda_k1, lambda_q2, lambda_k2, diff_subln_gain, rel_bias,
           ln_gain, ln_bias, router_w, router_b, moe_w_gate_up, moe_w_down):
    b, s, d = x_prompt.shape
    bd, t_new, _ = x_sample.shape
    n_pages = page_table.shape[1]
    past = n_pages * PAGE
    assert s % KT == 0 and s >= WINDOW and past >= WINDOW and state_win_kv.shape[1] == WINDOW
    assert t_new <= CMP_STRIDE and (NSA_HEADS * t_new) % 8 == 0 and (DIFF_J * t_new) % 8 == 0
    n_p = b * s
    n_s = bd * t_new
    x_p = x_prompt.reshape(n_p, d)
    x_s = x_sample.reshape(n_s, d)

    hq = NSA_HEADS * NSA_DH
    wr_t = router_w.T
    rb = router_b[:, None]
    w_gu = moe_w_gate_up
    w_dn = moe_w_down
    t640 = _toeplitz_bias(rel_bias, QB, WINDOW + QB, WINDOW, 0, WINDOW, False)
    t_near = _toeplitz_bias(rel_bias, QB, 2 * QB, QB, 0, 1 << 30, True)
    n_pp = 16 if n_pages % 16 == 0 else 8
    assert n_pages % n_pp == 0

    w = nsa_w_in.astype(BF16)
    wg = jnp.pad(w[:, hq + 6 * NSA_KVW:], ((0, 0), (0, 128 - 3 * NSA_HEADS)))
    kv_w = [w[:, hq + 2 * k * NSA_KVW: hq + 2 * (k + 1) * NSA_KVW] for k in range(3)]
    f5 = (BF16, F32, F32, F32, F32)
    q4, cmp_t, slc_t, win_t, gates_p = _project(
        x_p, [w[:, :hq]] + [a.T for a in kv_w] + [wg],
        ("q_heads_flag", "transposed", "transposed", "transposed", "sigmoid"), f5, s)
    q_s, cmp_s, slc_s, win_s, gates_s = _project(x_s, [w[:, :hq]] + kv_w + [wg],
                                                 ("qscale", None, None, None, "sigmoid"), f5)
    cw = _compress_weights((cmp_k_w1, cmp_k_b1, cmp_k_w2, cmp_k_b2), (cmp_v_w1, cmp_v_b1, cmp_v_w2, cmp_v_b2))

    kvw = 2 * NSA_KVW
    ck, cv = _compress_prompt(cmp_t, cw)

    def per_group(a):
        return a.reshape(b, a.shape[1], NSA_KV, NSA_DH).transpose(0, 2, 1, 3)

    n_sb_p = s // SEL_BLOCK
    sp = s + 2 * PADF
    pos = np.arange(sp) - PADF
    inside = (pos >= 0) & (pos < s)
    blk_rows = -(-n_sb_p // 128) * 128
    onehot = (pos[None, :] // SEL_BLOCK == np.arange(blk_rows)[:, None]) & inside[None, :]
    flag_rows = np.zeros((AUG, sp), np.float32)
    flag_rows[0] = ~inside
    ones_rows = np.zeros((AUG, sp), np.float32)
    ones_rows[0] = 1.0

    def per_bg(a):
        return jnp.broadcast_to(jnp.asarray(a, BF16), (b, NSA_KV) + a.shape)

    def kv_t(rows_t):
        r = jnp.pad(rows_t.astype(BF16), ((0, 0), (0, 0), (PADF, PADF))).reshape(b, 2, NSA_KV, NSA_DH, sp)
        return r[:, 0], r[:, 1]

    skt, svt = kv_t(slc_t)
    wkt, wvt = kv_t(win_t)
    sk = jnp.concatenate([per_bg(onehot), skt, per_bg(flag_rows)], axis=2)
    sv = jnp.concatenate([svt, per_bg(ones_rows)], axis=2)
    wk = jnp.concatenate([wkt, per_bg(flag_rows)], axis=2)
    wv = jnp.concatenate([wvt, per_bg(ones_rows)], axis=2)
    ck4 = jnp.pad(per_group(ck), ((0, 0), (0, 0), (0, 0), (0, AUG)))
    o4 = _nsa_prompt_attention(q4, gates_p.reshape(b, s, 128), ck4, per_group(cv), sk, sv, wk, wv,
                               t640, t_near, _chunk_score_matrix(s // CMP_STRIDE, n_sb_p), n_sb_p)
    o_p = o4.reshape(n_p, hq)

    nrows = NSA_HEADS * t_new
    n_sb_s = -(-(past + t_new) // SEL_BLOCK)
    qs = q_s.reshape(bd, t_new, NSA_KV, NSA_J, NSA_DH).transpose(0, 3, 2, 1, 4)
    qbd = jnp.einsum("bjgtd,gh->bjgthd", qs, jnp.eye(NSA_KV, dtype=BF16)).reshape(bd, nrows, NSA_KVW)

    def native_t(a):
        return a.transpose(0, 2, 3, 4, 1).reshape(a.shape[0], kvw, a.shape[1])

    def new_rows_t(rows):
        return jnp.pad(rows.reshape(bd, t_new, kvw).transpose(0, 2, 1), ((0, 0), (0, 0), (0, PAGE - t_new)))

    xc = jnp.pad(cmp_s.reshape(bd, t_new, kvw), ((0, 0), (0, CMP_STRIDE - t_new), (0, 0)))
    ocmp, sel = _nsa_sample_a(page_table, native_t(cache_cmp_kv), xc, qbd, cw,
                              _chunk_score_matrix(n_pages * 8, n_sb_s), past, t_new, n_sb_s, n_pp)
    head_of_row = (np.arange(nrows) // t_new % NSA_KV) * NSA_J + np.arange(nrows) // (t_new * NSA_KV)
    tok_of_row = np.arange(nrows) % t_new
    tsn = _toeplitz_bias(rel_bias, t_new, 2 * PAGE, PAGE, 0, 1 << 30, True)[head_of_row, tok_of_row]
    tsw_full = _toeplitz_bias(rel_bias, t_new, WINDOW + PAGE, WINDOW, 0, WINDOW, False)
    tsw_full = jnp.where(jnp.arange(WINDOW + PAGE)[None, None, :] < WINDOW + t_new, tsw_full, NEG)
    tsw = tsw_full[head_of_row, tok_of_row]
    gs = gates_s[:, :3 * NSA_HEADS].reshape(bd, t_new, NSA_KV, NSA_J, 3).transpose(0, 3, 2, 1, 4).reshape(bd, nrows, 3)
    fold = jnp.asarray(np.tile(np.eye(NSA_DH, dtype=np.float32), (NSA_KV, 1)), BF16)
    o_s = _nsa_sample_b(page_table, native_t(cache_slc_kv), qbd, sel, new_rows_t(slc_s), native_t(state_win_kv),
                        new_rows_t(win_s), ocmp, gs, tsn, tsw, fold, n_pp)
    o_s = o_s.reshape(bd, NSA_J, NSA_KV, t_new, NSA_DH).transpose(0, 3, 2, 1, 4).reshape(n_s, hq)

    ones = jnp.ones((1, DIFF_DV), F32)
    x_all = jnp.concatenate([x_p, x_s], axis=0)
    x1, eidx, gate = _post_attention(jnp.concatenate([o_p, o_s], axis=0), x_all, nsa_w_out.astype(BF16),
                                     ln_gain[0, 0][None], ln_bias[0, 0][None], wr_t, rb, ones, None)
    x_all = _moe_layer(x1, eidx, gate, w_gu[0], w_dn[0], ln_gain[0, 1][None], ln_bias[0, 1][None])

    lam_init = 0.8 - 0.6 * math.exp(-0.3 * 1)
    lams = [a[None] for a in (lambda_q1, lambda_k1, lambda_q2, lambda_k2)]
    wd = diff_w_in.astype(BF16)
    dq = DIFF_HEADS * 2 * DIFF_DH
    dkw = DIFF_KV * DIFF_DV
    wq_gmj = wd[:, :dq].reshape(d, DIFF_KV, DIFF_J, 2, DIFF_DH).transpose(0, 1, 3, 2, 4).reshape(d, dq)
    q6, diff_p, diff_t = _project(x_all, [wq_gmj, wd[:, dq:], wd[:, dq:].T],
                                  ("q_heads", "split128", "transposed"), (BF16, F32, BF16), s, (0, n_p))
    q6 = q6.reshape(b, DIFF_KV, 2, DIFF_J, s, DIFF_DH)
    qd_s, diff_s = _project(x_all, [wd[:, :dq], wd[:, dq:]], ("qscale", "split128"), (BF16, F32), None, (n_p, n_s))
    tn_d = t_near.reshape(DIFF_KV, DIFF_J, 2, QB, 2 * QB).transpose(0, 2, 1, 3, 4)
    spd = s + PADF + KT_DIFF
    dt = jnp.pad(diff_t, ((0, 0), (0, 0), (PADF, KT_DIFF)))
    k5 = dt[:, :dkw].reshape(b, DIFF_KV, 2, DIFF_DH, spd)
    ones_d = np.zeros((AUG, spd), np.float32)
    ones_d[0] = 1.0
    ones_d = jnp.broadcast_to(jnp.asarray(ones_d, BF16), (b, DIFF_KV, AUG, spd))
    v4 = jnp.concatenate([dt[:, dkw:].reshape(b, DIFF_KV, DIFF_DV, spd), ones_d], axis=2)
    o_p = _diff_prompt_attention(q6, k5, v4, tn_d, lams, lam_init).reshape(n_p, DIFF_HEADS * DIFF_DV)

    nrows_d = 2 * DIFF_HEADS * t_new
    qsd = qd_s.reshape(bd, t_new, DIFF_KV, DIFF_J, 2, DIFF_DH).transpose(0, 2, 4, 3, 1, 5)
    qbd_d = jnp.einsum("bgmjtd,mM->bgmjtMd", qsd, jnp.eye(2, dtype=BF16)).reshape(bd, nrows_d, 2 * DIFF_DH)
    r = np.arange(nrows_d)
    col_of_row = (r // (2 * DIFF_J * t_new)) * 4 + (r // t_new % DIFF_J) * 2 + r // (DIFF_J * t_new) % 2
    tdn_full = _toeplitz_bias(rel_bias, t_new, 2 * PAGE, PAGE, 0, 1 << 30, True)
    tdn_full = jnp.where(jnp.arange(2 * PAGE)[None, None, :] < PAGE + t_new, tdn_full, NEG)
    tdn = tdn_full[col_of_row, r % t_new]
    dnew = jnp.pad(diff_s.reshape(bd, t_new, 2 * dkw), ((0, 0), (0, PAGE - t_new), (0, 0)))
    cache_d = cache_diff_kv.reshape(-1, PAGE * 2 * DIFF_KV, DIFF_DV)
    o_s = _diff_sample(page_table, cache_d, qbd_d, dnew, tdn, lams, lam_init, n_pp)
    o_s = o_s.reshape(bd, DIFF_KV, DIFF_J, t_new, DIFF_DV).transpose(0, 3, 1, 2, 4).reshape(n_s, DIFF_HEADS * DIFF_DV)

    x1, eidx, gate = _post_attention(jnp.concatenate([o_p, o_s], axis=0), x_all, diff_w_out.astype(BF16),
                                     ln_gain[1, 0][None], ln_bias[1, 0][None], wr_t, rb, diff_subln_gain[None],
                                     1.0 - lam_init)
    x_all = _moe_layer(x1, eidx, gate, w_gu[1], w_dn[1], ln_gain[1, 1][None], ln_bias[1, 1][None])

    kv5 = (2, NSA_KV, NSA_DH)

    def rows_out(a_t):
        return a_t.reshape((b,) + kv5 + (a_t.shape[2],)).transpose(0, 4, 1, 2, 3)

    win_out_s = jnp.concatenate([state_win_kv, win_s.reshape((bd, t_new) + kv5)], axis=1)[:, t_new:]
    dshape = (2, DIFF_KV, DIFF_DV)
    return (x_all[:n_p].reshape(b, s, d), x_all[n_p:].reshape(bd, t_new, d),
            rows_out(cmp_t), cmp_s.reshape((bd, t_new) + kv5), rows_out(slc_t), slc_s.reshape((bd, t_new) + kv5),
            rows_out(win_t[:, :, s - WINDOW:]), win_out_s,
            diff_p.reshape((b, s) + dshape), diff_s.reshape((bd, t_new) + dshape))
```
